```python
import jax, jax.numpy as jnp
from jax import lax
import numpy as np

D_MODEL = 1024
BATCH = 16
SEQ = 2048
DEPTH = 1

HEAD_DIM = D_MODEL // 16
N_ATTN_HEADS = 8
N_KV_HEADS = 2
GQA_GROUP = N_ATTN_HEADS // N_KV_HEADS
WINDOW = 128
BLOCK = 128
N_CONV_GROUPS = 4
CONV_WIDTH = N_CONV_GROUPS * HEAD_DIM
CONV_K = 3
N_MEM_HEADS = 4
N_MEM = 256
ATTN_WIDTH = N_ATTN_HEADS * HEAD_DIM
KV_WIDTH = N_KV_HEADS * HEAD_DIM
MEM_WIDTH = N_MEM_HEADS * HEAD_DIM
MIX_WIDTH = ATTN_WIDTH + CONV_WIDTH + MEM_WIDTH
IN_PROJ_WIDTH = ATTN_WIDTH + 2 * KV_WIDTH + 3 * CONV_WIDTH + MEM_WIDTH
SPLIT_POINTS = (
    ATTN_WIDTH,
    ATTN_WIDTH + KV_WIDTH,
    ATTN_WIDTH + 2 * KV_WIDTH,
    ATTN_WIDTH + 2 * KV_WIDTH + CONV_WIDTH,
    ATTN_WIDTH + 2 * KV_WIDTH + 2 * CONV_WIDTH,
    ATTN_WIDTH + 2 * KV_WIDTH + 3 * CONV_WIDTH,
)
D_FF = ((8 * D_MODEL // 3 + 255) // 256) * 256
EPS = 1e-6
NEG_INF = -1e30

kernel_name = "hymba_conv_swa_memory_hybrid"


def rms_norm(x, g):
    xf = x.astype(jnp.float32)
    y = xf * lax.rsqrt(jnp.mean(xf * xf, axis=-1, keepdims=True) + EPS)
    return (y * g.astype(jnp.float32)).astype(x.dtype)


def alibi_slopes():
    return jnp.asarray(2.0 ** (-8.0 * np.arange(1, N_ATTN_HEADS + 1) / N_ATTN_HEADS), dtype=jnp.float32)


def sliding_window_attention(q, k, v, sinks):
    B, S = q.shape[0], q.shape[1]
    nb = S // BLOCK
    qb = q.reshape(B, nb, BLOCK, N_KV_HEADS, GQA_GROUP, HEAD_DIM)

    def band(t):
        tb = t.reshape(B, nb, BLOCK, N_KV_HEADS, HEAD_DIM)
        prev = jnp.pad(tb[:, :-1], ((0, 0), (1, 0), (0, 0), (0, 0), (0, 0)))
        return jnp.concatenate([prev, tb], axis=2)

    kb, vb = band(k), band(v)
    scores = jnp.einsum('bnqhgd,bnkhd->bnhgqk', qb, kb).astype(jnp.float32) * (HEAD_DIM ** -0.5)
    q_idx = jnp.arange(BLOCK)[:, None]
    k_idx = jnp.arange(2 * BLOCK)[None, :]
    dist = q_idx + BLOCK - k_idx
    key_pos = jnp.arange(nb)[:, None] * BLOCK - BLOCK + jnp.arange(2 * BLOCK)[None, :]
    valid = ((dist >= 0) & (dist < WINDOW))[None] & (key_pos >= 0)[:, None, :]
    slopes = alibi_slopes().reshape(N_KV_HEADS, GQA_GROUP)
    bias = -slopes[:, :, None, None] * dist.astype(jnp.float32)
    scores = jnp.where(valid[None, :, None, None], scores + bias[None, None], NEG_INF)
    sink = jnp.broadcast_to(sinks.astype(jnp.float32).reshape(1, 1, N_KV_HEADS, GQA_GROUP, 1, 1),
                            scores.shape[:-1] + (1,))
    probs = jax.nn.softmax(jnp.concatenate([scores, sink], axis=-1), axis=-1)[..., :-1]
    out = jnp.einsum('bnhgqk,bnkhd->bnqhgd', probs.astype(v.dtype), vb)
    return out.reshape(B, S, ATTN_WIDTH)


def short_gated_conv(h, b_gate, c_gate, conv_w, conv_b):
    S = h.shape[1]
    u = c_gate * h
    u_pad = jnp.pad(u, ((0, 0), (CONV_K - 1, 0), (0, 0)))
    conv = sum(conv_w[j] * u_pad[:, j:j + S] for j in range(CONV_K)) + conv_b
    return b_gate * conv


def memory_cross_attention(q, mem_n, w_mem_kv, mem_q_norm, mem_k_norm):
    B, S = q.shape[0], q.shape[1]
    q = rms_norm(q.reshape(B, S, N_MEM_HEADS, HEAD_DIM), mem_q_norm)
    kv = mem_n @ w_mem_kv
    k, v = jnp.split(kv, 2, axis=-1)
    k = rms_norm(k.reshape(B, -1, N_MEM_HEADS, HEAD_DIM), mem_k_norm)
    v = v.reshape(B, -1, N_MEM_HEADS, HEAD_DIM)
    scores = jnp.einsum('bshd,bmhd->bhsm', q, k).astype(jnp.float32) * (HEAD_DIM ** -0.5)
    probs = jax.nn.softmax(scores, axis=-1)
    out = jnp.einsum('bhsm,bmhd->bshd', probs.astype(v.dtype), v)
    return out.reshape(B, S, MEM_WIDTH)


def hybrid_mixer(xn, mem_n, w_in, q_norm, k_norm, attn_sinks, conv_w, conv_b, w_mem_kv,
                 mem_q_norm, mem_k_norm, out_norm_attn, out_norm_conv, out_norm_mem, w_out):
    B, S = xn.shape[0], xn.shape[1]
    proj = xn @ w_in
    q_a, k_a, v_a, c_h, c_b, c_c, q_m = jnp.split(proj, SPLIT_POINTS, axis=-1)
    q_a = rms_norm(q_a.reshape(B, S, N_ATTN_HEADS, HEAD_DIM), q_norm)
    k_a = rms_norm(k_a.reshape(B, S, N_KV_HEADS, HEAD_DIM), k_norm)
    v_a = v_a.reshape(B, S, N_KV_HEADS, HEAD_DIM)
    attn_out = sliding_window_attention(q_a, k_a, v_a, attn_sinks)
    conv_out = short_gated_conv(c_h, c_b, c_c, conv_w, conv_b)
    mem_out = memory_cross_attention(q_m, mem_n, w_mem_kv, mem_q_norm, mem_k_norm)
    merged = jnp.concatenate([rms_norm(attn_out, out_norm_attn),
                              rms_norm(conv_out, out_norm_conv),
                              rms_norm(mem_out, out_norm_mem)], axis=-1)
    return merged @ w_out


def swiglu_ffn(xn, w_gate, w_up, w_down):
    return (jax.nn.silu(xn @ w_gate) * (xn @ w_up)) @ w_down


def setup_inputs(seed: int = 0) -> dict:
    key = jax.random.key(seed)
    ks = jax.random.split(key, 24)
    f32 = jnp.float32

    def normal(k, shape, scale):
        return jax.random.normal(k, shape, f32) * scale

    def gain(k, shape):
        return 1.0 + 0.05 * jax.random.normal(k, shape, f32)

    L = DEPTH
    return {
        "x": jax.random.normal(ks[0], (BATCH, SEQ, D_MODEL), f32),
        "mem": jax.random.normal(ks[1], (BATCH, N_MEM, D_MODEL), f32),
        "norm_mix": gain(ks[2], (L, D_MODEL)),
        "w_in": normal(ks[3], (L, D_MODEL, IN_PROJ_WIDTH), D_MODEL ** -0.5),
        "q_norm": gain(ks[4], (L, HEAD_DIM)),
        "k_norm": gain(ks[5], (L, HEAD_DIM)),
        "attn_sinks": normal(ks[6], (L, N_ATTN_HEADS), 0.5),
        "conv_w": normal(ks[7], (L, CONV_K, CONV_WIDTH), CONV_K ** -0.5),
        "conv_b": normal(ks[8], (L, CONV_WIDTH), 0.01),
        "norm_mem": gain(ks[9], (L, D_MODEL)),
        "w_mem_kv": normal(ks[10], (L, D_MODEL, 2 * MEM_WIDTH), D_MODEL ** -0.5),
        "mem_q_norm": gain(ks[11], (L, HEAD_DIM)),
        "mem_k_norm": gain(ks[12], (L, HEAD_DIM)),
        "out_norm_attn": gain(ks[13], (L, ATTN_WIDTH)),
        "out_norm_conv": gain(ks[14], (L, CONV_WIDTH)),
        "out_norm_mem": gain(ks[15], (L, MEM_WIDTH)),
        "w_out": normal(ks[16], (L, MIX_WIDTH, D_MODEL), MIX_WIDTH ** -0.5),
        "norm_ffn": gain(ks[17], (L, D_MODEL)),
        "w_gate": normal(ks[18], (L, D_MODEL, D_FF), D_MODEL ** -0.5),
        "w_up": normal(ks[19], (L, D_MODEL, D_FF), D_MODEL ** -0.5),
        "w_down": normal(ks[20], (L, D_FF, D_MODEL), D_FF ** -0.5),
    }


def reference(x, mem, norm_mix, w_in, q_norm, k_norm, attn_sinks, conv_w, conv_b, norm_mem,
              w_mem_kv, mem_q_norm, mem_k_norm, out_norm_attn, out_norm_conv, out_norm_mem,
              w_out, norm_ffn, w_gate, w_up, w_down):
    for l in range(DEPTH):
        xn = rms_norm(x, norm_mix[l])
        mem_n = rms_norm(mem, norm_mem[l])
        x = x + hybrid_mixer(xn, mem_n, w_in[l], q_norm[l], k_norm[l], attn_sinks[l], conv_w[l],
                             conv_b[l], w_mem_kv[l], mem_q_norm[l], mem_k_norm[l],
                             out_norm_attn[l], out_norm_conv[l], out_norm_mem[l], w_out[l])
        x = x + swiglu_ffn(rms_norm(x, norm_ffn[l]), w_gate[l], w_up[l], w_down[l])
    return x
```

```python
import functools

import jax
import jax.numpy as jnp
import numpy as np
from jax import lax
from jax.experimental import pallas as pl
from jax.experimental.pallas import tpu as pltpu

D_MODEL = 1024
HEAD_DIM = 64
N_ATTN_HEADS = 8
N_KV_HEADS = 2
GQA_GROUP = N_ATTN_HEADS // N_KV_HEADS
BLOCK = 128
N_MEM_HEADS = 4
N_MEM = 256
CONV_K = 3
ATTN_WIDTH = N_ATTN_HEADS * HEAD_DIM
KV_WIDTH = N_KV_HEADS * HEAD_DIM
CONV_WIDTH = 256
MEM_WIDTH = N_MEM_HEADS * HEAD_DIM
IN_PROJ_WIDTH = ATTN_WIDTH + 2 * KV_WIDTH + 3 * CONV_WIDTH + MEM_WIDTH
GROUP_WIDTH = GQA_GROUP * HEAD_DIM
EPS = 1e-6
MASKED_DIST = 2.0 ** 110

V7X_VMEM_BYTES = 64 * 1024 * 1024
V7X_SUBLANES = 8

F32 = jnp.float32
BF16 = jnp.bfloat16


def _plan():
    seq_tile = 4 * BLOCK
    ffn_tile = 512
    vmem_limit = V7X_VMEM_BYTES - 8 * 1024 * 1024
    return seq_tile, ffn_tile, vmem_limit


def _rms(a, gain):
    return a * lax.rsqrt(jnp.mean(a * a, axis=-1, keepdims=True) + EPS) * gain


def _head_rms_scale(t, gmat):
    sq = (t * t).astype(BF16)
    ss = jnp.dot(sq, gmat, preferred_element_type=F32)
    return lax.rsqrt(ss * (1.0 / HEAD_DIM) + EPS)


def _dot_nt(a, b):
    return lax.dot_general(a, b, (((1,), (1,)), ((), ())), preferred_element_type=F32)


def _lane_head_scale(cols, width):
    rows = cols[0].shape[0]
    lane_head = lax.broadcasted_iota(jnp.int32, (rows, width), 1) // HEAD_DIM
    out = jnp.broadcast_to(cols[-1], (rows, width))
    for h in range(len(cols) - 2, -1, -1):
        out = jnp.where(lane_head == h, cols[h], out)
    return out


def _mem_kv_kernel(mem_ref, gain_ref, w_ref, kgain_ref, gmat_ref, kblk_ref, vblk_ref):
    m = mem_ref[0]
    mn = _rms(m, gain_ref[...]).astype(BF16)
    kv = jnp.dot(mn, w_ref[...], preferred_element_type=F32)
    k = kv[:, :MEM_WIDTH]
    v = kv[:, MEM_WIDTH:]
    kn = k * _head_rms_scale(k, gmat_ref[...]) * kgain_ref[...]
    lane_head = lax.broadcasted_iota(jnp.int32, (N_MEM, MEM_WIDTH), 1) // HEAD_DIM
    for h in range(N_MEM_HEADS):
        keep = lane_head == h
        kblk_ref[0, h * N_MEM:(h + 1) * N_MEM, :] = jnp.where(keep, kn, 0.0).astype(BF16)
        vblk_ref[0, h * N_MEM:(h + 1) * N_MEM, :] = jnp.where(keep, v, 0.0).astype(BF16)


def _group_blockdiag(t, group):
    lane = lax.broadcasted_iota(jnp.int32, t.shape, 1)
    swapped = pltpu.roll(t, HEAD_DIM, axis=1)
    if group == 0:
        dup = jnp.where(lane < HEAD_DIM, t, swapped)
    else:
        dup = jnp.where(lane < HEAD_DIM, swapped, t)
    dup2 = jnp.concatenate([dup, dup], axis=1)
    lane_head = lax.broadcasted_iota(jnp.int32, dup2.shape, 1) // HEAD_DIM
    blocks = [jnp.where(lane_head == h, dup2, 0.0).astype(BF16) for h in range(GQA_GROUP)]
    return jnp.concatenate(blocks, axis=0)


def _mixer_kernel(seq_tile,
                  sinks_ref, x_ref, kblk_ref, vblk_ref, win_ref, wout_ref, gmat_ref,
                  nmix_ref, qgain_ref, kgain_ref, mqgain_ref, convw_ref, convb_ref,
                  ona_ref, onc_ref, onm_ref,
                  out_ref,
                  kcar_ref, vcar_ref, upad_ref):
    j = pl.program_id(1)

    @pl.when(j == 0)
    def _():
        kcar_ref[...] = jnp.zeros_like(kcar_ref)
        vcar_ref[...] = jnp.zeros_like(vcar_ref)
        upad_ref[0:V7X_SUBLANES, :] = jnp.zeros((V7X_SUBLANES, CONV_WIDTH), F32)

    x = x_ref[0]
    xn = _rms(x, nmix_ref[...]).astype(BF16)
    proj = jnp.dot(xn, win_ref[...], preferred_element_type=F32)
    gmat = gmat_ref[...]

    o = 0
    q_a = proj[:, o:o + ATTN_WIDTH]; o += ATTN_WIDTH
    kv_a = proj[:, o:o + 2 * KV_WIDTH]; o += 2 * KV_WIDTH
    c_h = proj[:, o:o + CONV_WIDTH]; o += CONV_WIDTH
    c_b = proj[:, o:o + CONV_WIDTH]; o += CONV_WIDTH
    c_c = proj[:, o:o + CONV_WIDTH]; o += CONV_WIDTH
    q_m = proj[:, o:o + MEM_WIDTH]

    q_scale = jnp.concatenate(
        [_head_rms_scale(q_a[:, g * GROUP_WIDTH:(g + 1) * GROUP_WIDTH], gmat) for g in range(N_KV_HEADS)],
        axis=1)
    qn = (q_a * q_scale * (qgain_ref[...] * HEAD_DIM ** -0.5)).astype(BF16)
    kv_scale = _head_rms_scale(kv_a, gmat)
    kn = kv_a[:, :KV_WIDTH] * kv_scale[:, :KV_WIDTH] * kgain_ref[...]
    v_a = kv_a[:, KV_WIDTH:]
    qmn = (q_m * _head_rms_scale(q_m, gmat) * (mqgain_ref[...] * HEAD_DIM ** -0.5)).astype(BF16)

    row = lax.broadcasted_iota(jnp.int32, (BLOCK, BLOCK), 0)
    col = lax.broadcasted_iota(jnp.int32, (BLOCK, BLOCK), 1)
    from_prev = col > row
    dist = jnp.where(from_prev, row + BLOCK - col, row - col).astype(F32)
    dist_first = jnp.where(jnp.logical_and(from_prev, j == 0), MASKED_DIST, dist)

    kblk_prev = [_group_blockdiag(kcar_ref[...], g) for g in range(N_KV_HEADS)]
    vblk_prev = [_group_blockdiag(vcar_ref[...], g) for g in range(N_KV_HEADS)]
    attn_blocks = []
    n_blocks = seq_tile // BLOCK
    for b in range(n_blocks):
        rows = slice(b * BLOCK, (b + 1) * BLOCK)
        kb = kn[rows]
        vb = v_a[rows]
        dist_b = dist_first if b == 0 else dist
        outs = []
        for g in range(N_KV_HEADS):
            kblk_own = _group_blockdiag(kb, g)
            vblk_own = _group_blockdiag(vb, g)
            qg = qn[rows, g * GROUP_WIDTH:(g + 1) * GROUP_WIDTH]
            s_prev = _dot_nt(qg, kblk_prev[g])
            s_own = _dot_nt(qg, kblk_own)
            p_prev, p_own, inv_l = [], [], []
            for hh in range(GQA_GROUP):
                head = g * GQA_GROUP + hh
                slope = 2.0 ** (-8.0 * (head + 1) / N_ATTN_HEADS)
                sink = sinks_ref[head]
                lanes = slice(hh * BLOCK, (hh + 1) * BLOCK)
                s = jnp.where(from_prev, s_prev[:, lanes], s_own[:, lanes]) - slope * dist_b
                m = jnp.maximum(jnp.max(s, axis=-1, keepdims=True), sink)
                p = jnp.exp(s - m)
                l = jnp.sum(p, axis=-1, keepdims=True) + jnp.exp(sink - m)
                p_prev.append(jnp.where(from_prev, p, 0.0).astype(BF16))
                p_own.append(jnp.where(from_prev, 0.0, p).astype(BF16))
                inv_l.append(1.0 / l)
            pcat = jnp.concatenate(p_prev + p_own, axis=1)
            vcat = jnp.concatenate([vblk_prev[g], vblk_own], axis=0)
            og = jnp.dot(pcat, vcat, preferred_element_type=F32)
            outs.append(og * _lane_head_scale(inv_l, GROUP_WIDTH))
            kblk_prev[g] = kblk_own
            vblk_prev[g] = vblk_own
        attn_blocks.append(jnp.concatenate(outs, axis=1))
    attn_out = jnp.concatenate(attn_blocks, axis=0)
    kcar_ref[...] = kn[seq_tile - BLOCK:]
    vcar_ref[...] = v_a[seq_tile - BLOCK:]

    u = c_c * c_h
    upad_ref[V7X_SUBLANES:V7X_SUBLANES + seq_tile, :] = u
    u1 = upad_ref[V7X_SUBLANES - 1:V7X_SUBLANES - 1 + seq_tile, :]
    u2 = upad_ref[V7X_SUBLANES - 2:V7X_SUBLANES - 2 + seq_tile, :]
    cw = convw_ref[...]
    conv = cw[0:1] * u2 + cw[1:2] * u1 + cw[2:3] * u + convb_ref[...]
    conv_out = c_b * conv
    upad_ref[0:V7X_SUBLANES, :] = upad_ref[seq_tile:seq_tile + V7X_SUBLANES, :]

    kblk = kblk_ref[0]
    vblk = vblk_ref[0]
    mem_blocks = []
    for b in range(n_blocks):
        rows = slice(b * BLOCK, (b + 1) * BLOCK)
        s_all = _dot_nt(qmn[rows], kblk)
        ps, inv_l = [], []
        for h in range(N_MEM_HEADS):
            s = s_all[:, h * N_MEM:(h + 1) * N_MEM]
            m = jnp.max(s, axis=-1, keepdims=True)
            p = jnp.exp(s - m)
            inv_l.append(1.0 / jnp.sum(p, axis=-1, keepdims=True))
            ps.append(p.astype(BF16))
        om = jnp.dot(jnp.concatenate(ps, axis=1), vblk, preferred_element_type=F32)
        mem_blocks.append(om * _lane_head_scale(inv_l, MEM_WIDTH))
    mem_out = jnp.concatenate(mem_blocks, axis=0)

    merged = jnp.concatenate(
        [_rms(attn_out, ona_ref[...]), _rms(conv_out, onc_ref[...]), _rms(mem_out, onm_ref[...])],
        axis=1).astype(BF16)
    out_ref[0] = x + jnp.dot(merged, wout_ref[...], preferred_element_type=F32)


def _ffn_kernel(x_ref, gain_ref, wg_ref, wu_ref, wd_ref, out_ref):
    x = x_ref[...]
    h = _rms(x, gain_ref[...]).astype(BF16)
    gate = jnp.dot(h, wg_ref[...], preferred_element_type=F32)
    up = jnp.dot(h, wu_ref[...], preferred_element_type=F32)
    act = (gate * jax.nn.sigmoid(gate) * up).astype(BF16)
    out_ref[...] = x + jnp.dot(act, wd_ref[...], preferred_element_type=F32)


def _const_spec(shape):
    return pl.BlockSpec(shape, lambda *_: (0,) * len(shape), pipeline_mode=pl.Buffered(1))


def _layer(x, mem_blocks, lp, plan):
    seq_tile, ffn_tile, vmem_limit = plan
    batch, seq, _ = x.shape
    kblk, vblk = mem_blocks
    row = lambda a: a.reshape(1, -1)

    mixer = pl.pallas_call(
        functools.partial(_mixer_kernel, seq_tile),
        out_shape=jax.ShapeDtypeStruct(x.shape, F32),
        grid=(batch, seq // seq_tile),
        in_specs=[
            pl.BlockSpec(memory_space=pltpu.SMEM),
            pl.BlockSpec((1, seq_tile, D_MODEL), lambda b, j: (b, j, 0)),
            pl.BlockSpec((1, N_MEM_HEADS * N_MEM, MEM_WIDTH), lambda b, j: (b, 0, 0)),
            pl.BlockSpec((1, N_MEM_HEADS * N_MEM, MEM_WIDTH), lambda b, j: (b, 0, 0)),
            _const_spec((D_MODEL, IN_PROJ_WIDTH)),
            _const_spec((D_MODEL, D_MODEL)),
            _const_spec((GROUP_WIDTH, GROUP_WIDTH)),
            _const_spec((1, D_MODEL)),
            _const_spec((1, ATTN_WIDTH)),
            _const_spec((1, KV_WIDTH)),
            _const_spec((1, MEM_WIDTH)),
            _const_spec((CONV_K, CONV_WIDTH)),
            _const_spec((1, CONV_WIDTH)),
            _const_spec((1, ATTN_WIDTH)),
            _const_spec((1, CONV_WIDTH)),
            _const_spec((1, MEM_WIDTH)),
        ],
        out_specs=pl.BlockSpec((1, seq_tile, D_MODEL), lambda b, j: (b, j, 0)),
        scratch_shapes=[
            pltpu.VMEM((BLOCK, KV_WIDTH), F32),
            pltpu.VMEM((BLOCK, KV_WIDTH), F32),
            pltpu.VMEM((seq_tile + V7X_SUBLANES, CONV_WIDTH), F32),
        ],
        compiler_params=pltpu.CompilerParams(
            dimension_semantics=("arbitrary", "arbitrary"), vmem_limit_bytes=vmem_limit),
        name="mixer",
    )
    x1 = mixer(
        lp["attn_sinks"], x, kblk, vblk, lp["w_in"], lp["w_out"], lp["gmat"],
        row(lp["norm_mix"]), jnp.tile(row(lp["q_norm"]), (1, N_ATTN_HEADS)),
        jnp.tile(row(lp["k_norm"]), (1, N_KV_HEADS)), jnp.tile(row(lp["mem_q_norm"]), (1, N_MEM_HEADS)),
        lp["conv_w"], row(lp["conv_b"]),
        row(lp["out_norm_attn"]), row(lp["out_norm_conv"]), row(lp["out_norm_mem"]))

    tokens = batch * seq
    d_ff = lp["w_gate"].shape[1]
    ffn = pl.pallas_call(
        _ffn_kernel,
        out_shape=jax.ShapeDtypeStruct((tokens, D_MODEL), F32),
        grid=(tokens // ffn_tile,),
        in_specs=[
            pl.BlockSpec((ffn_tile, D_MODEL), lambda i: (i, 0)),
            _const_spec((1, D_MODEL)),
            _const_spec((D_MODEL, d_ff)),
            _const_spec((D_MODEL, d_ff)),
            _const_spec((d_ff, D_MODEL)),
        ],
        out_specs=pl.BlockSpec((ffn_tile, D_MODEL), lambda i: (i, 0)),
        compiler_params=pltpu.CompilerParams(
            dimension_semantics=("arbitrary",), vmem_limit_bytes=vmem_limit),
        name="ffn",
    )
    y = ffn(x1.reshape(tokens, D_MODEL), row(lp["norm_ffn"]), lp["w_gate"], lp["w_up"], lp["w_down"])
    return y.reshape(x.shape)


def _mem_kv(mem, lp, vmem_limit):
    batch = mem.shape[0]
    blk_shape = (batch, N_MEM_HEADS * N_MEM, MEM_WIDTH)
    call = pl.pallas_call(
        _mem_kv_kernel,
        out_shape=(jax.ShapeDtypeStruct(blk_shape, BF16), jax.ShapeDtypeStruct(blk_shape, BF16)),
        grid=(batch,),
        in_specs=[
            pl.BlockSpec((1, N_MEM, D_MODEL), lambda b: (b, 0, 0)),
            _const_spec((1, D_MODEL)),
            _const_spec((D_MODEL, 2 * MEM_WIDTH)),
            _const_spec((1, MEM_WIDTH)),
            _const_spec((GROUP_WIDTH, GROUP_WIDTH)),
        ],
        out_specs=(pl.BlockSpec((1,) + blk_shape[1:], lambda b: (b, 0, 0)),
                   pl.BlockSpec((1,) + blk_shape[1:], lambda b: (b, 0, 0))),
        compiler_params=pltpu.CompilerParams(
            dimension_semantics=("arbitrary",), vmem_limit_bytes=vmem_limit),
        name="mem_kv",
    )
    return call(mem, lp["norm_mem"].reshape(1, -1), lp["w_mem_kv"],
                jnp.tile(lp["mem_k_norm"].reshape(1, -1), (1, N_MEM_HEADS)), lp["gmat"])


def kernel(x, mem, norm_mix, w_in, q_norm, k_norm, attn_sinks, conv_w, conv_b, norm_mem, w_mem_kv,
           mem_q_norm, mem_k_norm, out_norm_attn, out_norm_conv, out_norm_mem, w_out, norm_ffn,
           w_gate, w_up, w_down):
    plan = _plan()
    head_of_lane = np.arange(GROUP_WIDTH) // HEAD_DIM
    gmat = jnp.asarray(head_of_lane[:, None] == head_of_lane[None, :], dtype=BF16)
    depth = w_in.shape[0]
    for l in range(depth):
        lp = dict(
            norm_mix=norm_mix[l], w_in=w_in[l].astype(BF16), q_norm=q_norm[l], k_norm=k_norm[l],
            attn_sinks=attn_sinks[l], conv_w=conv_w[l], conv_b=conv_b[l], norm_mem=norm_mem[l],
            w_mem_kv=w_mem_kv[l].astype(BF16), mem_q_norm=mem_q_norm[l], mem_k_norm=mem_k_norm[l],
            out_norm_attn=out_norm_attn[l], out_norm_conv=out_norm_conv[l],
            out_norm_mem=out_norm_mem[l], w_out=w_out[l].astype(BF16), norm_ffn=norm_ffn[l],
            w_gate=w_gate[l].astype(BF16), w_up=w_up[l].astype(BF16), w_down=w_down[l].astype(BF16),
            gmat=gmat)
        x = _layer(x, _mem_kv(mem, lp, plan[2]), lp, plan)
    return x
```

```python
import functools

import jax
import jax.numpy as jnp
import numpy as np
from jax import lax
from jax.experimental import pallas as pl
from jax.experimental.pallas import tpu as pltpu

D_MODEL = 1024
HEAD_DIM = 64
N_ATTN_HEADS = 8
N_KV_HEADS = 2
GQA_GROUP = N_ATTN_HEADS // N_KV_HEADS
BLOCK = 128
N_MEM_HEADS = 4
N_MEM = 256
CONV_K = 3
ATTN_WIDTH = N_ATTN_HEADS * HEAD_DIM
KV_WIDTH = N_KV_HEADS * HEAD_DIM
CONV_WIDTH = 256
MEM_WIDTH = N_MEM_HEADS * HEAD_DIM
IN_PROJ_WIDTH = ATTN_WIDTH + 2 * KV_WIDTH + 3 * CONV_WIDTH + MEM_WIDTH
EPS = 1e-6
MASKED_DIST = 2.0 ** 110

V7X_VMEM_BYTES = 64 * 1024 * 1024
V7X_SUBLANES = 8
V7X_LANES = 128
V7X_MXU_DIM = 256

F32 = jnp.float32
BF16 = jnp.bfloat16


def _plan():
    seq_tile = 4 * BLOCK
    ffn_tile = 512
    vmem_limit = V7X_VMEM_BYTES - 8 * 1024 * 1024
    return seq_tile, ffn_tile, vmem_limit


def _rms(a, gain):
    return a * lax.rsqrt(jnp.mean(a * a, axis=-1, keepdims=True) + EPS) * gain


def _head_rms_scale(t, gmat):
    sq = (t * t).astype(BF16)
    ss = jnp.dot(sq, gmat, preferred_element_type=F32)
    return lax.rsqrt(ss * (1.0 / HEAD_DIM) + EPS)


def _head_rms_rows(t, gain):
    heads = t.shape[0] // HEAD_DIM
    out = []
    for h in range(heads):
        th = t[h * HEAD_DIM:(h + 1) * HEAD_DIM]
        ss = jnp.sum(th * th, axis=0, keepdims=True)
        out.append(th * lax.rsqrt(ss * (1.0 / HEAD_DIM) + EPS) * gain[h * HEAD_DIM:(h + 1) * HEAD_DIM])
    return out


def _mem_kv_kernel(mem_ref, gain_ref, w_ref, kgain_ref, gmat_ref, kblk_ref, vblkt_ref):
    m = mem_ref[0]
    mn = _rms(m, gain_ref[...]).astype(BF16)
    kv = jnp.dot(mn, w_ref[...], preferred_element_type=F32)
    k = kv[:, :MEM_WIDTH]
    vt = kv[:, MEM_WIDTH:].T
    kn = k * _head_rms_scale(k, gmat_ref[...]) * kgain_ref[...]
    lane_head = lax.broadcasted_iota(jnp.int32, (N_MEM, MEM_WIDTH), 1) // HEAD_DIM
    row_head = lax.broadcasted_iota(jnp.int32, (MEM_WIDTH, N_MEM), 0) // HEAD_DIM
    for h in range(N_MEM_HEADS):
        kblk_ref[0, h * N_MEM:(h + 1) * N_MEM, :] = jnp.where(lane_head == h, kn, 0.0).astype(BF16)
        vblkt_ref[0, :, h * N_MEM:(h + 1) * N_MEM] = jnp.where(row_head == h, vt, 0.0).astype(BF16)


def _mixer_kernel(seq_tile,
                  x_ref, kblk_ref, vblkt_ref, win_ref, wout_ref, gmat_ref,
                  nmix_ref, qgain_ref, kgain_ref, mqgain_ref, sinks_ref, convw_ref, convb_ref,
                  ona_ref, onc_ref, onm_ref,
                  out_ref,
                  kpad_ref, vtpad_ref, upad_ref):
    j = pl.program_id(1)
    n_blocks = seq_tile // BLOCK

    @pl.when(j == 0)
    def _():
        kpad_ref[0:BLOCK, :] = jnp.zeros((BLOCK, KV_WIDTH), BF16)
        vtpad_ref[:, 0:BLOCK] = jnp.zeros((KV_WIDTH, BLOCK), BF16)
        upad_ref[0:V7X_SUBLANES, :] = jnp.zeros((V7X_SUBLANES, CONV_WIDTH), F32)

    x = x_ref[0]
    xn = _rms(x, nmix_ref[...]).astype(BF16)
    proj = jnp.dot(xn, win_ref[...], preferred_element_type=F32)

    o = 0
    q_a = proj[:, o:o + ATTN_WIDTH]; o += ATTN_WIDTH
    kv_a = proj[:, o:o + 2 * KV_WIDTH]; o += 2 * KV_WIDTH
    c_h = proj[:, o:o + CONV_WIDTH]; o += CONV_WIDTH
    c_b = proj[:, o:o + CONV_WIDTH]; o += CONV_WIDTH
    c_c = proj[:, o:o + CONV_WIDTH]; o += CONV_WIDTH
    q_m = proj[:, o:o + MEM_WIDTH]

    kv_scale = _head_rms_scale(kv_a, gmat_ref[...])
    kn = kv_a[:, :KV_WIDTH] * kv_scale[:, :KV_WIDTH] * kgain_ref[...]
    kpad_ref[BLOCK:BLOCK + seq_tile, :] = kn.astype(BF16)
    vtpad_ref[:, BLOCK:BLOCK + seq_tile] = kv_a[:, KV_WIDTH:].T.astype(BF16)

    q_t = q_a.T
    q_gain = qgain_ref[...] * HEAD_DIM ** -0.5
    key = lax.broadcasted_iota(jnp.int32, (BLOCK, GQA_GROUP * BLOCK), 0)
    qry = lax.broadcasted_iota(jnp.int32, (BLOCK, GQA_GROUP * BLOCK), 1) % BLOCK
    from_prev = key > qry
    dist = jnp.where(from_prev, qry + BLOCK - key, qry - key).astype(F32)
    dist_first = jnp.where(jnp.logical_and(from_prev, j == 0), MASKED_DIST, dist)
    zeros_q = jnp.zeros((HEAD_DIM, GQA_GROUP * BLOCK), BF16)

    attn_t_blocks = []
    for b in range(n_blocks):
        cols = slice(b * BLOCK, (b + 1) * BLOCK)
        qn = _head_rms_rows(q_t[:, cols], q_gain)
        k_cat = kpad_ref[b * BLOCK:(b + 2) * BLOCK, :]
        dist_b = dist_first if b == 0 else dist
        head_rows = []
        for g in range(N_KV_HEADS):
            heads = range(g * GQA_GROUP, (g + 1) * GQA_GROUP)
            q4 = jnp.concatenate([qn[h].astype(BF16) for h in heads], axis=1)
            w_q = jnp.concatenate([q4, zeros_q] if g == 0 else [zeros_q, q4], axis=0)
            s2 = jnp.dot(k_cat, w_q, preferred_element_type=F32)
            neg_slope = jnp.concatenate(
                [jnp.full((1, BLOCK), -(2.0 ** (-8.0 * (h + 1) / N_ATTN_HEADS)), F32) for h in heads], axis=1)
            sink = sinks_ref[:, g * GQA_GROUP * BLOCK:(g + 1) * GQA_GROUP * BLOCK]
            s = jnp.where(from_prev, s2[:BLOCK], s2[BLOCK:]) + dist_b * neg_slope
            m = jnp.maximum(jnp.max(s, axis=0, keepdims=True), sink)
            p = jnp.exp(s - m)
            l = jnp.sum(p, axis=0, keepdims=True) + jnp.exp(sink - m)
            p_t = jnp.concatenate(
                [jnp.where(from_prev, p, 0.0).astype(BF16), jnp.where(from_prev, 0.0, p).astype(BF16)],
                axis=0)
            vt_cat = vtpad_ref[g * HEAD_DIM:(g + 1) * HEAD_DIM, b * BLOCK:(b + 2) * BLOCK]
            o_t = jnp.dot(vt_cat, p_t, preferred_element_type=F32) * (1.0 / l)
            head_rows += [o_t[:, hh * BLOCK:(hh + 1) * BLOCK] for hh in range(GQA_GROUP)]
        attn_t_blocks.append(jnp.concatenate(head_rows, axis=0))
    attn_out = jnp.concatenate(attn_t_blocks, axis=1).T
    kpad_ref[0:BLOCK, :] = kpad_ref[seq_tile:seq_tile + BLOCK, :]
    vtpad_ref[:, 0:BLOCK] = vtpad_ref[:, seq_tile:seq_tile + BLOCK]

    u = c_c * c_h
    upad_ref[V7X_SUBLANES:V7X_SUBLANES + seq_tile, :] = u
    u1 = upad_ref[V7X_SUBLANES - 1:V7X_SUBLANES - 1 + seq_tile, :]
    u2 = upad_ref[V7X_SUBLANES - 2:V7X_SUBLANES - 2 + seq_tile, :]
    cw = convw_ref[...]
    conv = cw[0:1] * u2 + cw[1:2] * u1 + cw[2:3] * u + convb_ref[...]
    conv_out = c_b * conv
    upad_ref[0:V7X_SUBLANES, :] = upad_ref[seq_tile:seq_tile + V7X_SUBLANES, :]

    mq_gain = mqgain_ref[...] * HEAD_DIM ** -0.5
    mq_gain = jnp.concatenate([mq_gain] * (V7X_MXU_DIM // BLOCK), axis=1)
    kblk = kblk_ref[0]
    vblkt = vblkt_ref[0]
    q_mt = q_m.T
    mem_t_chunks = []
    for c in range(seq_tile // V7X_MXU_DIM):
        cols = slice(c * V7X_MXU_DIM, (c + 1) * V7X_MXU_DIM)
        qmn = jnp.concatenate(_head_rms_rows(q_mt[:, cols], mq_gain), axis=0).astype(BF16)
        s_all = jnp.dot(kblk, qmn, preferred_element_type=F32)
        ps, inv_l = [], []
        for h in range(N_MEM_HEADS):
            s = s_all[h * N_MEM:(h + 1) * N_MEM]
            m = jnp.max(s, axis=0, keepdims=True)
            p = jnp.exp(s - m)
            inv_l.append(1.0 / jnp.sum(p, axis=0, keepdims=True))
            ps.append(p.astype(BF16))
        o_t = jnp.dot(vblkt, jnp.concatenate(ps, axis=0), preferred_element_type=F32)
        mem_t_chunks.append(jnp.concatenate(
            [o_t[h * HEAD_DIM:(h + 1) * HEAD_DIM] * inv_l[h] for h in range(N_MEM_HEADS)], axis=0))
    mem_out = jnp.concatenate(mem_t_chunks, axis=1).T

    merged = jnp.concatenate(
        [_rms(attn_out, ona_ref[...]), _rms(conv_out, onc_ref[...]), _rms(mem_out, onm_ref[...])],
        axis=1).astype(BF16)
    out_ref[0] = x + jnp.dot(merged, wout_ref[...], preferred_element_type=F32)


def _ffn_kernel(x_ref, gain_ref, wg_ref, wu_ref, wd_ref, out_ref):
    x = x_ref[...]
    h = _rms(x, gain_ref[...]).astype(BF16)
    gate = jnp.dot(h, wg_ref[...], preferred_element_type=F32)
    up = jnp.dot(h, wu_ref[...], preferred_element_type=F32)
    act = (gate * jax.nn.sigmoid(gate) * up).astype(BF16)
    out_ref[...] = x + jnp.dot(act, wd_ref[...], preferred_element_type=F32)


def _const_spec(shape):
    return pl.BlockSpec(shape, lambda *_: (0,) * len(shape), pipeline_mode=pl.Buffered(1))


def _row(a):
    return a.reshape(1, -1)


def _head_column(gain, heads):
    return jnp.broadcast_to(jnp.tile(gain, heads)[:, None], (heads * HEAD_DIM, BLOCK))


def _layer(x, mem_blocks, lp, plan):
    seq_tile, ffn_tile, vmem_limit = plan
    batch, seq, _ = x.shape
    kblk, vblkt = mem_blocks

    mixer = pl.pallas_call(
        functools.partial(_mixer_kernel, seq_tile),
        out_shape=jax.ShapeDtypeStruct(x.shape, F32),
        grid=(batch, seq // seq_tile),
        in_specs=[
            pl.BlockSpec((1, seq_tile, D_MODEL), lambda b, j: (b, j, 0)),
            pl.BlockSpec((1, N_MEM_HEADS * N_MEM, MEM_WIDTH), lambda b, j: (b, 0, 0)),
            pl.BlockSpec((1, MEM_WIDTH, N_MEM_HEADS * N_MEM), lambda b, j: (b, 0, 0)),
            _const_spec((D_MODEL, IN_PROJ_WIDTH)),
            _const_spec((D_MODEL, D_MODEL)),
            _const_spec((V7X_MXU_DIM, V7X_MXU_DIM)),
            _const_spec((1, D_MODEL)),
            _const_spec((ATTN_WIDTH, BLOCK)),
            _const_spec((1, KV_WIDTH)),
            _const_spec((MEM_WIDTH, BLOCK)),
            _const_spec((1, N_ATTN_HEADS * BLOCK)),
            _const_spec((CONV_K, CONV_WIDTH)),
            _const_spec((1, CONV_WIDTH)),
            _const_spec((1, ATTN_WIDTH)),
            _const_spec((1, CONV_WIDTH)),
            _const_spec((1, MEM_WIDTH)),
        ],
        out_specs=pl.BlockSpec((1, seq_tile, D_MODEL), lambda b, j: (b, j, 0)),
        scratch_shapes=[
            pltpu.VMEM((seq_tile + BLOCK, KV_WIDTH), BF16),
            pltpu.VMEM((KV_WIDTH, seq_tile + BLOCK), BF16),
            pltpu.VMEM((seq_tile + V7X_SUBLANES, CONV_WIDTH), F32),
        ],
        compiler_params=pltpu.CompilerParams(
            dimension_semantics=("arbitrary", "arbitrary"), vmem_limit_bytes=vmem_limit),
        name="mixer",
    )
    x1 = mixer(
        x, kblk, vblkt, lp["w_in"], lp["w_out"], lp["gmat"],
        _row(lp["norm_mix"]), _head_column(lp["q_norm"], N_ATTN_HEADS),
        jnp.tile(_row(lp["k_norm"]), (1, N_KV_HEADS)), _head_column(lp["mem_q_norm"], N_MEM_HEADS),
        _row(jnp.repeat(lp["attn_sinks"], BLOCK)),
        lp["conv_w"], _row(lp["conv_b"]),
        _row(lp["out_norm_attn"]), _row(lp["out_norm_conv"]), _row(lp["out_norm_mem"]))

    tokens = batch * seq
    d_ff = lp["w_gate"].shape[1]
    ffn = pl.pallas_call(
        _ffn_kernel,
        out_shape=jax.ShapeDtypeStruct((tokens, D_MODEL), F32),
        grid=(tokens // ffn_tile,),
        in_specs=[
            pl.BlockSpec((ffn_tile, D_MODEL), lambda i: (i, 0)),
            _const_spec((1, D_MODEL)),
            _const_spec((D_MODEL, d_ff)),
            _const_spec((D_MODEL, d_ff)),
            _const_spec((d_ff, D_MODEL)),
        ],
        out_specs=pl.BlockSpec((ffn_tile, D_MODEL), lambda i: (i, 0)),
        compiler_params=pltpu.CompilerParams(
            dimension_semantics=("arbitrary",), vmem_limit_bytes=vmem_limit),
        name="ffn",
    )
    y = ffn(x1.reshape(tokens, D_MODEL), _row(lp["norm_ffn"]), lp["w_gate"], lp["w_up"], lp["w_down"])
    return y.reshape(x.shape)


def _mem_kv(mem, lp, vmem_limit):
    batch = mem.shape[0]
    kblk_shape = (batch, N_MEM_HEADS * N_MEM, MEM_WIDTH)
    vblkt_shape = (batch, MEM_WIDTH, N_MEM_HEADS * N_MEM)
    call = pl.pallas_call(
        _mem_kv_kernel,
        out_shape=(jax.ShapeDtypeStruct(kblk_shape, BF16), jax.ShapeDtypeStruct(vblkt_shape, BF16)),
        grid=(batch,),
        in_specs=[
            pl.BlockSpec((1, N_MEM, D_MODEL), lambda b: (b, 0, 0)),
            _const_spec((1, D_MODEL)),
            _const_spec((D_MODEL, 2 * MEM_WIDTH)),
            _const_spec((1, MEM_WIDTH)),
            _const_spec((V7X_MXU_DIM, V7X_MXU_DIM)),
        ],
        out_specs=(pl.BlockSpec((1,) + kblk_shape[1:], lambda b: (b, 0, 0)),
                   pl.BlockSpec((1,) + vblkt_shape[1:], lambda b: (b, 0, 0))),
        compiler_params=pltpu.CompilerParams(
            dimension_semantics=("arbitrary",), vmem_limit_bytes=vmem_limit),
        name="mem_kv",
    )
    return call(mem, _row(lp["norm_mem"]), lp["w_mem_kv"],
                jnp.tile(_row(lp["mem_k_norm"]), (1, N_MEM_HEADS)), lp["gmat"])


def kernel(x, mem, norm_mix, w_in, q_norm, k_norm, attn_sinks, conv_w, conv_b, norm_mem, w_mem_kv,
           mem_q_norm, mem_k_norm, out_norm_attn, out_norm_conv, out_norm_mem, w_out, norm_ffn,
           w_gate, w_up, w_down):
    plan = _plan()
    head_of_lane = np.arange(V7X_MXU_DIM) // HEAD_DIM
    gmat = jnp.asarray(head_of_lane[:, None] == head_of_lane[None, :], dtype=BF16)
    depth = w_in.shape[0]
    for l in range(depth):
        lp = dict(
            norm_mix=norm_mix[l], w_in=w_in[l].astype(BF16), q_norm=q_norm[l], k_norm=k_norm[l],
            attn_sinks=attn_sinks[l], conv_w=conv_w[l], conv_b=conv_b[l], norm_mem=norm_mem[l],
            w_mem_kv=w_mem_kv[l].astype(BF16), mem_q_norm=mem_q_norm[l], mem_k_norm=mem_k_norm[l],
            out_norm_attn=out_norm_attn[l], out_norm_conv=out_norm_conv[l],
            out_norm_mem=out_norm_mem[l], w_out=w_out[l].astype(BF16), norm_ffn=norm_ffn[l],
            w_gate=w_gate[l].astype(BF16), w_up=w_up[l].astype(BF16), w_down=w_down[l].astype(BF16),
            gmat=gmat)
        x = _layer(x, _mem_kv(mem, lp, plan[2]), lp, plan)
    return x
```

```python
import functools

import jax
import jax.numpy as jnp
import numpy as np
from jax import lax
from jax.experimental import pallas as pl
from jax.experimental.pallas import tpu as pltpu

D_MODEL = 1024
HEAD_DIM = 64
N_ATTN_HEADS = 8
N_KV_HEADS = 2
GQA_GROUP = N_ATTN_HEADS // N_KV_HEADS
BLOCK = 128
N_MEM_HEADS = 4
N_MEM = 256
CONV_K = 3
ATTN_WIDTH = N_ATTN_HEADS * HEAD_DIM
KV_WIDTH = N_KV_HEADS * HEAD_DIM
CONV_WIDTH = 256
MEM_WIDTH = N_MEM_HEADS * HEAD_DIM
IN_PROJ_WIDTH = ATTN_WIDTH + 2 * KV_WIDTH + 3 * CONV_WIDTH + MEM_WIDTH
EPS = 1e-6
MASKED_DIST = 2.0 ** 110

V7X_VMEM_BYTES = 64 * 1024 * 1024
V7X_SUBLANES = 8
V7X_LANES = 128
V7X_MXU_DIM = 256

F32 = jnp.float32
BF16 = jnp.bfloat16


def _plan():
    seq_tile = 4 * BLOCK
    ffn_tile = 512
    vmem_limit = V7X_VMEM_BYTES - 8 * 1024 * 1024
    return seq_tile, ffn_tile, vmem_limit


def _rms(a, gain):
    return a * lax.rsqrt(jnp.mean(a * a, axis=-1, keepdims=True) + EPS) * gain


def _head_rms_scale(t, gmat):
    sq = (t * t).astype(BF16)
    ss = jnp.dot(sq, gmat, preferred_element_type=F32)
    return lax.rsqrt(ss * (1.0 / HEAD_DIM) + EPS)


def _head_rms_rows(t, gain):
    heads = t.shape[0] // HEAD_DIM
    out = []
    for h in range(heads):
        th = t[h * HEAD_DIM:(h + 1) * HEAD_DIM]
        ss = jnp.sum(th * th, axis=0, keepdims=True)
        out.append(th * lax.rsqrt(ss * (1.0 / HEAD_DIM) + EPS) * gain[h * HEAD_DIM:(h + 1) * HEAD_DIM])
    return out


def _mem_kv_kernel(mem_ref, gain_ref, w_ref, kgain_ref, gmat_ref, kblk_ref, vblkt_ref):
    m = mem_ref[0]
    mn = _rms(m, gain_ref[...]).astype(BF16)
    kv = jnp.dot(mn, w_ref[...], preferred_element_type=F32)
    k = kv[:, :MEM_WIDTH]
    vt = kv[:, MEM_WIDTH:].T
    kn = k * _head_rms_scale(k, gmat_ref[...]) * kgain_ref[...]
    lane_head = lax.broadcasted_iota(jnp.int32, (N_MEM, MEM_WIDTH), 1) // HEAD_DIM
    row_head = lax.broadcasted_iota(jnp.int32, (MEM_WIDTH, N_MEM), 0) // HEAD_DIM
    for h in range(N_MEM_HEADS):
        kblk_ref[0, h * N_MEM:(h + 1) * N_MEM, :] = jnp.where(lane_head == h, kn, 0.0).astype(BF16)
        vblkt_ref[0, :, h * N_MEM:(h + 1) * N_MEM] = jnp.where(row_head == h, vt, 0.0).astype(BF16)


ROW_CHUNK = V7X_MXU_DIM
COL_GROUP = 2 * V7X_MXU_DIM


def _stage1_pieces(x_ref, wslot, carry, refs):
    (win_ref, gmat_ref, nmix_ref, qgain_ref, kgain_ref, mqgain_ref,
     qnt_ref, qmnt_ref, kpad_ref, vtpad_ref, upad_ref, cb_ref) = refs
    seq_tile = cb_ref.shape[1]
    xn, val = {}, {}
    conv_base = ATTN_WIDTH + 2 * KV_WIDTH

    def carried(ref_slice_fn, shape, dtype):
        if carry is None:
            return jnp.zeros(shape, dtype)
        prev_slot, first_of_seq = carry
        return jnp.where(first_of_seq, jnp.zeros(shape, dtype), ref_slice_fn(prev_slot))

    def heads_t(t, gain_ref, out_ref, r):
        t = t.T
        gain = gain_ref[...] * HEAD_DIM ** -0.5
        gain = jnp.concatenate([gain] * (ROW_CHUNK // BLOCK), axis=1)
        out_ref[wslot, :, r * ROW_CHUNK:(r + 1) * ROW_CHUNK] = jnp.concatenate(
            _head_rms_rows(t, gain), axis=0).astype(BF16)

    norms, dots, posts, narrow_dots, narrow_posts = [], [], [], [], []
    for r in range(seq_tile // ROW_CHUNK):
        rows = slice(r * ROW_CHUNK, (r + 1) * ROW_CHUNK)

        def norm(r=r, rows=rows):
            xn[r] = _rms(x_ref[0, rows, :], nmix_ref[...]).astype(BF16)

        def dot_piece(name, start, width, r=r):
            def run():
                val[name, r] = jnp.dot(xn[r], win_ref[:, start:start + width], preferred_element_type=F32)
            return run

        def post_q(r=r):
            heads_t(val.pop(("q", r)), qgain_ref, qnt_ref, r)

        def post_kv(r=r):
            t = val.pop(("kvch", r))
            kv_a = t[:, :2 * KV_WIDTH]
            val["ch", r] = t[:, 2 * KV_WIDTH:]
            scale = _head_rms_scale(kv_a, gmat_ref[...])
            kn = kv_a[:, :KV_WIDTH] * scale[:, :KV_WIDTH] * kgain_ref[...]
            kpad_ref[wslot, BLOCK + r * ROW_CHUNK:BLOCK + (r + 1) * ROW_CHUNK, :] = kn.astype(BF16)
            vtpad_ref[wslot, :, BLOCK + r * ROW_CHUNK:BLOCK + (r + 1) * ROW_CHUNK] = (
                kv_a[:, KV_WIDTH:].T.astype(BF16))
            if r == 0:
                kpad_ref[wslot, 0:BLOCK, :] = carried(
                    lambda s: kpad_ref[s, seq_tile:seq_tile + BLOCK, :], (BLOCK, KV_WIDTH), BF16)
                vtpad_ref[wslot, :, 0:BLOCK] = carried(
                    lambda s: vtpad_ref[s, :, seq_tile:seq_tile + BLOCK], (KV_WIDTH, BLOCK), BF16)

        def post_conv(r=r):
            t = val.pop(("cbcc", r))
            cb_ref[wslot, r * ROW_CHUNK:(r + 1) * ROW_CHUNK, :] = t[:, :CONV_WIDTH]
            u = t[:, CONV_WIDTH:] * val.pop(("ch", r))
            upad_ref[wslot, V7X_SUBLANES + r * ROW_CHUNK:V7X_SUBLANES + (r + 1) * ROW_CHUNK, :] = u
            if r == 0:
                upad_ref[wslot, 0:V7X_SUBLANES, :] = carried(
                    lambda s: upad_ref[s, seq_tile:seq_tile + V7X_SUBLANES, :],
                    (V7X_SUBLANES, CONV_WIDTH), F32)

        def post_qm(r=r):
            heads_t(val.pop(("qm", r)), mqgain_ref, qmnt_ref, r)

        norms.append(norm)
        dots += [dot_piece("q", 0, COL_GROUP), dot_piece("kvch", ATTN_WIDTH, COL_GROUP),
                 dot_piece("cbcc", conv_base + CONV_WIDTH, COL_GROUP)]
        posts += [post_q, post_kv, post_conv]
        narrow_dots.append(dot_piece("qm", IN_PROJ_WIDTH - MEM_WIDTH, MEM_WIDTH))
        narrow_posts.append(post_qm)
    return norms, dots + narrow_dots, posts + narrow_posts


def _mixer_kernel(seq_tile, tiles_per_seq,
                  xcur_ref, xnext_ref, kblk_ref, vblkt_ref, win_ref, wout_ref, gmat_ref,
                  nmix_ref, qgain_ref, kgain_ref, mqgain_ref, sinks_ref, convw_ref, convb_ref,
                  ona_ref, onc_ref, onm_ref,
                  out_ref,
                  qnt_ref, qmnt_ref, kpad_ref, vtpad_ref, upad_ref, cb_ref):
    t = pl.program_id(0)
    slot = t % 2
    n_blocks = seq_tile // BLOCK
    first_of_seq = (t % tiles_per_seq) == 0
    stage1_refs = (win_ref, gmat_ref, nmix_ref, qgain_ref, kgain_ref, mqgain_ref,
                   qnt_ref, qmnt_ref, kpad_ref, vtpad_ref, upad_ref, cb_ref)

    @pl.when(t == 0)
    def _():
        norms, dots, posts = _stage1_pieces(xcur_ref, 0, None, stage1_refs)
        for piece in norms + [p for pair in zip(dots, posts) for p in pair]:
            piece()

    norms, dots, posts = _stage1_pieces(
        xnext_ref, 1 - slot, (slot, ((t + 1) % tiles_per_seq) == 0), stage1_refs)
    n_chains = n_blocks * N_KV_HEADS
    held = 2
    chain_fill = [[] for _ in range(n_chains)]
    chain_fill[0] = [dots[0], dots[1]]
    n_early = len(dots) - held
    for i in range(1, n_chains):
        chain_fill[i] = ([posts[i - 1]] if i - 1 < n_early else []) + ([dots[i + 1]] if i + 1 < n_early else [])
    tail_dots, tail_posts = dots[n_early:], posts[n_early:]

    key = lax.broadcasted_iota(jnp.int32, (BLOCK, GQA_GROUP * BLOCK), 0)
    qry = lax.broadcasted_iota(jnp.int32, (BLOCK, GQA_GROUP * BLOCK), 1) % BLOCK
    from_prev = key > qry
    dist = jnp.where(from_prev, qry + BLOCK - key, qry - key).astype(F32)
    dist_first = jnp.where(jnp.logical_and(from_prev, first_of_seq), MASKED_DIST, dist)
    zeros_q = jnp.zeros((HEAD_DIM, GQA_GROUP * BLOCK), BF16)

    chains = [(b, g) for b in range(n_blocks) for g in range(N_KV_HEADS)]
    vt_cat = {(b, g): vtpad_ref[slot, g * HEAD_DIM:(g + 1) * HEAD_DIM, b * BLOCK:(b + 2) * BLOCK]
              for b, g in chains}
    qmn = qmnt_ref[slot]
    n_halves = seq_tile // V7X_MXU_DIM
    u = upad_ref[slot, V7X_SUBLANES:V7X_SUBLANES + seq_tile, :]
    u1 = upad_ref[slot, V7X_SUBLANES - 1:V7X_SUBLANES - 1 + seq_tile, :]
    u2 = upad_ref[slot, V7X_SUBLANES - 2:V7X_SUBLANES - 2 + seq_tile, :]
    cw = convw_ref[...]
    conv = cw[0:1] * u2 + cw[1:2] * u1 + cw[2:3] * u + convb_ref[...]
    conv_n = _rms(cb_ref[slot] * conv, onc_ref[...])

    scores = {}
    for b, g in chains:
        cols = slice(b * BLOCK, (b + 1) * BLOCK)
        k_cat = kpad_ref[slot, b * BLOCK:(b + 2) * BLOCK, :]
        heads = range(g * GQA_GROUP, (g + 1) * GQA_GROUP)
        q4 = jnp.concatenate(
            [qnt_ref[slot, h * HEAD_DIM:(h + 1) * HEAD_DIM, cols] for h in heads], axis=1)
        w_q = jnp.concatenate([q4, zeros_q] if g == 0 else [zeros_q, q4], axis=0)
        scores[b, g] = jnp.dot(k_cat, w_q, preferred_element_type=F32)
    kblk = kblk_ref[0]
    vblkt = vblkt_ref[0]
    mem_scores = jnp.dot(kblk, qmn, preferred_element_type=F32)
    y_conv = jnp.dot(conv_n.astype(BF16), wout_ref[ATTN_WIDTH:ATTN_WIDTH + CONV_WIDTH, :],
                     preferred_element_type=F32)
    mem_units = [(c, h) for c in range(n_halves) for h in range(N_MEM_HEADS)]
    mem_p, mem_inv_l = {}, {}

    def mem_softmax(c, h):
        s = mem_scores[h * N_MEM:(h + 1) * N_MEM, c * V7X_MXU_DIM:(c + 1) * V7X_MXU_DIM]
        m = jnp.max(s, axis=0, keepdims=True)
        p = jnp.exp(s - m)
        mem_inv_l[c, h] = 1.0 / jnp.sum(p, axis=0, keepdims=True)
        mem_p[c, h] = p.astype(BF16)

    def finish_rows(c):
        rows = slice(c * V7X_MXU_DIM, (c + 1) * V7X_MXU_DIM)
        blocks = range(c * V7X_MXU_DIM // BLOCK, (c + 1) * V7X_MXU_DIM // BLOCK)
        attn_rows = jnp.concatenate(
            [jnp.concatenate([row for g in range(N_KV_HEADS) for row in head_rows[b, g]], axis=0)
             for b in blocks], axis=1).T
        p_all = jnp.concatenate([mem_p[c, h] for h in range(N_MEM_HEADS)], axis=0)
        o_t = jnp.dot(vblkt, p_all, preferred_element_type=F32)
        mem_rows = jnp.concatenate(
            [o_t[h * HEAD_DIM:(h + 1) * HEAD_DIM] * mem_inv_l[c, h] for h in range(N_MEM_HEADS)],
            axis=0).T
        y = y_conv[rows] + jnp.dot(_rms(attn_rows, ona_ref[...]).astype(BF16), wout_ref[0:ATTN_WIDTH, :],
                                   preferred_element_type=F32)
        y = y + jnp.dot(_rms(mem_rows, onm_ref[...]).astype(BF16), wout_ref[ATTN_WIDTH + CONV_WIDTH:, :],
                        preferred_element_type=F32)
        out_ref[0, rows, :] = xcur_ref[0, rows, :] + y

    for piece in norms:
        piece()
    head_rows = {}
    chains_per_half = len(chains) // n_halves
    assert len(mem_units) == len(chains)
    for i, (b, g) in enumerate(chains):
        heads = range(g * GQA_GROUP, (g + 1) * GQA_GROUP)
        dist_b = dist_first if b == 0 else dist
        neg_slope = jnp.concatenate(
            [jnp.full((1, BLOCK), -(2.0 ** (-8.0 * (h + 1) / N_ATTN_HEADS)), F32) for h in heads], axis=1)
        sink = sinks_ref[:, g * GQA_GROUP * BLOCK:(g + 1) * GQA_GROUP * BLOCK]
        s2 = scores.pop((b, g))
        s = jnp.where(from_prev, s2[:BLOCK], s2[BLOCK:]) + dist_b * neg_slope
        m = jnp.maximum(jnp.max(s, axis=0, keepdims=True), sink)
        p = jnp.exp(s - m)
        l = jnp.sum(p, axis=0, keepdims=True) + jnp.exp(sink - m)
        p_t = jnp.concatenate(
            [jnp.where(from_prev, p, 0.0).astype(BF16), jnp.where(from_prev, 0.0, p).astype(BF16)],
            axis=0)
        for piece in chain_fill[i]:
            piece()
        o_t = jnp.dot(vt_cat[b, g], p_t, preferred_element_type=F32) * (1.0 / l)
        head_rows[b, g] = [o_t[:, hh * BLOCK:(hh + 1) * BLOCK] for hh in range(GQA_GROUP)]
        mem_softmax(*mem_units[i])
        if i + 1 == chains_per_half:
            finish_rows(0)
    for piece in tail_dots:
        piece()
    finish_rows(1)
    for piece in tail_posts:
        piece()


def _ffn_kernel(x_ref, gain_ref, wg_ref, wu_ref, wd_ref, out_ref):
    x = x_ref[...]
    h = _rms(x, gain_ref[...]).astype(BF16)
    gate = jnp.dot(h, wg_ref[...], preferred_element_type=F32)
    up = jnp.dot(h, wu_ref[...], preferred_element_type=F32)
    act = (gate * jax.nn.sigmoid(gate) * up).astype(BF16)
    out_ref[...] = x + jnp.dot(act, wd_ref[...], preferred_element_type=F32)


def _const_spec(shape):
    return pl.BlockSpec(shape, lambda *_: (0,) * len(shape), pipeline_mode=pl.Buffered(1))


def _row(a):
    return a.reshape(1, -1)


def _head_column(gain, heads):
    return jnp.broadcast_to(jnp.tile(gain, heads)[:, None], (heads * HEAD_DIM, BLOCK))


def _layer(x, mem_blocks, lp, plan):
    seq_tile, ffn_tile, vmem_limit = plan
    batch, seq, _ = x.shape
    kblk, vblkt = mem_blocks

    tiles_per_seq = seq // seq_tile
    n_tiles = batch * tiles_per_seq

    def tile_index(t):
        return (t // tiles_per_seq, t % tiles_per_seq, 0)

    mixer = pl.pallas_call(
        functools.partial(_mixer_kernel, seq_tile, tiles_per_seq),
        out_shape=jax.ShapeDtypeStruct(x.shape, F32),
        grid=(n_tiles,),
        in_specs=[
            pl.BlockSpec((1, seq_tile, D_MODEL), tile_index),
            pl.BlockSpec((1, seq_tile, D_MODEL), lambda t: tile_index(jnp.minimum(t + 1, n_tiles - 1))),
            pl.BlockSpec((1, N_MEM_HEADS * N_MEM, MEM_WIDTH), lambda t: (t // tiles_per_seq, 0, 0)),
            pl.BlockSpec((1, MEM_WIDTH, N_MEM_HEADS * N_MEM), lambda t: (t // tiles_per_seq, 0, 0)),
            _const_spec((D_MODEL, IN_PROJ_WIDTH)),
            _const_spec((D_MODEL, D_MODEL)),
            _const_spec((V7X_MXU_DIM, V7X_MXU_DIM)),
            _const_spec((1, D_MODEL)),
            _const_spec((ATTN_WIDTH, BLOCK)),
            _const_spec((1, KV_WIDTH)),
            _const_spec((MEM_WIDTH, BLOCK)),
            _const_spec((1, N_ATTN_HEADS * BLOCK)),
            _const_spec((CONV_K, CONV_WIDTH)),
            _const_spec((1, CONV_WIDTH)),
            _const_spec((1, ATTN_WIDTH)),
            _const_spec((1, CONV_WIDTH)),
            _const_spec((1, MEM_WIDTH)),
        ],
        out_specs=pl.BlockSpec((1, seq_tile, D_MODEL), tile_index),
        scratch_shapes=[
            pltpu.VMEM((2, ATTN_WIDTH, seq_tile), BF16),
            pltpu.VMEM((2, MEM_WIDTH, seq_tile), BF16),
            pltpu.VMEM((2, seq_tile + BLOCK, KV_WIDTH), BF16),
            pltpu.VMEM((2, KV_WIDTH, seq_tile + BLOCK), BF16),
            pltpu.VMEM((2, seq_tile + V7X_SUBLANES, CONV_WIDTH), F32),
            pltpu.VMEM((2, seq_tile, CONV_WIDTH), F32),
        ],
        compiler_params=pltpu.CompilerParams(
            dimension_semantics=("arbitrary",), vmem_limit_bytes=vmem_limit),
        name="mixer",
    )
    x1 = mixer(
        x, x, kblk, vblkt, lp["w_in"], lp["w_out"], lp["gmat"],
        _row(lp["norm_mix"]), _head_column(lp["q_norm"], N_ATTN_HEADS),
        jnp.tile(_row(lp["k_norm"]), (1, N_KV_HEADS)), _head_column(lp["mem_q_norm"], N_MEM_HEADS),
        _row(jnp.repeat(lp["attn_sinks"], BLOCK)),
        lp["conv_w"], _row(lp["conv_b"]),
        _row(lp["out_norm_attn"]), _row(lp["out_norm_conv"]), _row(lp["out_norm_mem"]))

    tokens = batch * seq
    d_ff = lp["w_gate"].shape[1]
    ffn = pl.pallas_call(
        _ffn_kernel,
        out_shape=jax.ShapeDtypeStruct((tokens, D_MODEL), F32),
        grid=(tokens // ffn_tile,),
        in_specs=[
            pl.BlockSpec((ffn_tile, D_MODEL), lambda i: (i, 0)),
            _const_spec((1, D_MODEL)),
            _const_spec((D_MODEL, d_ff)),
            _const_spec((D_MODEL, d_ff)),
            _const_spec((d_ff, D_MODEL)),
        ],
        out_specs=pl.BlockSpec((ffn_tile, D_MODEL), lambda i: (i, 0)),
        compiler_params=pltpu.CompilerParams(
            dimension_semantics=("arbitrary",), vmem_limit_bytes=vmem_limit),
        name="ffn",
    )
    y = ffn(x1.reshape(tokens, D_MODEL), _row(lp["norm_ffn"]), lp["w_gate"], lp["w_up"], lp["w_down"])
    return y.reshape(x.shape)


def _mem_kv(mem, lp, vmem_limit):
    batch = mem.shape[0]
    kblk_shape = (batch, N_MEM_HEADS * N_MEM, MEM_WIDTH)
    vblkt_shape = (batch, MEM_WIDTH, N_MEM_HEADS * N_MEM)
    call = pl.pallas_call(
        _mem_kv_kernel,
        out_shape=(jax.ShapeDtypeStruct(kblk_shape, BF16), jax.ShapeDtypeStruct(vblkt_shape, BF16)),
        grid=(batch,),
        in_specs=[
            pl.BlockSpec((1, N_MEM, D_MODEL), lambda b: (b, 0, 0)),
            _const_spec((1, D_MODEL)),
            _const_spec((D_MODEL, 2 * MEM_WIDTH)),
            _const_spec((1, MEM_WIDTH)),
            _const_spec((V7X_MXU_DIM, V7X_MXU_DIM)),
        ],
        out_specs=(pl.BlockSpec((1,) + kblk_shape[1:], lambda b: (b, 0, 0)),
                   pl.BlockSpec((1,) + vblkt_shape[1:], lambda b: (b, 0, 0))),
        compiler_params=pltpu.CompilerParams(
            dimension_semantics=("arbitrary",), vmem_limit_bytes=vmem_limit),
        name="mem_kv",
    )
    return call(mem, _row(lp["norm_mem"]), lp["w_mem_kv"],
                jnp.tile(_row(lp["mem_k_norm"]), (1, N_MEM_HEADS)), lp["gmat"])


def kernel(x, mem, norm_mix, w_in, q_norm, k_norm, attn_sinks, conv_w, conv_b, norm_mem, w_mem_kv,
           mem_q_norm, mem_k_norm, out_norm_attn, out_norm_conv, out_norm_mem, w_out, norm_ffn,
           w_gate, w_up, w_down):
    plan = _plan()
    head_of_lane = np.arange(V7X_MXU_DIM) // HEAD_DIM
    gmat = jnp.asarray(head_of_lane[:, None] == head_of_lane[None, :], dtype=BF16)
    depth = w_in.shape[0]
    for l in range(depth):
        lp = dict(
            norm_mix=norm_mix[l], w_in=w_in[l].astype(BF16), q_norm=q_norm[l], k_norm=k_norm[l],
            attn_sinks=attn_sinks[l], conv_w=conv_w[l], conv_b=conv_b[l], norm_mem=norm_mem[l],
            w_mem_kv=w_mem_kv[l].astype(BF16), mem_q_norm=mem_q_norm[l], mem_k_norm=mem_k_norm[l],
            out_norm_attn=out_norm_attn[l], out_norm_conv=out_norm_conv[l],
            out_norm_mem=out_norm_mem[l], w_out=w_out[l].astype(BF16), norm_ffn=norm_ffn[l],
            w_gate=w_gate[l].astype(BF16), w_up=w_up[l].astype(BF16), w_down=w_down[l].astype(BF16),
            gmat=gmat)
        x = _layer(x, _mem_kv(mem, lp, plan[2]), lp, plan)
    return x
```

```python
import functools

import jax
import jax.numpy as jnp
import numpy as np
from jax import lax
from jax.experimental import pallas as pl
from jax.experimental.pallas import tpu as pltpu

D_MODEL = 1024
HEAD_DIM = 64
N_ATTN_HEADS = 8
N_KV_HEADS = 2
GQA_GROUP = N_ATTN_HEADS // N_KV_HEADS
BLOCK = 128
N_MEM_HEADS = 4
N_MEM = 256
CONV_K = 3
ATTN_WIDTH = N_ATTN_HEADS * HEAD_DIM
KV_WIDTH = N_KV_HEADS * HEAD_DIM
CONV_WIDTH = 256
MEM_WIDTH = N_MEM_HEADS * HEAD_DIM
IN_PROJ_WIDTH = ATTN_WIDTH + 2 * KV_WIDTH + 3 * CONV_WIDTH + MEM_WIDTH
EPS = 1e-6
MASKED_DIST = 2.0 ** 110

V7X_VMEM_BYTES = 64 * 1024 * 1024
V7X_SUBLANES = 8
V7X_LANES = 128
V7X_MXU_DIM = 256

F32 = jnp.float32
BF16 = jnp.bfloat16


def _plan():
    seq_tile = 4 * BLOCK
    vmem_limit = V7X_VMEM_BYTES - 3 * 1024 * 1024
    return seq_tile, vmem_limit


def _rms(a, gain):
    return a * lax.rsqrt(jnp.mean(a * a, axis=-1, keepdims=True) + EPS) * gain


def _head_rms_scale(t, gmat):
    sq = (t * t).astype(BF16)
    ss = jnp.dot(sq, gmat, preferred_element_type=F32)
    return lax.rsqrt(ss * (1.0 / HEAD_DIM) + EPS)


def _head_rms_rows(t, gain):
    heads = t.shape[0] // HEAD_DIM
    out = []
    for h in range(heads):
        th = t[h * HEAD_DIM:(h + 1) * HEAD_DIM]
        ss = jnp.sum(th * th, axis=0, keepdims=True)
        out.append(th * lax.rsqrt(ss * (1.0 / HEAD_DIM) + EPS) * gain[h * HEAD_DIM:(h + 1) * HEAD_DIM])
    return out


def _mem_kv_kernel(mem_ref, gain_ref, w_ref, kgain_ref, gmat_ref, kblk_ref, vblkt_ref):
    m = mem_ref[0]
    mn = _rms(m, gain_ref[...]).astype(BF16)
    kv = jnp.dot(mn, w_ref[...], preferred_element_type=F32)
    k = kv[:, :MEM_WIDTH]
    vt = kv[:, MEM_WIDTH:].T
    kn = k * _head_rms_scale(k, gmat_ref[...]) * kgain_ref[...]
    lane_head = lax.broadcasted_iota(jnp.int32, (N_MEM, MEM_WIDTH), 1) // HEAD_DIM
    row_head = lax.broadcasted_iota(jnp.int32, (MEM_WIDTH, N_MEM), 0) // HEAD_DIM
    for h in range(N_MEM_HEADS):
        kblk_ref[0, h * N_MEM:(h + 1) * N_MEM, :] = jnp.where(lane_head == h, kn, 0.0).astype(BF16)
        vblkt_ref[0, :, h * N_MEM:(h + 1) * N_MEM] = jnp.where(row_head == h, vt, 0.0).astype(BF16)


ROW_CHUNK = V7X_MXU_DIM
COL_GROUP = 2 * V7X_MXU_DIM
FF_CHUNK = V7X_MXU_DIM


def _stage1_pieces(x_ref, wslot, carry, refs):
    (win_ref, gmat_ref, nmix_ref, qgain_ref, kgain_ref, mqgain_ref,
     qnt_ref, qmnt_ref, kpad_ref, vtpad_ref, upad_ref, cb_ref) = refs
    seq_tile = cb_ref.shape[1]
    xn, val = {}, {}
    conv_base = ATTN_WIDTH + 2 * KV_WIDTH

    def carried(ref_slice_fn, shape, dtype):
        if carry is None:
            return jnp.zeros(shape, dtype)
        prev_slot, first_of_seq = carry
        return jnp.where(first_of_seq, jnp.zeros(shape, dtype), ref_slice_fn(prev_slot))

    def heads_t(t, gain_ref, out_ref, r):
        t = t.T
        gain = gain_ref[...] * HEAD_DIM ** -0.5
        gain = jnp.concatenate([gain] * (ROW_CHUNK // BLOCK), axis=1)
        out_ref[wslot, :, r * ROW_CHUNK:(r + 1) * ROW_CHUNK] = jnp.concatenate(
            _head_rms_rows(t, gain), axis=0).astype(BF16)

    norms, dots, posts, narrow_dots, narrow_posts = [], [], [], [], []
    for r in range(seq_tile // ROW_CHUNK):
        rows = slice(r * ROW_CHUNK, (r + 1) * ROW_CHUNK)

        def norm(r=r, rows=rows):
            xn[r] = _rms(x_ref[0, rows, :], nmix_ref[...]).astype(BF16)

        def dot_piece(name, start, width, r=r):
            def run():
                val[name, r] = jnp.dot(xn[r], win_ref[:, start:start + width], preferred_element_type=F32)
            return run

        def post_q(r=r):
            heads_t(val.pop(("q", r)), qgain_ref, qnt_ref, r)

        def post_kv(r=r):
            t = val.pop(("kvch", r))
            kv_a = t[:, :2 * KV_WIDTH]
            val["ch", r] = t[:, 2 * KV_WIDTH:]
            scale = _head_rms_scale(kv_a, gmat_ref[...])
            kn = kv_a[:, :KV_WIDTH] * scale[:, :KV_WIDTH] * kgain_ref[...]
            kpad_ref[wslot, BLOCK + r * ROW_CHUNK:BLOCK + (r + 1) * ROW_CHUNK, :] = kn.astype(BF16)
            vtpad_ref[wslot, :, BLOCK + r * ROW_CHUNK:BLOCK + (r + 1) * ROW_CHUNK] = (
                kv_a[:, KV_WIDTH:].T.astype(BF16))
            if r == 0:
                kpad_ref[wslot, 0:BLOCK, :] = carried(
                    lambda s: kpad_ref[s, seq_tile:seq_tile + BLOCK, :], (BLOCK, KV_WIDTH), BF16)
                vtpad_ref[wslot, :, 0:BLOCK] = carried(
                    lambda s: vtpad_ref[s, :, seq_tile:seq_tile + BLOCK], (KV_WIDTH, BLOCK), BF16)

        def post_conv(r=r):
            t = val.pop(("cbcc", r))
            cb_ref[wslot, r * ROW_CHUNK:(r + 1) * ROW_CHUNK, :] = t[:, :CONV_WIDTH]
            u = t[:, CONV_WIDTH:] * val.pop(("ch", r))
            upad_ref[wslot, V7X_SUBLANES + r * ROW_CHUNK:V7X_SUBLANES + (r + 1) * ROW_CHUNK, :] = u
            if r == 0:
                upad_ref[wslot, 0:V7X_SUBLANES, :] = carried(
                    lambda s: upad_ref[s, seq_tile:seq_tile + V7X_SUBLANES, :],
                    (V7X_SUBLANES, CONV_WIDTH), F32)

        def post_qm(r=r):
            heads_t(val.pop(("qm", r)), mqgain_ref, qmnt_ref, r)

        norms.append(norm)
        dots += [dot_piece("q", 0, COL_GROUP), dot_piece("kvch", ATTN_WIDTH, COL_GROUP),
                 dot_piece("cbcc", conv_base + CONV_WIDTH, COL_GROUP)]
        posts += [post_q, post_kv, post_conv]
        narrow_dots.append(dot_piece("qm", IN_PROJ_WIDTH - MEM_WIDTH, MEM_WIDTH))
        narrow_posts.append(post_qm)
    return norms, dots + narrow_dots, posts + narrow_posts


def _stage3_pieces(x1, gain_ref, wg_ref, wu_ref, wd_ref, out_ref):
    st = {}

    def norm():
        st["h"] = _rms(x1, gain_ref[...]).astype(BF16)
        st["y"] = x1

    def up_piece(c):
        cols = slice(c * FF_CHUNK, (c + 1) * FF_CHUNK)

        def run():
            st["g", c] = jnp.dot(st["h"], wg_ref[:, cols], preferred_element_type=F32)
            st["u", c] = jnp.dot(st["h"], wu_ref[:, cols], preferred_element_type=F32)
        return run

    def down_piece(c):
        cols = slice(c * FF_CHUNK, (c + 1) * FF_CHUNK)

        def run():
            gate = st.pop(("g", c))
            act = (gate * jax.nn.sigmoid(gate) * st.pop(("u", c))).astype(BF16)
            st["y"] = st["y"] + jnp.dot(act, wd_ref[cols, :], preferred_element_type=F32)
        return run

    def store():
        out_ref[0] = st["y"]

    n_chunks = wg_ref.shape[1] // FF_CHUNK
    return norm, [up_piece(c) for c in range(n_chunks)], [down_piece(c) for c in range(n_chunks)], store


def _layer_kernel(seq_tile, tiles_per_seq,
                  xcur_ref, xnext_ref, kblk_ref, vblkt_ref, win_ref, wout_ref, gmat_ref,
                  nmix_ref, qgain_ref, kgain_ref, mqgain_ref, sinks_ref, convw_ref, convb_ref,
                  ona_ref, onc_ref, onm_ref, nffn_ref, wg_ref, wu_ref, wd_ref,
                  out_ref,
                  qnt_ref, qmnt_ref, kpad_ref, vtpad_ref, upad_ref, cb_ref, x1_ref):
    t = pl.program_id(0)
    slot = t % 2
    n_blocks = seq_tile // BLOCK
    first_of_seq = (t % tiles_per_seq) == 0
    stage1_refs = (win_ref, gmat_ref, nmix_ref, qgain_ref, kgain_ref, mqgain_ref,
                   qnt_ref, qmnt_ref, kpad_ref, vtpad_ref, upad_ref, cb_ref)

    @pl.when(t == 0)
    def _():
        norms, dots, posts = _stage1_pieces(xcur_ref, 0, None, stage1_refs)
        for piece in norms + [p for pair in zip(dots, posts) for p in pair]:
            piece()
        x1_ref[1] = jnp.zeros(x1_ref.shape[1:], F32)

    ffn_norm, ffn_ups, ffn_downs, ffn_store = _stage3_pieces(
        x1_ref[1 - slot], nffn_ref, wg_ref, wu_ref, wd_ref, out_ref)
    chains = [(b, g) for b in range(n_blocks) for g in range(N_KV_HEADS)]
    vt_cat = {(b, g): vtpad_ref[slot, g * HEAD_DIM:(g + 1) * HEAD_DIM, b * BLOCK:(b + 2) * BLOCK]
              for b, g in chains}
    qmn = qmnt_ref[slot]
    n_halves = seq_tile // V7X_MXU_DIM
    u = upad_ref[slot, V7X_SUBLANES:V7X_SUBLANES + seq_tile, :]
    u1 = upad_ref[slot, V7X_SUBLANES - 1:V7X_SUBLANES - 1 + seq_tile, :]
    u2 = upad_ref[slot, V7X_SUBLANES - 2:V7X_SUBLANES - 2 + seq_tile, :]
    cw = convw_ref[...]
    conv = cw[0:1] * u2 + cw[1:2] * u1 + cw[2:3] * u + convb_ref[...]
    conv_n = _rms(cb_ref[slot] * conv, onc_ref[...])

    norms, dots, posts = _stage1_pieces(
        xnext_ref, 1 - slot, (slot, ((t + 1) % tiles_per_seq) == 0), stage1_refs)
    n_chains = len(chains)
    held = 2
    n_early = len(dots) - held
    chain_fill = [[] for _ in range(n_chains)]
    chain_fill[0] = [dots[0], dots[1]]
    for i in range(1, n_chains):
        chain_fill[i] = ([posts[i - 1]] if i - 1 < n_early else []) + ([dots[i + 1]] if i + 1 < n_early else [])
    tail_dots, tail_posts = dots[n_early:], posts[n_early:]
    ffn_lead = 2
    for i in range(n_chains):
        chain_fill[i] += [ffn_ups[i + ffn_lead], ffn_downs[i]]
    ffn_tail = [p for pair in zip(ffn_ups[n_chains + ffn_lead:], ffn_downs[n_chains:]) for p in pair]
    ffn_tail += ffn_downs[len(ffn_ups) - ffn_lead:]

    key = lax.broadcasted_iota(jnp.int32, (BLOCK, GQA_GROUP * BLOCK), 0)
    qry = lax.broadcasted_iota(jnp.int32, (BLOCK, GQA_GROUP * BLOCK), 1) % BLOCK
    from_prev = key > qry
    dist = jnp.where(from_prev, qry + BLOCK - key, qry - key).astype(F32)
    dist_first = jnp.where(jnp.logical_and(from_prev, first_of_seq), MASKED_DIST, dist)
    zeros_q = jnp.zeros((HEAD_DIM, GQA_GROUP * BLOCK), BF16)

    scores = {}
    for b, g in chains:
        cols = slice(b * BLOCK, (b + 1) * BLOCK)
        k_cat = kpad_ref[slot, b * BLOCK:(b + 2) * BLOCK, :]
        heads = range(g * GQA_GROUP, (g + 1) * GQA_GROUP)
        q4 = jnp.concatenate(
            [qnt_ref[slot, h * HEAD_DIM:(h + 1) * HEAD_DIM, cols] for h in heads], axis=1)
        w_q = jnp.concatenate([q4, zeros_q] if g == 0 else [zeros_q, q4], axis=0)
        scores[b, g] = jnp.dot(k_cat, w_q, preferred_element_type=F32)
    kblk = kblk_ref[0]
    vblkt = vblkt_ref[0]
    mem_scores = jnp.dot(kblk, qmn, preferred_element_type=F32)
    ffn_norm()
    for piece in ffn_ups[:ffn_lead]:
        piece()
    y_conv = jnp.dot(conv_n.astype(BF16), wout_ref[ATTN_WIDTH:ATTN_WIDTH + CONV_WIDTH, :],
                     preferred_element_type=F32)
    mem_units = [(c, h) for c in range(n_halves) for h in range(N_MEM_HEADS)]
    mem_p, mem_inv_l = {}, {}

    def mem_softmax(c, h):
        s = mem_scores[h * N_MEM:(h + 1) * N_MEM, c * V7X_MXU_DIM:(c + 1) * V7X_MXU_DIM]
        m = jnp.max(s, axis=0, keepdims=True)
        p = jnp.exp(s - m)
        mem_inv_l[c, h] = 1.0 / jnp.sum(p, axis=0, keepdims=True)
        mem_p[c, h] = p.astype(BF16)

    def finish_rows(c):
        rows = slice(c * V7X_MXU_DIM, (c + 1) * V7X_MXU_DIM)
        blocks = range(c * V7X_MXU_DIM // BLOCK, (c + 1) * V7X_MXU_DIM // BLOCK)
        attn_rows = jnp.concatenate(
            [jnp.concatenate([row for g in range(N_KV_HEADS) for row in head_rows[b, g]], axis=0)
             for b in blocks], axis=1).T
        p_all = jnp.concatenate([mem_p[c, h] for h in range(N_MEM_HEADS)], axis=0)
        o_t = jnp.dot(vblkt, p_all, preferred_element_type=F32)
        mem_rows = jnp.concatenate(
            [o_t[h * HEAD_DIM:(h + 1) * HEAD_DIM] * mem_inv_l[c, h] for h in range(N_MEM_HEADS)],
            axis=0).T
        y = y_conv[rows] + jnp.dot(_rms(attn_rows, ona_ref[...]).astype(BF16), wout_ref[0:ATTN_WIDTH, :],
                                   preferred_element_type=F32)
        y = y + jnp.dot(_rms(mem_rows, onm_ref[...]).astype(BF16), wout_ref[ATTN_WIDTH + CONV_WIDTH:, :],
                        preferred_element_type=F32)
        x1_ref[slot, rows, :] = xcur_ref[0, rows, :] + y

    for piece in norms:
        piece()
    head_rows = {}
    chains_per_half = n_chains // n_halves
    assert len(mem_units) == n_chains
    for i, (b, g) in enumerate(chains):
        heads = range(g * GQA_GROUP, (g + 1) * GQA_GROUP)
        dist_b = dist_first if b == 0 else dist
        neg_slope = jnp.concatenate(
            [jnp.full((1, BLOCK), -(2.0 ** (-8.0 * (h + 1) / N_ATTN_HEADS)), F32) for h in heads], axis=1)
        sink = sinks_ref[:, g * GQA_GROUP * BLOCK:(g + 1) * GQA_GROUP * BLOCK]
        s2 = scores.pop((b, g))
        s = jnp.where(from_prev, s2[:BLOCK], s2[BLOCK:]) + dist_b * neg_slope
        m = jnp.maximum(jnp.max(s, axis=0, keepdims=True), sink)
        p = jnp.exp(s - m)
        l = jnp.sum(p, axis=0, keepdims=True) + jnp.exp(sink - m)
        p_t = jnp.concatenate(
            [jnp.where(from_prev, p, 0.0).astype(BF16), jnp.where(from_prev, 0.0, p).astype(BF16)],
            axis=0)
        for piece in chain_fill[i]:
            piece()
        o_t = jnp.dot(vt_cat[b, g], p_t, preferred_element_type=F32) * (1.0 / l)
        head_rows[b, g] = [o_t[:, hh * BLOCK:(hh + 1) * BLOCK] for hh in range(GQA_GROUP)]
        mem_softmax(*mem_units[i])
        if i + 1 == chains_per_half:
            finish_rows(0)
    for piece in tail_dots + ffn_tail[:2]:
        piece()
    finish_rows(1)
    for piece in tail_posts + ffn_tail[2:]:
        piece()
    ffn_store()


def _const_spec(shape):
    return pl.BlockSpec(shape, lambda *_: (0,) * len(shape), pipeline_mode=pl.Buffered(1))


def _row(a):
    return a.reshape(1, -1)


def _head_column(gain, heads):
    return jnp.broadcast_to(jnp.tile(gain, heads)[:, None], (heads * HEAD_DIM, BLOCK))


def _layer(x, mem_blocks, lp, plan):
    seq_tile, vmem_limit = plan
    batch, seq, _ = x.shape
    kblk, vblkt = mem_blocks
    d_ff = lp["w_gate"].shape[1]
    tiles_per_seq = seq // seq_tile
    n_tiles = batch * tiles_per_seq
    last = n_tiles - 1

    def tile_index(t):
        return (t // tiles_per_seq, t % tiles_per_seq, 0)

    layer = pl.pallas_call(
        functools.partial(_layer_kernel, seq_tile, tiles_per_seq),
        out_shape=jax.ShapeDtypeStruct(x.shape, F32),
        grid=(n_tiles + 1,),
        in_specs=[
            pl.BlockSpec((1, seq_tile, D_MODEL), lambda t: tile_index(jnp.minimum(t, last))),
            pl.BlockSpec((1, seq_tile, D_MODEL), lambda t: tile_index(jnp.minimum(t + 1, last))),
            pl.BlockSpec((1, N_MEM_HEADS * N_MEM, MEM_WIDTH),
                         lambda t: (jnp.minimum(t, last) // tiles_per_seq, 0, 0)),
            pl.BlockSpec((1, MEM_WIDTH, N_MEM_HEADS * N_MEM),
                         lambda t: (jnp.minimum(t, last) // tiles_per_seq, 0, 0)),
            _const_spec((D_MODEL, IN_PROJ_WIDTH)),
            _const_spec((D_MODEL, D_MODEL)),
            _const_spec((V7X_MXU_DIM, V7X_MXU_DIM)),
            _const_spec((1, D_MODEL)),
            _const_spec((ATTN_WIDTH, BLOCK)),
            _const_spec((1, KV_WIDTH)),
            _const_spec((MEM_WIDTH, BLOCK)),
            _const_spec((1, N_ATTN_HEADS * BLOCK)),
            _const_spec((CONV_K, CONV_WIDTH)),
            _const_spec((1, CONV_WIDTH)),
            _const_spec((1, ATTN_WIDTH)),
            _const_spec((1, CONV_WIDTH)),
            _const_spec((1, MEM_WIDTH)),
            _const_spec((1, D_MODEL)),
            _const_spec((D_MODEL, d_ff)),
            _const_spec((D_MODEL, d_ff)),
            _const_spec((d_ff, D_MODEL)),
        ],
        out_specs=pl.BlockSpec((1, seq_tile, D_MODEL), lambda t: tile_index(jnp.maximum(t - 1, 0))),
        scratch_shapes=[
            pltpu.VMEM((2, ATTN_WIDTH, seq_tile), BF16),
            pltpu.VMEM((2, MEM_WIDTH, seq_tile), BF16),
            pltpu.VMEM((2, seq_tile + BLOCK, KV_WIDTH), BF16),
            pltpu.VMEM((2, KV_WIDTH, seq_tile + BLOCK), BF16),
            pltpu.VMEM((2, seq_tile + V7X_SUBLANES, CONV_WIDTH), F32),
            pltpu.VMEM((2, seq_tile, CONV_WIDTH), F32),
            pltpu.VMEM((2, seq_tile, D_MODEL), F32),
        ],
        compiler_params=pltpu.CompilerParams(
            dimension_semantics=("arbitrary",), vmem_limit_bytes=vmem_limit),
        name="layer",
    )
    return layer(
        x, x, kblk, vblkt, lp["w_in"], lp["w_out"], lp["gmat"],
        _row(lp["norm_mix"]), _head_column(lp["q_norm"], N_ATTN_HEADS),
        jnp.tile(_row(lp["k_norm"]), (1, N_KV_HEADS)), _head_column(lp["mem_q_norm"], N_MEM_HEADS),
        _row(jnp.repeat(lp["attn_sinks"], BLOCK)),
        lp["conv_w"], _row(lp["conv_b"]),
        _row(lp["out_norm_attn"]), _row(lp["out_norm_conv"]), _row(lp["out_norm_mem"]),
        _row(lp["norm_ffn"]), lp["w_gate"], lp["w_up"], lp["w_down"])


def _mem_kv(mem, lp, vmem_limit):
    batch = mem.shape[0]
    kblk_shape = (batch, N_MEM_HEADS * N_MEM, MEM_WIDTH)
    vblkt_shape = (batch, MEM_WIDTH, N_MEM_HEADS * N_MEM)
    call = pl.pallas_call(
        _mem_kv_kernel,
        out_shape=(jax.ShapeDtypeStruct(kblk_shape, BF16), jax.ShapeDtypeStruct(vblkt_shape, BF16)),
        grid=(batch,),
        in_specs=[
            pl.BlockSpec((1, N_MEM, D_MODEL), lambda b: (b, 0, 0)),
            _const_spec((1, D_MODEL)),
            _const_spec((D_MODEL, 2 * MEM_WIDTH)),
            _const_spec((1, MEM_WIDTH)),
            _const_spec((V7X_MXU_DIM, V7X_MXU_DIM)),
        ],
        out_specs=(pl.BlockSpec((1,) + kblk_shape[1:], lambda b: (b, 0, 0)),
                   pl.BlockSpec((1,) + vblkt_shape[1:], lambda b: (b, 0, 0))),
        compiler_params=pltpu.CompilerParams(
            dimension_semantics=("arbitrary",), vmem_limit_bytes=vmem_limit),
        name="mem_kv",
    )
    return call(mem, _row(lp["norm_mem"]), lp["w_mem_kv"],
                jnp.tile(_row(lp["mem_k_norm"]), (1, N_MEM_HEADS)), lp["gmat"])


def kernel(x, mem, norm_mix, w_in, q_norm, k_norm, attn_sinks, conv_w, conv_b, norm_mem, w_mem_kv,
           mem_q_norm, mem_k_norm, out_norm_attn, out_norm_conv, out_norm_mem, w_out, norm_ffn,
           w_gate, w_up, w_down):
    plan = _plan()
    head_of_lane = np.arange(V7X_MXU_DIM) // HEAD_DIM
    gmat = jnp.asarray(head_of_lane[:, None] == head_of_lane[None, :], dtype=BF16)
    depth = w_in.shape[0]
    for l in range(depth):
        lp = dict(
            norm_mix=norm_mix[l], w_in=w_in[l].astype(BF16), q_norm=q_norm[l], k_norm=k_norm[l],
            attn_sinks=attn_sinks[l], conv_w=conv_w[l], conv_b=conv_b[l], norm_mem=norm_mem[l],
            w_mem_kv=w_mem_kv[l].astype(BF16), mem_q_norm=mem_q_norm[l], mem_k_norm=mem_k_norm[l],
            out_norm_attn=out_norm_attn[l], out_norm_conv=out_norm_conv[l],
            out_norm_mem=out_norm_mem[l], w_out=w_out[l].astype(BF16), norm_ffn=norm_ffn[l],
            w_gate=w_gate[l].astype(BF16), w_up=w_up[l].astype(BF16), w_down=w_down[l].astype(BF16),
            gmat=gmat)
        x = _layer(x, _mem_kv(mem, lp, plan[1]), lp, plan)
    return x
```

```python
import functools

import jax
import jax.numpy as jnp
import numpy as np
from jax import lax
from jax.experimental import pallas as pl
from jax.experimental.pallas import tpu as pltpu

D_MODEL = 1024
HEAD_DIM = 64
N_ATTN_HEADS = 8
N_KV_HEADS = 2
GQA_GROUP = N_ATTN_HEADS // N_KV_HEADS
BLOCK = 128
N_MEM_HEADS = 4
N_MEM = 256
CONV_K = 3
ATTN_WIDTH = N_ATTN_HEADS * HEAD_DIM
KV_WIDTH = N_KV_HEADS * HEAD_DIM
CONV_WIDTH = 256
MEM_WIDTH = N_MEM_HEADS * HEAD_DIM
IN_PROJ_WIDTH = ATTN_WIDTH + 2 * KV_WIDTH + 3 * CONV_WIDTH + MEM_WIDTH
EPS = 1e-6
MASKED_DIST = 2.0 ** 110

V7X_VMEM_BYTES = 64 * 1024 * 1024
V7X_SUBLANES = 8
V7X_LANES = 128
V7X_MXU_DIM = 256

F32 = jnp.float32
BF16 = jnp.bfloat16


def _plan():
    seq_tile = 4 * BLOCK
    ffn_tile = 1024
    vmem_limit = V7X_VMEM_BYTES - 8 * 1024 * 1024
    return seq_tile, ffn_tile, vmem_limit


def _rms(a, gain):
    return a * lax.rsqrt(jnp.mean(a * a, axis=-1, keepdims=True) + EPS) * gain


def _head_rms_scale(t, gmat):
    sq = (t * t).astype(BF16)
    ss = jnp.dot(sq, gmat, preferred_element_type=F32)
    return lax.rsqrt(ss * (1.0 / HEAD_DIM) + EPS)


def _head_rms_rows(t, gain):
    heads = t.shape[0] // HEAD_DIM
    out = []
    for h in range(heads):
        th = t[h * HEAD_DIM:(h + 1) * HEAD_DIM]
        ss = jnp.sum(th * th, axis=0, keepdims=True)
        out.append(th * lax.rsqrt(ss * (1.0 / HEAD_DIM) + EPS) * gain[h * HEAD_DIM:(h + 1) * HEAD_DIM])
    return out


def _mem_kv_kernel(mem_ref, gain_ref, w_ref, kgain_ref, gmat_ref, kblk_ref, vblkt_ref):
    m = mem_ref[0]
    mn = _rms(m, gain_ref[...]).astype(BF16)
    kv = jnp.dot(mn, w_ref[...], preferred_element_type=F32)
    k = kv[:, :MEM_WIDTH]
    vt = kv[:, MEM_WIDTH:].T
    kn = k * _head_rms_scale(k, gmat_ref[...]) * kgain_ref[...]
    lane_head = lax.broadcasted_iota(jnp.int32, (N_MEM, MEM_WIDTH), 1) // HEAD_DIM
    row_head = lax.broadcasted_iota(jnp.int32, (MEM_WIDTH, N_MEM), 0) // HEAD_DIM
    for h in range(N_MEM_HEADS):
        kblk_ref[0, h * N_MEM:(h + 1) * N_MEM, :] = jnp.where(lane_head == h, kn, 0.0).astype(BF16)
        vblkt_ref[0, :, h * N_MEM:(h + 1) * N_MEM] = jnp.where(row_head == h, vt, 0.0).astype(BF16)


ROW_CHUNK = V7X_MXU_DIM
COL_GROUP = 2 * V7X_MXU_DIM


def _stage1_pieces(x_ref, wslot, carry, refs):
    (win_ref, gmat_ref, nmix_ref, qgain_ref, kgain_ref, mqgain_ref,
     qnt_ref, qmnt_ref, kpad_ref, vtpad_ref, upad_ref, cb_ref, xres_ref) = refs
    seq_tile = cb_ref.shape[1]
    xn, val = {}, {}
    conv_base = ATTN_WIDTH + 2 * KV_WIDTH

    def carried(ref_slice_fn, shape, dtype):
        if carry is None:
            return jnp.zeros(shape, dtype)
        prev_slot, first_of_seq = carry
        return jnp.where(first_of_seq, jnp.zeros(shape, dtype), ref_slice_fn(prev_slot))

    def heads_t(t, gain_ref, out_ref, r):
        t = t.T
        gain = gain_ref[...] * HEAD_DIM ** -0.5
        gain = jnp.concatenate([gain] * (ROW_CHUNK // BLOCK), axis=1)
        out_ref[wslot, :, r * ROW_CHUNK:(r + 1) * ROW_CHUNK] = jnp.concatenate(
            _head_rms_rows(t, gain), axis=0).astype(BF16)

    norms, dots, posts, narrow_dots, narrow_posts = [], [], [], [], []
    for r in range(seq_tile // ROW_CHUNK):
        rows = slice(r * ROW_CHUNK, (r + 1) * ROW_CHUNK)

        def norm(r=r, rows=rows):
            x = x_ref[0, rows, :]
            xres_ref[wslot, rows, :] = x
            xn[r] = _rms(x, nmix_ref[...]).astype(BF16)

        def dot_piece(name, start, width, r=r):
            def run():
                val[name, r] = jnp.dot(xn[r], win_ref[:, start:start + width], preferred_element_type=F32)
            return run

        def post_q(r=r):
            heads_t(val.pop(("q", r)), qgain_ref, qnt_ref, r)

        def post_kv(r=r):
            t = val.pop(("kvch", r))
            kv_a = t[:, :2 * KV_WIDTH]
            val["ch", r] = t[:, 2 * KV_WIDTH:]
            scale = _head_rms_scale(kv_a, gmat_ref[...])
            kn = kv_a[:, :KV_WIDTH] * scale[:, :KV_WIDTH] * kgain_ref[...]
            kpad_ref[wslot, BLOCK + r * ROW_CHUNK:BLOCK + (r + 1) * ROW_CHUNK, :] = kn.astype(BF16)
            vtpad_ref[wslot, :, BLOCK + r * ROW_CHUNK:BLOCK + (r + 1) * ROW_CHUNK] = (
                kv_a[:, KV_WIDTH:].T.astype(BF16))
            if r == 0:
                kpad_ref[wslot, 0:BLOCK, :] = carried(
                    lambda s: kpad_ref[s, seq_tile:seq_tile + BLOCK, :], (BLOCK, KV_WIDTH), BF16)
                vtpad_ref[wslot, :, 0:BLOCK] = carried(
                    lambda s: vtpad_ref[s, :, seq_tile:seq_tile + BLOCK], (KV_WIDTH, BLOCK), BF16)

        def post_conv(r=r):
            t = val.pop(("cbcc", r))
            cb_ref[wslot, r * ROW_CHUNK:(r + 1) * ROW_CHUNK, :] = t[:, :CONV_WIDTH]
            u = t[:, CONV_WIDTH:] * val.pop(("ch", r))
            upad_ref[wslot, V7X_SUBLANES + r * ROW_CHUNK:V7X_SUBLANES + (r + 1) * ROW_CHUNK, :] = u
            if r == 0:
                upad_ref[wslot, 0:V7X_SUBLANES, :] = carried(
                    lambda s: upad_ref[s, seq_tile:seq_tile + V7X_SUBLANES, :],
                    (V7X_SUBLANES, CONV_WIDTH), F32)

        def post_qm(r=r):
            heads_t(val.pop(("qm", r)), mqgain_ref, qmnt_ref, r)

        norms.append(norm)
        dots += [dot_piece("q", 0, COL_GROUP), dot_piece("kvch", ATTN_WIDTH, COL_GROUP),
                 dot_piece("cbcc", conv_base + CONV_WIDTH, COL_GROUP)]
        posts += [post_q, post_kv, post_conv]
        narrow_dots.append(dot_piece("qm", IN_PROJ_WIDTH - MEM_WIDTH, MEM_WIDTH))
        narrow_posts.append(post_qm)
    return norms, dots + narrow_dots, posts + narrow_posts


def _mixer_kernel(seq_tile, tiles_per_seq,
                  xfirst_ref, xnext_ref, kblk_ref, vblkt_ref, win_ref, wout_ref, gmat_ref,
                  nmix_ref, qgain_ref, kgain_ref, mqgain_ref, sinks_ref, convw_ref, convb_ref,
                  ona_ref, onc_ref, onm_ref,
                  out_ref,
                  qnt_ref, qmnt_ref, kpad_ref, vtpad_ref, upad_ref, cb_ref, xres_ref):
    t = pl.program_id(0)
    slot = t % 2
    n_blocks = seq_tile // BLOCK
    first_of_seq = (t % tiles_per_seq) == 0
    stage1_refs = (win_ref, gmat_ref, nmix_ref, qgain_ref, kgain_ref, mqgain_ref,
                   qnt_ref, qmnt_ref, kpad_ref, vtpad_ref, upad_ref, cb_ref, xres_ref)

    @pl.when(t == 0)
    def _():
        norms, dots, posts = _stage1_pieces(xfirst_ref, 0, None, stage1_refs)
        for piece in norms + [p for pair in zip(dots, posts) for p in pair]:
            piece()

    norms, dots, posts = _stage1_pieces(
        xnext_ref, 1 - slot, (slot, ((t + 1) % tiles_per_seq) == 0), stage1_refs)
    n_chains = n_blocks * N_KV_HEADS
    held = 2
    chain_fill = [[] for _ in range(n_chains)]
    chain_fill[0] = [dots[0], dots[1]]
    n_early = len(dots) - held
    for i in range(1, n_chains):
        chain_fill[i] = ([posts[i - 1]] if i - 1 < n_early else []) + ([dots[i + 1]] if i + 1 < n_early else [])
    tail_dots, tail_posts = dots[n_early:], posts[n_early:]

    key = lax.broadcasted_iota(jnp.int32, (BLOCK, GQA_GROUP * BLOCK), 0)
    qry = lax.broadcasted_iota(jnp.int32, (BLOCK, GQA_GROUP * BLOCK), 1) % BLOCK
    from_prev = key > qry
    dist = jnp.where(from_prev, qry + BLOCK - key, qry - key).astype(F32)
    dist_first = jnp.where(jnp.logical_and(from_prev, first_of_seq), MASKED_DIST, dist)
    zeros_q = jnp.zeros((HEAD_DIM, GQA_GROUP * BLOCK), BF16)

    chains = [(b, g) for b in range(n_blocks) for g in range(N_KV_HEADS)]
    vt_cat = {(b, g): vtpad_ref[slot, g * HEAD_DIM:(g + 1) * HEAD_DIM, b * BLOCK:(b + 2) * BLOCK]
              for b, g in chains}
    qmn = qmnt_ref[slot]
    n_halves = seq_tile // V7X_MXU_DIM
    u = upad_ref[slot, V7X_SUBLANES:V7X_SUBLANES + seq_tile, :]
    u1 = upad_ref[slot, V7X_SUBLANES - 1:V7X_SUBLANES - 1 + seq_tile, :]
    u2 = upad_ref[slot, V7X_SUBLANES - 2:V7X_SUBLANES - 2 + seq_tile, :]
    cw = convw_ref[...]
    conv = cw[0:1] * u2 + cw[1:2] * u1 + cw[2:3] * u + convb_ref[...]
    conv_n = _rms(cb_ref[slot] * conv, onc_ref[...])

    scores = {}
    for b, g in chains:
        cols = slice(b * BLOCK, (b + 1) * BLOCK)
        k_cat = kpad_ref[slot, b * BLOCK:(b + 2) * BLOCK, :]
        heads = range(g * GQA_GROUP, (g + 1) * GQA_GROUP)
        q4 = jnp.concatenate(
            [qnt_ref[slot, h * HEAD_DIM:(h + 1) * HEAD_DIM, cols] for h in heads], axis=1)
        w_q = jnp.concatenate([q4, zeros_q] if g == 0 else [zeros_q, q4], axis=0)
        scores[b, g] = jnp.dot(k_cat, w_q, preferred_element_type=F32)
    kblk = kblk_ref[0]
    vblkt = vblkt_ref[0]
    mem_scores = jnp.dot(kblk, qmn, preferred_element_type=F32)
    y_conv = jnp.dot(conv_n.astype(BF16), wout_ref[ATTN_WIDTH:ATTN_WIDTH + CONV_WIDTH, :],
                     preferred_element_type=F32)
    mem_units = [(c, h) for c in range(n_halves) for h in range(N_MEM_HEADS)]
    mem_p, mem_inv_l = {}, {}

    def mem_softmax(c, h):
        s = mem_scores[h * N_MEM:(h + 1) * N_MEM, c * V7X_MXU_DIM:(c + 1) * V7X_MXU_DIM]
        m = jnp.max(s, axis=0, keepdims=True)
        p = jnp.exp(s - m)
        mem_inv_l[c, h] = 1.0 / jnp.sum(p, axis=0, keepdims=True)
        mem_p[c, h] = p.astype(BF16)

    def finish_rows(c):
        rows = slice(c * V7X_MXU_DIM, (c + 1) * V7X_MXU_DIM)
        blocks = range(c * V7X_MXU_DIM // BLOCK, (c + 1) * V7X_MXU_DIM // BLOCK)
        attn_rows = jnp.concatenate(
            [jnp.concatenate([row for g in range(N_KV_HEADS) for row in head_rows[b, g]], axis=0)
             for b in blocks], axis=1).T
        p_all = jnp.concatenate([mem_p[c, h] for h in range(N_MEM_HEADS)], axis=0)
        o_t = jnp.dot(vblkt, p_all, preferred_element_type=F32)
        mem_rows = jnp.concatenate(
            [o_t[h * HEAD_DIM:(h + 1) * HEAD_DIM] * mem_inv_l[c, h] for h in range(N_MEM_HEADS)],
            axis=0).T
        y = y_conv[rows] + jnp.dot(_rms(attn_rows, ona_ref[...]).astype(BF16), wout_ref[0:ATTN_WIDTH, :],
                                   preferred_element_type=F32)
        y = y + jnp.dot(_rms(mem_rows, onm_ref[...]).astype(BF16), wout_ref[ATTN_WIDTH + CONV_WIDTH:, :],
                        preferred_element_type=F32)
        out_ref[0, rows, :] = xres_ref[slot, rows, :] + y

    for piece in norms:
        piece()
    head_rows = {}
    chains_per_half = len(chains) // n_halves
    assert len(mem_units) == len(chains)
    for i, (b, g) in enumerate(chains):
        heads = range(g * GQA_GROUP, (g + 1) * GQA_GROUP)
        dist_b = dist_first if b == 0 else dist
        neg_slope = jnp.concatenate(
            [jnp.full((1, BLOCK), -(2.0 ** (-8.0 * (h + 1) / N_ATTN_HEADS)), F32) for h in heads], axis=1)
        sink = sinks_ref[:, g * GQA_GROUP * BLOCK:(g + 1) * GQA_GROUP * BLOCK]
        s2 = scores.pop((b, g))
        s = jnp.where(from_prev, s2[:BLOCK], s2[BLOCK:]) + dist_b * neg_slope
        m = jnp.maximum(jnp.max(s, axis=0, keepdims=True), sink)
        p = jnp.exp(s - m)
        l = jnp.sum(p, axis=0, keepdims=True) + jnp.exp(sink - m)
        p_t = jnp.concatenate(
            [jnp.where(from_prev, p, 0.0).astype(BF16), jnp.where(from_prev, 0.0, p).astype(BF16)],
            axis=0)
        for piece in chain_fill[i]:
            piece()
        o_t = jnp.dot(vt_cat[b, g], p_t, preferred_element_type=F32) * (1.0 / l)
        head_rows[b, g] = [o_t[:, hh * BLOCK:(hh + 1) * BLOCK] for hh in range(GQA_GROUP)]
        mem_softmax(*mem_units[i])
        if i + 1 == chains_per_half:
            finish_rows(0)
    for piece in tail_dots:
        piece()
    finish_rows(1)
    for piece in tail_posts:
        piece()


def _ffn_kernel(x_ref, gain_ref, wg_ref, wu_ref, wd_ref, out_ref):
    x = x_ref[...]
    h = _rms(x, gain_ref[...]).astype(BF16)
    y = x
    for c in range(wg_ref.shape[1] // V7X_MXU_DIM):
        cols = slice(c * V7X_MXU_DIM, (c + 1) * V7X_MXU_DIM)
        gate = jnp.dot(h, wg_ref[:, cols], preferred_element_type=F32)
        up = jnp.dot(h, wu_ref[:, cols], preferred_element_type=F32)
        act = (gate * jax.nn.sigmoid(gate) * up).astype(BF16)
        y = y + jnp.dot(act, wd_ref[cols, :], preferred_element_type=F32)
    out_ref[...] = y


def _const_spec(shape):
    return pl.BlockSpec(shape, lambda *_: (0,) * len(shape), pipeline_mode=pl.Buffered(1))


def _row(a):
    return a.reshape(1, -1)


def _head_column(gain, heads):
    return jnp.broadcast_to(jnp.tile(gain, heads)[:, None], (heads * HEAD_DIM, BLOCK))


def _layer(x, mem_blocks, lp, plan):
    seq_tile, ffn_tile, vmem_limit = plan
    batch, seq, _ = x.shape
    kblk, vblkt = mem_blocks

    tiles_per_seq = seq // seq_tile
    n_tiles = batch * tiles_per_seq

    def tile_index(t):
        return (t // tiles_per_seq, t % tiles_per_seq, 0)

    mixer = pl.pallas_call(
        functools.partial(_mixer_kernel, seq_tile, tiles_per_seq),
        out_shape=jax.ShapeDtypeStruct(x.shape, F32),
        grid=(n_tiles,),
        in_specs=[
            _const_spec((1, seq_tile, D_MODEL)),
            pl.BlockSpec((1, seq_tile, D_MODEL), lambda t: tile_index(jnp.minimum(t + 1, n_tiles - 1))),
            pl.BlockSpec((1, N_MEM_HEADS * N_MEM, MEM_WIDTH), lambda t: (t // tiles_per_seq, 0, 0)),
            pl.BlockSpec((1, MEM_WIDTH, N_MEM_HEADS * N_MEM), lambda t: (t // tiles_per_seq, 0, 0)),
            _const_spec((D_MODEL, IN_PROJ_WIDTH)),
            _const_spec((D_MODEL, D_MODEL)),
            _const_spec((V7X_MXU_DIM, V7X_MXU_DIM)),
            _const_spec((1, D_MODEL)),
            _const_spec((ATTN_WIDTH, BLOCK)),
            _const_spec((1, KV_WIDTH)),
            _const_spec((MEM_WIDTH, BLOCK)),
            _const_spec((1, N_ATTN_HEADS * BLOCK)),
            _const_spec((CONV_K, CONV_WIDTH)),
            _const_spec((1, CONV_WIDTH)),
            _const_spec((1, ATTN_WIDTH)),
            _const_spec((1, CONV_WIDTH)),
            _const_spec((1, MEM_WIDTH)),
        ],
        out_specs=pl.BlockSpec((1, seq_tile, D_MODEL), tile_index),
        scratch_shapes=[
            pltpu.VMEM((2, ATTN_WIDTH, seq_tile), BF16),
            pltpu.VMEM((2, MEM_WIDTH, seq_tile), BF16),
            pltpu.VMEM((2, seq_tile + BLOCK, KV_WIDTH), BF16),
            pltpu.VMEM((2, KV_WIDTH, seq_tile + BLOCK), BF16),
            pltpu.VMEM((2, seq_tile + V7X_SUBLANES, CONV_WIDTH), F32),
            pltpu.VMEM((2, seq_tile, CONV_WIDTH), F32),
            pltpu.VMEM((2, seq_tile, D_MODEL), F32),
        ],
        compiler_params=pltpu.CompilerParams(
            dimension_semantics=("arbitrary",), vmem_limit_bytes=vmem_limit),
        name="mixer",
    )
    x1 = mixer(
        x, x, kblk, vblkt, lp["w_in"], lp["w_out"], lp["gmat"],
        _row(lp["norm_mix"]), _head_column(lp["q_norm"], N_ATTN_HEADS),
        jnp.tile(_row(lp["k_norm"]), (1, N_KV_HEADS)), _head_column(lp["mem_q_norm"], N_MEM_HEADS),
        _row(jnp.repeat(lp["attn_sinks"], BLOCK)),
        lp["conv_w"], _row(lp["conv_b"]),
        _row(lp["out_norm_attn"]), _row(lp["out_norm_conv"]), _row(lp["out_norm_mem"]))

    tokens = batch * seq
    d_ff = lp["w_gate"].shape[1]
    ffn = pl.pallas_call(
        _ffn_kernel,
        out_shape=jax.ShapeDtypeStruct((tokens, D_MODEL), F32),
        grid=(tokens // ffn_tile,),
        in_specs=[
            pl.BlockSpec((ffn_tile, D_MODEL), lambda i: (i, 0)),
            _const_spec((1, D_MODEL)),
            _const_spec((D_MODEL, d_ff)),
            _const_spec((D_MODEL, d_ff)),
            _const_spec((d_ff, D_MODEL)),
        ],
        out_specs=pl.BlockSpec((ffn_tile, D_MODEL), lambda i: (i, 0)),
        compiler_params=pltpu.CompilerParams(
            dimension_semantics=("arbitrary",), vmem_limit_bytes=vmem_limit),
        name="ffn",
    )
    y = ffn(x1.reshape(tokens, D_MODEL), _row(lp["norm_ffn"]), lp["w_gate"], lp["w_up"], lp["w_down"])
    return y.reshape(x.shape)


def _mem_kv(mem, lp, vmem_limit):
    batch = mem.shape[0]
    kblk_shape = (batch, N_MEM_HEADS * N_MEM, MEM_WIDTH)
    vblkt_shape = (batch, MEM_WIDTH, N_MEM_HEADS * N_MEM)
    call = pl.pallas_call(
        _mem_kv_kernel,
        out_shape=(jax.ShapeDtypeStruct(kblk_shape, BF16), jax.ShapeDtypeStruct(vblkt_shape, BF16)),
        grid=(batch,),
        in_specs=[
            pl.BlockSpec((1, N_MEM, D_MODEL), lambda b: (b, 0, 0)),
            _const_spec((1, D_MODEL)),
            _const_spec((D_MODEL, 2 * MEM_WIDTH)),
            _const_spec((1, MEM_WIDTH)),
            _const_spec((V7X_MXU_DIM, V7X_MXU_DIM)),
        ],
        out_specs=(pl.BlockSpec((1,) + kblk_shape[1:], lambda b: (b, 0, 0)),
                   pl.BlockSpec((1,) + vblkt_shape[1:], lambda b: (b, 0, 0))),
        compiler_params=pltpu.CompilerParams(
            dimension_semantics=("arbitrary",), vmem_limit_bytes=vmem_limit),
        name="mem_kv",
    )
    return call(mem, _row(lp["norm_mem"]), lp["w_mem_kv"],
                jnp.tile(_row(lp["mem_k_norm"]), (1, N_MEM_HEADS)), lp["gmat"])


def kernel(x, mem, norm_mix, w_in, q_norm, k_norm, attn_sinks, conv_w, conv_b, norm_mem, w_mem_kv,
           mem_q_norm, mem_k_norm, out_norm_attn, out_norm_conv, out_norm_mem, w_out, norm_ffn,
           w_gate, w_up, w_down):
    plan = _plan()
    head_of_lane = np.arange(V7X_MXU_DIM) // HEAD_DIM
    gmat = jnp.asarray(head_of_lane[:, None] == head_of_lane[None, :], dtype=BF16)
    depth = w_in.shape[0]
    for l in range(depth):
        lp = dict(
            norm_mix=norm_mix[l], w_in=w_in[l].astype(BF16), q_norm=q_norm[l], k_norm=k_norm[l],
            attn_sinks=attn_sinks[l], conv_w=conv_w[l], conv_b=conv_b[l], norm_mem=norm_mem[l],
            w_mem_kv=w_mem_kv[l].astype(BF16), mem_q_norm=mem_q_norm[l], mem_k_norm=mem_k_norm[l],
            out_norm_attn=out_norm_attn[l], out_norm_conv=out_norm_conv[l],
            out_norm_mem=out_norm_mem[l], w_out=w_out[l].astype(BF16), norm_ffn=norm_ffn[l],
            w_gate=w_gate[l].astype(BF16), w_up=w_up[l].astype(BF16), w_down=w_down[l].astype(BF16),
            gmat=gmat)
        x = _layer(x, _mem_kv(mem, lp, plan[2]), lp, plan)
    return x
```

```python
import functools

import jax
import jax.numpy as jnp
import numpy as np
from jax import lax
from jax.experimental import pallas as pl
from jax.experimental.pallas import tpu as pltpu

D_MODEL = 1024
HEAD_DIM = 64
N_ATTN_HEADS = 8
N_KV_HEADS = 2
GQA_GROUP = N_ATTN_HEADS // N_KV_HEADS
BLOCK = 128
N_MEM_HEADS = 4
N_MEM = 256
CONV_K = 3
ATTN_WIDTH = N_ATTN_HEADS * HEAD_DIM
KV_WIDTH = N_KV_HEADS * HEAD_DIM
CONV_WIDTH = 256
MEM_WIDTH = N_MEM_HEADS * HEAD_DIM
IN_PROJ_WIDTH = ATTN_WIDTH + 2 * KV_WIDTH + 3 * CONV_WIDTH + MEM_WIDTH
EPS = 1e-6
MASKED_DIST = 2.0 ** 110

V7X_VMEM_BYTES = 64 * 1024 * 1024
V7X_SUBLANES = 8
V7X_LANES = 128
V7X_MXU_DIM = 256

F32 = jnp.float32
BF16 = jnp.bfloat16


def _plan():
    seq_tile = 4 * BLOCK
    ffn_tile = 1024
    vmem_limit = V7X_VMEM_BYTES - 8 * 1024 * 1024
    return seq_tile, ffn_tile, vmem_limit


def _rms(a, gain):
    return a * lax.rsqrt(jnp.mean(a * a, axis=-1, keepdims=True) + EPS) * gain


def _head_rms_scale(t, gmat):
    sq = (t * t).astype(BF16)
    ss = jnp.dot(sq, gmat, preferred_element_type=F32)
    return lax.rsqrt(ss * (1.0 / HEAD_DIM) + EPS)


def _head_rms_rows(t, gain):
    heads = t.shape[0] // HEAD_DIM
    out = []
    for h in range(heads):
        th = t[h * HEAD_DIM:(h + 1) * HEAD_DIM]
        ss = jnp.sum(th * th, axis=0, keepdims=True)
        out.append(th * lax.rsqrt(ss * (1.0 / HEAD_DIM) + EPS) * gain[h * HEAD_DIM:(h + 1) * HEAD_DIM])
    return out


def _mem_kv_kernel(mem_ref, gain_ref, w_ref, kgain_ref, gmat_ref, win_ref, wout_ref,
                   kblk_ref, vblkt_ref, win16_ref, wout16_ref):
    win16_ref[...] = win_ref[...].astype(BF16)
    wout16_ref[...] = wout_ref[...].astype(BF16)
    n_seq = mem_ref.shape[0]
    m = mem_ref[...].reshape(n_seq * N_MEM, D_MODEL)
    mn = _rms(m, gain_ref[...]).astype(BF16)
    kv = jnp.dot(mn, w_ref[...].astype(BF16), preferred_element_type=F32)
    k = kv[:, :MEM_WIDTH]
    kn = k * _head_rms_scale(k, gmat_ref[...]) * kgain_ref[...]
    lane_head = lax.broadcasted_iota(jnp.int32, (N_MEM, MEM_WIDTH), 1) // HEAD_DIM
    row_head = lax.broadcasted_iota(jnp.int32, (MEM_WIDTH, N_MEM), 0) // HEAD_DIM
    for s in range(n_seq):
        rows = slice(s * N_MEM, (s + 1) * N_MEM)
        kn_s = kn[rows]
        vt = kv[rows, MEM_WIDTH:].T
        for h in range(N_MEM_HEADS):
            kblk_ref[s, h * N_MEM:(h + 1) * N_MEM, :] = jnp.where(lane_head == h, kn_s, 0.0).astype(BF16)
            vblkt_ref[s, :, h * N_MEM:(h + 1) * N_MEM] = jnp.where(row_head == h, vt, 0.0).astype(BF16)


ROW_CHUNK = V7X_MXU_DIM
COL_GROUP = 2 * V7X_MXU_DIM


def _stage1_pieces(x_ref, wslot, carry, refs):
    (win_ref, gmat_ref, nmix_ref, qgain_ref, kgain_ref, mqgain_ref,
     qnt_ref, qmnt_ref, kpad_ref, vtpad_ref, upad_ref, cb_ref, xres_ref) = refs
    seq_tile = cb_ref.shape[1]
    xn, val = {}, {}
    conv_base = ATTN_WIDTH + 2 * KV_WIDTH

    def carried(ref_slice_fn, shape, dtype):
        if carry is None:
            return jnp.zeros(shape, dtype)
        prev_slot, first_of_seq = carry
        return jnp.where(first_of_seq, jnp.zeros(shape, dtype), ref_slice_fn(prev_slot))

    def heads_t(t, gain_ref, out_ref, r):
        t = t.T
        gain = gain_ref[...] * HEAD_DIM ** -0.5
        gain = jnp.concatenate([gain] * (ROW_CHUNK // BLOCK), axis=1)
        out_ref[wslot, :, r * ROW_CHUNK:(r + 1) * ROW_CHUNK] = jnp.concatenate(
            _head_rms_rows(t, gain), axis=0).astype(BF16)

    norms, dots, posts, narrow_dots, narrow_posts = [], [], [], [], []
    for r in range(seq_tile // ROW_CHUNK):
        rows = slice(r * ROW_CHUNK, (r + 1) * ROW_CHUNK)

        def norm(r=r, rows=rows):
            x = x_ref[0, rows, :]
            xres_ref[wslot, rows, :] = x
            xn[r] = _rms(x, nmix_ref[...]).astype(BF16)

        def dot_piece(name, start, width, r=r):
            def run():
                val[name, r] = jnp.dot(xn[r], win_ref[:, start:start + width], preferred_element_type=F32)
            return run

        def post_q(r=r):
            heads_t(val.pop(("q", r)), qgain_ref, qnt_ref, r)

        def post_kv(r=r):
            t = val.pop(("kvch", r))
            kv_a = t[:, :2 * KV_WIDTH]
            val["ch", r] = t[:, 2 * KV_WIDTH:]
            scale = _head_rms_scale(kv_a, gmat_ref[...])
            kn = kv_a[:, :KV_WIDTH] * scale[:, :KV_WIDTH] * kgain_ref[...]
            kpad_ref[wslot, BLOCK + r * ROW_CHUNK:BLOCK + (r + 1) * ROW_CHUNK, :] = kn.astype(BF16)
            vtpad_ref[wslot, :, BLOCK + r * ROW_CHUNK:BLOCK + (r + 1) * ROW_CHUNK] = (
                kv_a[:, KV_WIDTH:].T.astype(BF16))
            if r == 0:
                kpad_ref[wslot, 0:BLOCK, :] = carried(
                    lambda s: kpad_ref[s, seq_tile:seq_tile + BLOCK, :], (BLOCK, KV_WIDTH), BF16)
                vtpad_ref[wslot, :, 0:BLOCK] = carried(
                    lambda s: vtpad_ref[s, :, seq_tile:seq_tile + BLOCK], (KV_WIDTH, BLOCK), BF16)

        def post_conv(r=r):
            t = val.pop(("cbcc", r))
            cb_ref[wslot, r * ROW_CHUNK:(r + 1) * ROW_CHUNK, :] = t[:, :CONV_WIDTH]
            u = t[:, CONV_WIDTH:] * val.pop(("ch", r))
            upad_ref[wslot, V7X_SUBLANES + r * ROW_CHUNK:V7X_SUBLANES + (r + 1) * ROW_CHUNK, :] = u
            if r == 0:
                upad_ref[wslot, 0:V7X_SUBLANES, :] = carried(
                    lambda s: upad_ref[s, seq_tile:seq_tile + V7X_SUBLANES, :],
                    (V7X_SUBLANES, CONV_WIDTH), F32)

        def post_qm(r=r):
            heads_t(val.pop(("qm", r)), mqgain_ref, qmnt_ref, r)

        norms.append(norm)
        dots += [dot_piece("q", 0, COL_GROUP), dot_piece("kvch", ATTN_WIDTH, COL_GROUP),
                 dot_piece("cbcc", conv_base + CONV_WIDTH, COL_GROUP)]
        posts += [post_q, post_kv, post_conv]
        narrow_dots.append(dot_piece("qm", IN_PROJ_WIDTH - MEM_WIDTH, MEM_WIDTH))
        narrow_posts.append(post_qm)
    return norms, dots + narrow_dots, posts + narrow_posts


def _mixer_kernel(seq_tile, tiles_per_seq,
                  xfirst_ref, xnext_ref, kblk_ref, vblkt_ref, win_ref, wout_ref, gmat_ref,
                  nmix_ref, qgain_ref, kgain_ref, mqgain_ref, sinks_ref, convw_ref, convb_ref,
                  ona_ref, onc_ref, onm_ref, wg_ref, wu_ref, wd_ref,
                  out_ref, wg16_ref, wu16_ref, wd16_ref,
                  qnt_ref, qmnt_ref, kpad_ref, vtpad_ref, upad_ref, cb_ref, xres_ref):
    wg16_ref[...] = wg_ref[...].astype(BF16)
    wu16_ref[...] = wu_ref[...].astype(BF16)
    wd16_ref[...] = wd_ref[...].astype(BF16)
    t = pl.program_id(0)
    slot = t % 2
    n_blocks = seq_tile // BLOCK
    first_of_seq = (t % tiles_per_seq) == 0
    stage1_refs = (win_ref, gmat_ref, nmix_ref, qgain_ref, kgain_ref, mqgain_ref,
                   qnt_ref, qmnt_ref, kpad_ref, vtpad_ref, upad_ref, cb_ref, xres_ref)

    @pl.when(t == 0)
    def _():
        norms, dots, posts = _stage1_pieces(xfirst_ref, 0, None, stage1_refs)
        for piece in norms + [p for pair in zip(dots, posts) for p in pair]:
            piece()

    norms, dots, posts = _stage1_pieces(
        xnext_ref, 1 - slot, (slot, ((t + 1) % tiles_per_seq) == 0), stage1_refs)
    n_chains = n_blocks * N_KV_HEADS
    held = 2
    chain_fill = [[] for _ in range(n_chains)]
    chain_fill[0] = [dots[0], dots[1]]
    n_early = len(dots) - held
    for i in range(1, n_chains):
        chain_fill[i] = ([posts[i - 1]] if i - 1 < n_early else []) + ([dots[i + 1]] if i + 1 < n_early else [])
    tail_dots, tail_posts = dots[n_early:], posts[n_early:]

    key = lax.broadcasted_iota(jnp.int32, (BLOCK, GQA_GROUP * BLOCK), 0)
    qry = lax.broadcasted_iota(jnp.int32, (BLOCK, GQA_GROUP * BLOCK), 1) % BLOCK
    from_prev = key > qry
    dist = jnp.where(from_prev, qry + BLOCK - key, qry - key).astype(F32)
    dist_first = jnp.where(jnp.logical_and(from_prev, first_of_seq), MASKED_DIST, dist)
    zeros_q = jnp.zeros((HEAD_DIM, GQA_GROUP * BLOCK), BF16)

    chains = [(b, g) for b in range(n_blocks) for g in range(N_KV_HEADS)]
    vt_cat = {(b, g): vtpad_ref[slot, g * HEAD_DIM:(g + 1) * HEAD_DIM, b * BLOCK:(b + 2) * BLOCK]
              for b, g in chains}
    qmn = qmnt_ref[slot]
    n_halves = seq_tile // V7X_MXU_DIM
    u = upad_ref[slot, V7X_SUBLANES:V7X_SUBLANES + seq_tile, :]
    u1 = upad_ref[slot, V7X_SUBLANES - 1:V7X_SUBLANES - 1 + seq_tile, :]
    u2 = upad_ref[slot, V7X_SUBLANES - 2:V7X_SUBLANES - 2 + seq_tile, :]
    cw = convw_ref[...]
    conv = cw[0:1] * u2 + cw[1:2] * u1 + cw[2:3] * u + convb_ref[...]
    conv_n = _rms(cb_ref[slot] * conv, onc_ref[...])

    scores = {}
    for b, g in chains:
        cols = slice(b * BLOCK, (b + 1) * BLOCK)
        k_cat = kpad_ref[slot, b * BLOCK:(b + 2) * BLOCK, :]
        heads = range(g * GQA_GROUP, (g + 1) * GQA_GROUP)
        q4 = jnp.concatenate(
            [qnt_ref[slot, h * HEAD_DIM:(h + 1) * HEAD_DIM, cols] for h in heads], axis=1)
        w_q = jnp.concatenate([q4, zeros_q] if g == 0 else [zeros_q, q4], axis=0)
        scores[b, g] = jnp.dot(k_cat, w_q, preferred_element_type=F32)
    kblk = kblk_ref[0]
    vblkt = vblkt_ref[0]
    mem_scores = jnp.dot(kblk, qmn, preferred_element_type=F32)
    y_conv = jnp.dot(conv_n.astype(BF16), wout_ref[ATTN_WIDTH:ATTN_WIDTH + CONV_WIDTH, :],
                     preferred_element_type=F32)
    mem_units = [(c, h) for c in range(n_halves) for h in range(N_MEM_HEADS)]
    mem_p, mem_inv_l = {}, {}

    def mem_softmax(c, h):
        s = mem_scores[h * N_MEM:(h + 1) * N_MEM, c * V7X_MXU_DIM:(c + 1) * V7X_MXU_DIM]
        m = jnp.max(s, axis=0, keepdims=True)
        p = jnp.exp(s - m)
        mem_inv_l[c, h] = 1.0 / jnp.sum(p, axis=0, keepdims=True)
        mem_p[c, h] = p.astype(BF16)

    def finish_rows(c):
        rows = slice(c * V7X_MXU_DIM, (c + 1) * V7X_MXU_DIM)
        blocks = range(c * V7X_MXU_DIM // BLOCK, (c + 1) * V7X_MXU_DIM // BLOCK)
        attn_rows = jnp.concatenate(
            [jnp.concatenate([row for g in range(N_KV_HEADS) for row in head_rows[b, g]], axis=0)
             for b in blocks], axis=1).T
        p_all = jnp.concatenate([mem_p[c, h] for h in range(N_MEM_HEADS)], axis=0)
        o_t = jnp.dot(vblkt, p_all, preferred_element_type=F32)
        mem_rows = jnp.concatenate(
            [o_t[h * HEAD_DIM:(h + 1) * HEAD_DIM] * mem_inv_l[c, h] for h in range(N_MEM_HEADS)],
            axis=0).T
        y = y_conv[rows] + jnp.dot(_rms(attn_rows, ona_ref[...]).astype(BF16), wout_ref[0:ATTN_WIDTH, :],
                                   preferred_element_type=F32)
        y = y + jnp.dot(_rms(mem_rows, onm_ref[...]).astype(BF16), wout_ref[ATTN_WIDTH + CONV_WIDTH:, :],
                        preferred_element_type=F32)
        out_ref[0, rows, :] = xres_ref[slot, rows, :] + y

    for piece in norms:
        piece()
    head_rows = {}
    chains_per_half = len(chains) // n_halves
    assert len(mem_units) == len(chains)
    for i, (b, g) in enumerate(chains):
        heads = range(g * GQA_GROUP, (g + 1) * GQA_GROUP)
        dist_b = dist_first if b == 0 else dist
        neg_slope = jnp.concatenate(
            [jnp.full((1, BLOCK), -(2.0 ** (-8.0 * (h + 1) / N_ATTN_HEADS)), F32) for h in heads], axis=1)
        sink = sinks_ref[:, g * GQA_GROUP * BLOCK:(g + 1) * GQA_GROUP * BLOCK]
        s2 = scores.pop((b, g))
        s = jnp.where(from_prev, s2[:BLOCK], s2[BLOCK:]) + dist_b * neg_slope
        m = jnp.maximum(jnp.max(s, axis=0, keepdims=True), sink)
        p = jnp.exp(s - m)
        l = jnp.sum(p, axis=0, keepdims=True) + jnp.exp(sink - m)
        p_t = jnp.concatenate(
            [jnp.where(from_prev, p, 0.0).astype(BF16), jnp.where(from_prev, 0.0, p).astype(BF16)],
            axis=0)
        for piece in chain_fill[i]:
            piece()
        o_t = jnp.dot(vt_cat[b, g], p_t, preferred_element_type=F32) * (1.0 / l)
        head_rows[b, g] = [o_t[:, hh * BLOCK:(hh + 1) * BLOCK] for hh in range(GQA_GROUP)]
        mem_softmax(*mem_units[i])
        if i + 1 == chains_per_half:
            finish_rows(0)
    for piece in tail_dots:
        piece()
    finish_rows(1)
    for piece in tail_posts:
        piece()


def _ffn_kernel(x_ref, gain_ref, wg_ref, wu_ref, wd_ref, out_ref):
    x = x_ref[...]
    h = _rms(x, gain_ref[...]).astype(BF16)
    y = x
    for c in range(wg_ref.shape[1] // V7X_MXU_DIM):
        cols = slice(c * V7X_MXU_DIM, (c + 1) * V7X_MXU_DIM)
        gate = jnp.dot(h, wg_ref[:, cols], preferred_element_type=F32)
        up = jnp.dot(h, wu_ref[:, cols], preferred_element_type=F32)
        act = (gate * jax.nn.sigmoid(gate) * up).astype(BF16)
        y = y + jnp.dot(act, wd_ref[cols, :], preferred_element_type=F32)
    out_ref[...] = y


def _const_spec(shape):
    return pl.BlockSpec(shape, lambda *_: (0,) * len(shape), pipeline_mode=pl.Buffered(1))


def _row(a):
    return a.reshape(1, -1)


def _head_column(gain, heads):
    return jnp.broadcast_to(jnp.tile(gain, heads)[:, None], (heads * HEAD_DIM, BLOCK))


def _layer(x, mem_blocks, lp, plan):
    seq_tile, ffn_tile, vmem_limit = plan
    batch, seq, _ = x.shape
    kblk, vblkt, w_in16, w_out16 = mem_blocks
    d_ff = lp["w_gate"].shape[1]

    tiles_per_seq = seq // seq_tile
    n_tiles = batch * tiles_per_seq
    slab = D_MODEL // n_tiles
    slab_spec = pl.BlockSpec((slab, d_ff), lambda t: (t, 0))
    ffn_w16 = jax.ShapeDtypeStruct((D_MODEL, d_ff), BF16)

    def tile_index(t):
        return (t // tiles_per_seq, t % tiles_per_seq, 0)

    mixer = pl.pallas_call(
        functools.partial(_mixer_kernel, seq_tile, tiles_per_seq),
        out_shape=(jax.ShapeDtypeStruct(x.shape, F32), ffn_w16, ffn_w16, ffn_w16),
        grid=(n_tiles,),
        in_specs=[
            _const_spec((1, seq_tile, D_MODEL)),
            pl.BlockSpec((1, seq_tile, D_MODEL), lambda t: tile_index(jnp.minimum(t + 1, n_tiles - 1))),
            pl.BlockSpec((1, N_MEM_HEADS * N_MEM, MEM_WIDTH), lambda t: (t // tiles_per_seq, 0, 0)),
            pl.BlockSpec((1, MEM_WIDTH, N_MEM_HEADS * N_MEM), lambda t: (t // tiles_per_seq, 0, 0)),
            _const_spec((D_MODEL, IN_PROJ_WIDTH)),
            _const_spec((D_MODEL, D_MODEL)),
            _const_spec((V7X_MXU_DIM, V7X_MXU_DIM)),
            _const_spec((1, D_MODEL)),
            _const_spec((ATTN_WIDTH, BLOCK)),
            _const_spec((1, KV_WIDTH)),
            _const_spec((MEM_WIDTH, BLOCK)),
            _const_spec((1, N_ATTN_HEADS * BLOCK)),
            _const_spec((CONV_K, CONV_WIDTH)),
            _const_spec((1, CONV_WIDTH)),
            _const_spec((1, ATTN_WIDTH)),
            _const_spec((1, CONV_WIDTH)),
            _const_spec((1, MEM_WIDTH)),
            slab_spec, slab_spec, slab_spec,
        ],
        out_specs=(pl.BlockSpec((1, seq_tile, D_MODEL), tile_index), slab_spec, slab_spec, slab_spec),
        scratch_shapes=[
            pltpu.VMEM((2, ATTN_WIDTH, seq_tile), BF16),
            pltpu.VMEM((2, MEM_WIDTH, seq_tile), BF16),
            pltpu.VMEM((2, seq_tile + BLOCK, KV_WIDTH), BF16),
            pltpu.VMEM((2, KV_WIDTH, seq_tile + BLOCK), BF16),
            pltpu.VMEM((2, seq_tile + V7X_SUBLANES, CONV_WIDTH), F32),
            pltpu.VMEM((2, seq_tile, CONV_WIDTH), F32),
            pltpu.VMEM((2, seq_tile, D_MODEL), F32),
        ],
        compiler_params=pltpu.CompilerParams(
            dimension_semantics=("arbitrary",), vmem_limit_bytes=vmem_limit),
        name="mixer",
    )
    x1, w_gate16, w_up16, w_down16 = mixer(
        x, x, kblk, vblkt, w_in16, w_out16, lp["gmat"],
        _row(lp["norm_mix"]), _head_column(lp["q_norm"], N_ATTN_HEADS),
        jnp.tile(_row(lp["k_norm"]), (1, N_KV_HEADS)), _head_column(lp["mem_q_norm"], N_MEM_HEADS),
        _row(jnp.repeat(lp["attn_sinks"], BLOCK)),
        lp["conv_w"], _row(lp["conv_b"]),
        _row(lp["out_norm_attn"]), _row(lp["out_norm_conv"]), _row(lp["out_norm_mem"]),
        lp["w_gate"], lp["w_up"], lp["w_down"].reshape(D_MODEL, d_ff))

    tokens = batch * seq
    ffn = pl.pallas_call(
        _ffn_kernel,
        out_shape=jax.ShapeDtypeStruct((tokens, D_MODEL), F32),
        grid=(tokens // ffn_tile,),
        in_specs=[
            pl.BlockSpec((ffn_tile, D_MODEL), lambda i: (i, 0)),
            _const_spec((1, D_MODEL)),
            _const_spec((D_MODEL, d_ff)),
            _const_spec((D_MODEL, d_ff)),
            _const_spec((d_ff, D_MODEL)),
        ],
        out_specs=pl.BlockSpec((ffn_tile, D_MODEL), lambda i: (i, 0)),
        compiler_params=pltpu.CompilerParams(
            dimension_semantics=("arbitrary",), vmem_limit_bytes=vmem_limit),
        name="ffn",
    )
    y = ffn(x1.reshape(tokens, D_MODEL), _row(lp["norm_ffn"]), w_gate16, w_up16,
            w_down16.reshape(d_ff, D_MODEL))
    return y.reshape(x.shape)


def _mem_kv(mem, lp, vmem_limit):
    batch = mem.shape[0]
    seqs_per_step = 4
    steps = batch // seqs_per_step
    kblk_shape = (batch, N_MEM_HEADS * N_MEM, MEM_WIDTH)
    vblkt_shape = (batch, MEM_WIDTH, N_MEM_HEADS * N_MEM)
    slab = D_MODEL // steps
    win_spec = pl.BlockSpec((slab, IN_PROJ_WIDTH), lambda b: (b, 0))
    wout_spec = pl.BlockSpec((slab, D_MODEL), lambda b: (b, 0))
    call = pl.pallas_call(
        _mem_kv_kernel,
        out_shape=(jax.ShapeDtypeStruct(kblk_shape, BF16), jax.ShapeDtypeStruct(vblkt_shape, BF16),
                   jax.ShapeDtypeStruct((D_MODEL, IN_PROJ_WIDTH), BF16),
                   jax.ShapeDtypeStruct((D_MODEL, D_MODEL), BF16)),
        grid=(steps,),
        in_specs=[
            pl.BlockSpec((seqs_per_step, N_MEM, D_MODEL), lambda b: (b, 0, 0)),
            _const_spec((1, D_MODEL)),
            _const_spec((D_MODEL, 2 * MEM_WIDTH)),
            _const_spec((1, MEM_WIDTH)),
            _const_spec((V7X_MXU_DIM, V7X_MXU_DIM)),
            win_spec, wout_spec,
        ],
        out_specs=(pl.BlockSpec((seqs_per_step,) + kblk_shape[1:], lambda b: (b, 0, 0)),
                   pl.BlockSpec((seqs_per_step,) + vblkt_shape[1:], lambda b: (b, 0, 0)),
                   win_spec, wout_spec),
        compiler_params=pltpu.CompilerParams(
            dimension_semantics=("arbitrary",), vmem_limit_bytes=vmem_limit),
        name="mem_kv",
    )
    return call(mem, _row(lp["norm_mem"]), lp["w_mem_kv"],
                jnp.tile(_row(lp["mem_k_norm"]), (1, N_MEM_HEADS)), lp["gmat"], lp["w_in"], lp["w_out"])


def kernel(x, mem, norm_mix, w_in, q_norm, k_norm, attn_sinks, conv_w, conv_b, norm_mem, w_mem_kv,
           mem_q_norm, mem_k_norm, out_norm_attn, out_norm_conv, out_norm_mem, w_out, norm_ffn,
           w_gate, w_up, w_down):
    plan = _plan()
    head_of_lane = np.arange(V7X_MXU_DIM) // HEAD_DIM
    gmat = jnp.asarray(head_of_lane[:, None] == head_of_lane[None, :], dtype=BF16)
    depth = w_in.shape[0]
    for l in range(depth):
        lp = dict(
            norm_mix=norm_mix[l], w_in=w_in[l], q_norm=q_norm[l], k_norm=k_norm[l],
            attn_sinks=attn_sinks[l], conv_w=conv_w[l], conv_b=conv_b[l], norm_mem=norm_mem[l],
            w_mem_kv=w_mem_kv[l], mem_q_norm=mem_q_norm[l], mem_k_norm=mem_k_norm[l],
            out_norm_attn=out_norm_attn[l], out_norm_conv=out_norm_conv[l],
            out_norm_mem=out_norm_mem[l], w_out=w_out[l], norm_ffn=norm_ffn[l],
            w_gate=w_gate[l], w_up=w_up[l], w_down=w_down[l],
            gmat=gmat)
        x = _layer(x, _mem_kv(mem, lp, plan[2]), lp, plan)
    return x
```

```python
import functools

import jax
import jax.numpy as jnp
import numpy as np
from jax import lax
from jax.experimental import pallas as pl
from jax.experimental.pallas import tpu as pltpu

D_MODEL = 1024
HEAD_DIM = 64
N_ATTN_HEADS = 8
N_KV_HEADS = 2
GQA_GROUP = N_ATTN_HEADS // N_KV_HEADS
BLOCK = 128
N_MEM_HEADS = 4
N_MEM = 256
CONV_K = 3
ATTN_WIDTH = N_ATTN_HEADS * HEAD_DIM
KV_WIDTH = N_KV_HEADS * HEAD_DIM
CONV_WIDTH = 256
MEM_WIDTH = N_MEM_HEADS * HEAD_DIM
IN_PROJ_WIDTH = ATTN_WIDTH + 2 * KV_WIDTH + 3 * CONV_WIDTH + MEM_WIDTH
EPS = 1e-6
MASKED_DIST = 2.0 ** 110

V7X_VMEM_BYTES = 64 * 1024 * 1024
V7X_SUBLANES = 8
V7X_LANES = 128
V7X_MXU_DIM = 256

F32 = jnp.float32
BF16 = jnp.bfloat16


def _plan():
    seq_tile = 4 * BLOCK
    ffn_tile = 1024
    vmem_limit = V7X_VMEM_BYTES - 8 * 1024 * 1024
    return seq_tile, ffn_tile, vmem_limit


def _rms(a, gain):
    return a * lax.rsqrt(jnp.mean(a * a, axis=-1, keepdims=True) + EPS) * gain


def _head_rms_scale(t, gmat):
    sq = (t * t).astype(BF16)
    ss = jnp.dot(sq, gmat, preferred_element_type=F32)
    return lax.rsqrt(ss * (1.0 / HEAD_DIM) + EPS)


def _head_rms_rows(t, gain):
    heads = t.shape[0] // HEAD_DIM
    out = []
    for h in range(heads):
        th = t[h * HEAD_DIM:(h + 1) * HEAD_DIM]
        ss = jnp.sum(th * th, axis=0, keepdims=True)
        out.append(th * lax.rsqrt(ss * (1.0 / HEAD_DIM) + EPS) * gain[h * HEAD_DIM:(h + 1) * HEAD_DIM])
    return out


def _mem_kv_kernel(mem_ref, gain_ref, w_ref, kgain_ref, gmat_ref, win_ref, wout_ref,
                   kblk_ref, vblkt_ref, win16_ref, wout16_ref):
    win16_ref[...] = win_ref[...].astype(BF16)
    wout16_ref[...] = wout_ref[...].astype(BF16)
    n_seq = mem_ref.shape[0]
    m = mem_ref[...].reshape(n_seq * N_MEM, D_MODEL)
    mn = _rms(m, gain_ref[...]).astype(BF16)
    kv = jnp.dot(mn, w_ref[...].astype(BF16), preferred_element_type=F32)
    k = kv[:, :MEM_WIDTH]
    kn = k * _head_rms_scale(k, gmat_ref[...]) * kgain_ref[...]
    lane_head = lax.broadcasted_iota(jnp.int32, (N_MEM, MEM_WIDTH), 1) // HEAD_DIM
    row_head = lax.broadcasted_iota(jnp.int32, (MEM_WIDTH, N_MEM), 0) // HEAD_DIM
    for s in range(n_seq):
        rows = slice(s * N_MEM, (s + 1) * N_MEM)
        kn_s = kn[rows]
        vt = kv[rows, MEM_WIDTH:].T
        for h in range(N_MEM_HEADS):
            kblk_ref[s, h * N_MEM:(h + 1) * N_MEM, :] = jnp.where(lane_head == h, kn_s, 0.0).astype(BF16)
            vblkt_ref[s, :, h * N_MEM:(h + 1) * N_MEM] = jnp.where(row_head == h, vt, 0.0).astype(BF16)


ROW_CHUNK = V7X_MXU_DIM
COL_GROUP = 2 * V7X_MXU_DIM


def _stage1_pieces(x_ref, wslot, carry, refs):
    (win_ref, gmat_ref, nmix_ref, qgain_ref, kgain_ref, mqgain_ref,
     qnt_ref, qmnt_ref, kpad_ref, vtpad_ref, upad_ref, cb_ref, xres_ref) = refs
    seq_tile = cb_ref.shape[1]
    xn, val = {}, {}
    conv_base = ATTN_WIDTH + 2 * KV_WIDTH

    def carried(ref_slice_fn, shape, dtype):
        if carry is None:
            return jnp.zeros(shape, dtype)
        prev_slot, first_of_seq = carry
        return jnp.where(first_of_seq, jnp.zeros(shape, dtype), ref_slice_fn(prev_slot))

    def heads_t(t, gain_ref, out_ref, r):
        t = t.T
        gain = gain_ref[...] * HEAD_DIM ** -0.5
        gain = jnp.concatenate([gain] * (ROW_CHUNK // BLOCK), axis=1)
        out_ref[wslot, :, r * ROW_CHUNK:(r + 1) * ROW_CHUNK] = jnp.concatenate(
            _head_rms_rows(t, gain), axis=0).astype(BF16)

    norms, dots, posts, narrow_dots, narrow_posts = [], [], [], [], []
    for r in range(seq_tile // ROW_CHUNK):
        rows = slice(r * ROW_CHUNK, (r + 1) * ROW_CHUNK)

        def norm(r=r, rows=rows):
            x = x_ref[0, rows, :]
            xres_ref[wslot, rows, :] = x
            xn[r] = _rms(x, nmix_ref[...]).astype(BF16)

        def dot_piece(name, start, width, r=r):
            def run():
                val[name, r] = jnp.dot(xn[r], win_ref[:, start:start + width], preferred_element_type=F32)
            return run

        def post_q(r=r):
            heads_t(val.pop(("q", r)), qgain_ref, qnt_ref, r)

        def post_kv(r=r):
            t = val.pop(("kvch", r))
            kv_a = t[:, :2 * KV_WIDTH]
            val["ch", r] = t[:, 2 * KV_WIDTH:]
            scale = _head_rms_scale(kv_a, gmat_ref[...])
            kn = kv_a[:, :KV_WIDTH] * scale[:, :KV_WIDTH] * kgain_ref[...]
            kpad_ref[wslot, BLOCK + r * ROW_CHUNK:BLOCK + (r + 1) * ROW_CHUNK, :] = kn.astype(BF16)
            vtpad_ref[wslot, :, BLOCK + r * ROW_CHUNK:BLOCK + (r + 1) * ROW_CHUNK] = (
                kv_a[:, KV_WIDTH:].T.astype(BF16))
            if r == 0:
                kpad_ref[wslot, 0:BLOCK, :] = carried(
                    lambda s: kpad_ref[s, seq_tile:seq_tile + BLOCK, :], (BLOCK, KV_WIDTH), BF16)
                vtpad_ref[wslot, :, 0:BLOCK] = carried(
                    lambda s: vtpad_ref[s, :, seq_tile:seq_tile + BLOCK], (KV_WIDTH, BLOCK), BF16)

        def post_conv(r=r):
            t = val.pop(("cbcc", r))
            cb_ref[wslot, r * ROW_CHUNK:(r + 1) * ROW_CHUNK, :] = t[:, :CONV_WIDTH]
            u = t[:, CONV_WIDTH:] * val.pop(("ch", r))
            upad_ref[wslot, V7X_SUBLANES + r * ROW_CHUNK:V7X_SUBLANES + (r + 1) * ROW_CHUNK, :] = u
            if r == 0:
                upad_ref[wslot, 0:V7X_SUBLANES, :] = carried(
                    lambda s: upad_ref[s, seq_tile:seq_tile + V7X_SUBLANES, :],
                    (V7X_SUBLANES, CONV_WIDTH), F32)

        def post_qm(r=r):
            heads_t(val.pop(("qm", r)), mqgain_ref, qmnt_ref, r)

        norms.append(norm)
        dots += [dot_piece("q", 0, COL_GROUP), dot_piece("kvch", ATTN_WIDTH, COL_GROUP),
                 dot_piece("cbcc", conv_base + CONV_WIDTH, COL_GROUP)]
        posts += [post_q, post_kv, post_conv]
        narrow_dots.append(dot_piece("qm", IN_PROJ_WIDTH - MEM_WIDTH, MEM_WIDTH))
        narrow_posts.append(post_qm)
    return norms, dots + narrow_dots, posts + narrow_posts


def _mixer_kernel(seq_tile, tiles_per_seq,
                  xfirst_ref, xnext_ref, kblk_ref, vblkt_ref, win_ref, wout_ref, gmat_ref,
                  nmix_ref, qgain_ref, kgain_ref, mqgain_ref, sinks_ref, convw_ref, convb_ref,
                  ona_ref, onc_ref, onm_ref, wg_ref, wu_ref, wd_ref,
                  out_ref, wg16_ref, wu16_ref, wd16_ref,
                  qnt_ref, qmnt_ref, kpad_ref, vtpad_ref, upad_ref, cb_ref, xres_ref):
    wg16_ref[...] = wg_ref[...].astype(BF16)
    wu16_ref[...] = wu_ref[...].astype(BF16)
    wd16_ref[...] = wd_ref[...].astype(BF16)
    t = pl.program_id(0)
    slot = t % 2
    n_blocks = seq_tile // BLOCK
    first_of_seq = (t % tiles_per_seq) == 0
    stage1_refs = (win_ref, gmat_ref, nmix_ref, qgain_ref, kgain_ref, mqgain_ref,
                   qnt_ref, qmnt_ref, kpad_ref, vtpad_ref, upad_ref, cb_ref, xres_ref)

    @pl.when(t == 0)
    def _():
        norms, dots, posts = _stage1_pieces(xfirst_ref, 0, None, stage1_refs)
        for piece in norms + [p for pair in zip(dots, posts) for p in pair]:
            piece()

    norms, dots, posts = _stage1_pieces(
        xnext_ref, 1 - slot, (slot, ((t + 1) % tiles_per_seq) == 0), stage1_refs)
    n_chains = n_blocks * N_KV_HEADS
    held = 2
    chain_fill = [[] for _ in range(n_chains)]
    chain_fill[0] = [dots[0], dots[1]]
    n_early = len(dots) - held
    for i in range(1, n_chains):
        chain_fill[i] = ([posts[i - 1]] if i - 1 < n_early else []) + ([dots[i + 1]] if i + 1 < n_early else [])
    tail_dots, tail_posts = dots[n_early:], posts[n_early:]

    key = lax.broadcasted_iota(jnp.int32, (BLOCK, GQA_GROUP * BLOCK), 0)
    qry = lax.broadcasted_iota(jnp.int32, (BLOCK, GQA_GROUP * BLOCK), 1) % BLOCK
    from_prev = key > qry
    dist = jnp.where(from_prev, qry + BLOCK - key, qry - key).astype(F32)
    dist_first = jnp.where(jnp.logical_and(from_prev, first_of_seq), MASKED_DIST, dist)
    zeros_q = jnp.zeros((HEAD_DIM, GQA_GROUP * BLOCK), BF16)

    chains = [(b, g) for b in range(n_blocks) for g in range(N_KV_HEADS)]
    vt_cat = {(b, g): vtpad_ref[slot, g * HEAD_DIM:(g + 1) * HEAD_DIM, b * BLOCK:(b + 2) * BLOCK]
              for b, g in chains}
    qmn = qmnt_ref[slot]
    n_halves = seq_tile // V7X_MXU_DIM
    u = upad_ref[slot, V7X_SUBLANES:V7X_SUBLANES + seq_tile, :]
    u1 = upad_ref[slot, V7X_SUBLANES - 1:V7X_SUBLANES - 1 + seq_tile, :]
    u2 = upad_ref[slot, V7X_SUBLANES - 2:V7X_SUBLANES - 2 + seq_tile, :]
    cw = convw_ref[...]
    conv = cw[0:1] * u2 + cw[1:2] * u1 + cw[2:3] * u + convb_ref[...]
    conv_n = _rms(cb_ref[slot] * conv, onc_ref[...])

    scores = {}
    for b, g in chains:
        cols = slice(b * BLOCK, (b + 1) * BLOCK)
        k_cat = kpad_ref[slot, b * BLOCK:(b + 2) * BLOCK, :]
        heads = range(g * GQA_GROUP, (g + 1) * GQA_GROUP)
        q4 = jnp.concatenate(
            [qnt_ref[slot, h * HEAD_DIM:(h + 1) * HEAD_DIM, cols] for h in heads], axis=1)
        w_q = jnp.concatenate([q4, zeros_q] if g == 0 else [zeros_q, q4], axis=0)
        scores[b, g] = jnp.dot(k_cat, w_q, preferred_element_type=F32)
    kblk = kblk_ref[0]
    vblkt = vblkt_ref[0]
    mem_scores = jnp.dot(kblk, qmn, preferred_element_type=F32)
    y_conv = jnp.dot(conv_n.astype(BF16), wout_ref[ATTN_WIDTH:ATTN_WIDTH + CONV_WIDTH, :],
                     preferred_element_type=F32)
    mem_units = [(c, h) for c in range(n_halves) for h in range(N_MEM_HEADS)]
    mem_p, mem_inv_l = {}, {}

    def mem_softmax(c, h):
        s = mem_scores[h * N_MEM:(h + 1) * N_MEM, c * V7X_MXU_DIM:(c + 1) * V7X_MXU_DIM]
        m = jnp.max(s, axis=0, keepdims=True)
        p = jnp.exp(s - m)
        mem_inv_l[c, h] = 1.0 / jnp.sum(p, axis=0, keepdims=True)
        mem_p[c, h] = p.astype(BF16)

    def finish_rows(c):
        rows = slice(c * V7X_MXU_DIM, (c + 1) * V7X_MXU_DIM)
        blocks = range(c * V7X_MXU_DIM // BLOCK, (c + 1) * V7X_MXU_DIM // BLOCK)
        attn_rows = jnp.concatenate(
            [jnp.concatenate([row for g in range(N_KV_HEADS) for row in head_rows[b, g]], axis=0)
             for b in blocks], axis=1).T
        p_all = jnp.concatenate([mem_p[c, h] for h in range(N_MEM_HEADS)], axis=0)
        o_t = jnp.dot(vblkt, p_all, preferred_element_type=F32)
        mem_rows = jnp.concatenate(
            [o_t[h * HEAD_DIM:(h + 1) * HEAD_DIM] * mem_inv_l[c, h] for h in range(N_MEM_HEADS)],
            axis=0).T
        y = y_conv[rows] + jnp.dot(_rms(attn_rows, ona_ref[...]).astype(BF16), wout_ref[0:ATTN_WIDTH, :],
                                   preferred_element_type=F32)
        y = y + jnp.dot(_rms(mem_rows, onm_ref[...]).astype(BF16), wout_ref[ATTN_WIDTH + CONV_WIDTH:, :],
                        preferred_element_type=F32)
        out_ref[0, rows, :] = xres_ref[slot, rows, :] + y

    for piece in norms:
        piece()
    head_rows = {}
    chains_per_half = len(chains) // n_halves
    assert len(mem_units) == len(chains)
    for i, (b, g) in enumerate(chains):
        heads = range(g * GQA_GROUP, (g + 1) * GQA_GROUP)
        dist_b = dist_first if b == 0 else dist
        neg_slope = jnp.concatenate(
            [jnp.full((1, BLOCK), -(2.0 ** (-8.0 * (h + 1) / N_ATTN_HEADS)), F32) for h in heads], axis=1)
        sink = sinks_ref[:, g * GQA_GROUP * BLOCK:(g + 1) * GQA_GROUP * BLOCK]
        s2 = scores.pop((b, g))
        s = jnp.where(from_prev, s2[:BLOCK], s2[BLOCK:]) + dist_b * neg_slope
        m = jnp.maximum(jnp.max(s, axis=0, keepdims=True), sink)
        p = jnp.exp(s - m)
        l = jnp.sum(p, axis=0, keepdims=True) + jnp.exp(sink - m)
        p_t = jnp.concatenate(
            [jnp.where(from_prev, p, 0.0).astype(BF16), jnp.where(from_prev, 0.0, p).astype(BF16)],
            axis=0)
        for piece in chain_fill[i]:
            piece()
        o_t = jnp.dot(vt_cat[b, g], p_t, preferred_element_type=F32) * (1.0 / l)
        head_rows[b, g] = [o_t[:, hh * BLOCK:(hh + 1) * BLOCK] for hh in range(GQA_GROUP)]
        mem_softmax(*mem_units[i])
        if i + 1 == chains_per_half:
            finish_rows(0)
    for piece in tail_dots:
        piece()
    finish_rows(1)
    for piece in tail_posts:
        piece()


def _ffn_kernel(x_ref, gain_ref, wg_ref, wu_ref, wd_ref, out_ref):
    x = x_ref[...]
    h = _rms(x, gain_ref[...]).astype(BF16)
    y = x
    for c in range(wg_ref.shape[1] // V7X_MXU_DIM):
        cols = slice(c * V7X_MXU_DIM, (c + 1) * V7X_MXU_DIM)
        gate = jnp.dot(h, wg_ref[:, cols], preferred_element_type=F32)
        up = jnp.dot(h, wu_ref[:, cols], preferred_element_type=F32)
        act = (gate * jax.nn.sigmoid(gate) * up).astype(BF16)
        y = y + jnp.dot(act, wd_ref[cols, :], preferred_element_type=F32)
    out_ref[...] = y


def _const_spec(shape):
    return pl.BlockSpec(shape, lambda *_: (0,) * len(shape), pipeline_mode=pl.Buffered(1))


def _row(a):
    return a.reshape(1, -1)


def _head_column(gain, heads):
    return jnp.broadcast_to(jnp.tile(gain, heads)[:, None], (heads * HEAD_DIM, BLOCK))


def _layer(x, mem_blocks, lp, plan):
    seq_tile, ffn_tile, vmem_limit = plan
    batch, seq, _ = x.shape
    kblk, vblkt, w_in16, w_out16 = mem_blocks
    d_ff = lp["w_gate"].shape[1]

    tiles_per_seq = seq // seq_tile
    n_tiles = batch * tiles_per_seq
    slab_spec = pl.BlockSpec((D_MODEL // n_tiles, d_ff), lambda t: (t, 0))
    down_slabs = d_ff // BLOCK
    assert down_slabs <= n_tiles
    down_spec = pl.BlockSpec((BLOCK, D_MODEL), lambda t: (jnp.minimum(t, down_slabs - 1), 0))
    up_w16 = jax.ShapeDtypeStruct((D_MODEL, d_ff), BF16)
    down_w16 = jax.ShapeDtypeStruct((d_ff, D_MODEL), BF16)

    def tile_index(t):
        return (t // tiles_per_seq, t % tiles_per_seq, 0)

    mixer = pl.pallas_call(
        functools.partial(_mixer_kernel, seq_tile, tiles_per_seq),
        out_shape=(jax.ShapeDtypeStruct(x.shape, F32), up_w16, up_w16, down_w16),
        grid=(n_tiles,),
        in_specs=[
            _const_spec((1, seq_tile, D_MODEL)),
            pl.BlockSpec((1, seq_tile, D_MODEL), lambda t: tile_index(jnp.minimum(t + 1, n_tiles - 1))),
            pl.BlockSpec((1, N_MEM_HEADS * N_MEM, MEM_WIDTH), lambda t: (t // tiles_per_seq, 0, 0)),
            pl.BlockSpec((1, MEM_WIDTH, N_MEM_HEADS * N_MEM), lambda t: (t // tiles_per_seq, 0, 0)),
            _const_spec((D_MODEL, IN_PROJ_WIDTH)),
            _const_spec((D_MODEL, D_MODEL)),
            _const_spec((V7X_MXU_DIM, V7X_MXU_DIM)),
            _const_spec((1, D_MODEL)),
            _const_spec((ATTN_WIDTH, BLOCK)),
            _const_spec((1, KV_WIDTH)),
            _const_spec((MEM_WIDTH, BLOCK)),
            _const_spec((1, N_ATTN_HEADS * BLOCK)),
            _const_spec((CONV_K, CONV_WIDTH)),
            _const_spec((1, CONV_WIDTH)),
            _const_spec((1, ATTN_WIDTH)),
            _const_spec((1, CONV_WIDTH)),
            _const_spec((1, MEM_WIDTH)),
            slab_spec, slab_spec, down_spec,
        ],
        out_specs=(pl.BlockSpec((1, seq_tile, D_MODEL), tile_index), slab_spec, slab_spec, down_spec),
        scratch_shapes=[
            pltpu.VMEM((2, ATTN_WIDTH, seq_tile), BF16),
            pltpu.VMEM((2, MEM_WIDTH, seq_tile), BF16),
            pltpu.VMEM((2, seq_tile + BLOCK, KV_WIDTH), BF16),
            pltpu.VMEM((2, KV_WIDTH, seq_tile + BLOCK), BF16),
            pltpu.VMEM((2, seq_tile + V7X_SUBLANES, CONV_WIDTH), F32),
            pltpu.VMEM((2, seq_tile, CONV_WIDTH), F32),
            pltpu.VMEM((2, seq_tile, D_MODEL), F32),
        ],
        compiler_params=pltpu.CompilerParams(
            dimension_semantics=("arbitrary",), vmem_limit_bytes=vmem_limit),
        name="mixer",
    )
    x1, w_gate16, w_up16, w_down16 = mixer(
        x, x, kblk, vblkt, w_in16, w_out16, lp["gmat"],
        _row(lp["norm_mix"]), _head_column(lp["q_norm"], N_ATTN_HEADS),
        jnp.tile(_row(lp["k_norm"]), (1, N_KV_HEADS)), _head_column(lp["mem_q_norm"], N_MEM_HEADS),
        _row(jnp.repeat(lp["attn_sinks"], BLOCK)),
        lp["conv_w"], _row(lp["conv_b"]),
        _row(lp["out_norm_attn"]), _row(lp["out_norm_conv"]), _row(lp["out_norm_mem"]),
        lp["w_gate"], lp["w_up"], lp["w_down"])

    tokens = batch * seq
    ffn = pl.pallas_call(
        _ffn_kernel,
        out_shape=jax.ShapeDtypeStruct((tokens, D_MODEL), F32),
        grid=(tokens // ffn_tile,),
        in_specs=[
            pl.BlockSpec((ffn_tile, D_MODEL), lambda i: (i, 0)),
            _const_spec((1, D_MODEL)),
            _const_spec((D_MODEL, d_ff)),
            _const_spec((D_MODEL, d_ff)),
            _const_spec((d_ff, D_MODEL)),
        ],
        out_specs=pl.BlockSpec((ffn_tile, D_MODEL), lambda i: (i, 0)),
        compiler_params=pltpu.CompilerParams(
            dimension_semantics=("arbitrary",), vmem_limit_bytes=vmem_limit),
        name="ffn",
    )
    y = ffn(x1.reshape(tokens, D_MODEL), _row(lp["norm_ffn"]), w_gate16, w_up16, w_down16)
    return y.reshape(x.shape)


def _mem_kv(mem, lp, vmem_limit):
    batch = mem.shape[0]
    seqs_per_step = 4
    steps = batch // seqs_per_step
    kblk_shape = (batch, N_MEM_HEADS * N_MEM, MEM_WIDTH)
    vblkt_shape = (batch, MEM_WIDTH, N_MEM_HEADS * N_MEM)
    slab = D_MODEL // steps
    win_spec = pl.BlockSpec((slab, IN_PROJ_WIDTH), lambda b: (b, 0))
    wout_spec = pl.BlockSpec((slab, D_MODEL), lambda b: (b, 0))
    call = pl.pallas_call(
        _mem_kv_kernel,
        out_shape=(jax.ShapeDtypeStruct(kblk_shape, BF16), jax.ShapeDtypeStruct(vblkt_shape, BF16),
                   jax.ShapeDtypeStruct((D_MODEL, IN_PROJ_WIDTH), BF16),
                   jax.ShapeDtypeStruct((D_MODEL, D_MODEL), BF16)),
        grid=(steps,),
        in_specs=[
            pl.BlockSpec((seqs_per_step, N_MEM, D_MODEL), lambda b: (b, 0, 0)),
            _const_spec((1, D_MODEL)),
            _const_spec((D_MODEL, 2 * MEM_WIDTH)),
            _const_spec((1, MEM_WIDTH)),
            _const_spec((V7X_MXU_DIM, V7X_MXU_DIM)),
            win_spec, wout_spec,
        ],
        out_specs=(pl.BlockSpec((seqs_per_step,) + kblk_shape[1:], lambda b: (b, 0, 0)),
                   pl.BlockSpec((seqs_per_step,) + vblkt_shape[1:], lambda b: (b, 0, 0)),
                   win_spec, wout_spec),
        compiler_params=pltpu.CompilerParams(
            dimension_semantics=("arbitrary",), vmem_limit_bytes=vmem_limit),
        name="mem_kv",
    )
    return call(mem, _row(lp["norm_mem"]), lp["w_mem_kv"],
                jnp.tile(_row(lp["mem_k_norm"]), (1, N_MEM_HEADS)), lp["gmat"], lp["w_in"], lp["w_out"])


def kernel(x, mem, norm_mix, w_in, q_norm, k_norm, attn_sinks, conv_w, conv_b, norm_mem, w_mem_kv,
           mem_q_norm, mem_k_norm, out_norm_attn, out_norm_conv, out_norm_mem, w_out, norm_ffn,
           w_gate, w_up, w_down):
    plan = _plan()
    head_of_lane = np.arange(V7X_MXU_DIM) // HEAD_DIM
    gmat = jnp.asarray(head_of_lane[:, None] == head_of_lane[None, :], dtype=BF16)
    depth = w_in.shape[0]
    for l in range(depth):
        lp = dict(
            norm_mix=norm_mix[l], w_in=w_in[l], q_norm=q_norm[l], k_norm=k_norm[l],
            attn_sinks=attn_sinks[l], conv_w=conv_w[l], conv_b=conv_b[l], norm_mem=norm_mem[l],
            w_mem_kv=w_mem_kv[l], mem_q_norm=mem_q_norm[l], mem_k_norm=mem_k_norm[l],
            out_norm_attn=out_norm_attn[l], out_norm_conv=out_norm_conv[l],
            out_norm_mem=out_norm_mem[l], w_out=w_out[l], norm_ffn=norm_ffn[l],
            w_gate=w_gate[l], w_up=w_up[l], w_down=w_down[l],
            gmat=gmat)
        x = _layer(x, _mem_kv(mem, lp, plan[2]), lp, plan)
    return x
```

```python
import functools

import jax
import jax.numpy as jnp
import numpy as np
from jax import lax
from jax.experimental import pallas as pl
from jax.experimental.pallas import tpu as pltpu

D_MODEL = 1024
HEAD_DIM = 64
N_ATTN_HEADS = 8
N_KV_HEADS = 2
GQA_GROUP = N_ATTN_HEADS // N_KV_HEADS
BLOCK = 128
N_MEM_HEADS = 4
N_MEM = 256
CONV_K = 3
ATTN_WIDTH = N_ATTN_HEADS * HEAD_DIM
KV_WIDTH = N_KV_HEADS * HEAD_DIM
CONV_WIDTH = 256
MEM_WIDTH = N_MEM_HEADS * HEAD_DIM
IN_PROJ_WIDTH = ATTN_WIDTH + 2 * KV_WIDTH + 3 * CONV_WIDTH + MEM_WIDTH
EPS = 1e-6
MASKED_DIST = 2.0 ** 110

V7X_VMEM_BYTES = 64 * 1024 * 1024
V7X_SUBLANES = 8
V7X_LANES = 128
V7X_MXU_DIM = 256

F32 = jnp.float32
BF16 = jnp.bfloat16


def _plan():
    seq_tile = 4 * BLOCK
    ffn_tile = 1024
    vmem_limit = V7X_VMEM_BYTES - 8 * 1024 * 1024
    return seq_tile, ffn_tile, vmem_limit


def _rms(a, gain):
    return a * lax.rsqrt(jnp.mean(a * a, axis=-1, keepdims=True) + EPS) * gain


def _head_rms_scale(t, gmat):
    sq = (t * t).astype(BF16)
    ss = jnp.dot(sq, gmat, preferred_element_type=F32)
    return lax.rsqrt(ss * (1.0 / HEAD_DIM) + EPS)


def _head_rms_rows(t, gain):
    heads = t.shape[0] // HEAD_DIM
    out = []
    for h in range(heads):
        th = t[h * HEAD_DIM:(h + 1) * HEAD_DIM]
        ss = jnp.sum(th * th, axis=0, keepdims=True)
        out.append(th * lax.rsqrt(ss * (1.0 / HEAD_DIM) + EPS) * gain[h * HEAD_DIM:(h + 1) * HEAD_DIM])
    return out


def _mem_kv_kernel(mem_ref, gain_ref, w_ref, kgain_ref, gmat_ref, win_ref, wout_ref,
                   kblk_ref, vblkt_ref, win16_ref, wout16_ref):
    win16_ref[...] = win_ref[...].astype(BF16)
    wout16_ref[...] = wout_ref[...].astype(BF16)
    n_seq = mem_ref.shape[0]
    m = mem_ref[...].reshape(n_seq * N_MEM, D_MODEL)
    mn = _rms(m, gain_ref[...]).astype(BF16)
    kv = jnp.dot(mn, w_ref[...].astype(BF16), preferred_element_type=F32)
    k = kv[:, :MEM_WIDTH]
    kn = k * _head_rms_scale(k, gmat_ref[...]) * kgain_ref[...]
    lane_head = lax.broadcasted_iota(jnp.int32, (N_MEM, MEM_WIDTH), 1) // HEAD_DIM
    row_head = lax.broadcasted_iota(jnp.int32, (MEM_WIDTH, N_MEM), 0) // HEAD_DIM
    for s in range(n_seq):
        rows = slice(s * N_MEM, (s + 1) * N_MEM)
        kn_s = kn[rows]
        vt = kv[rows, MEM_WIDTH:].T
        for h in range(N_MEM_HEADS):
            kblk_ref[s, h * N_MEM:(h + 1) * N_MEM, :] = jnp.where(lane_head == h, kn_s, 0.0).astype(BF16)
            vblkt_ref[s, :, h * N_MEM:(h + 1) * N_MEM] = jnp.where(row_head == h, vt, 0.0).astype(BF16)


ROW_CHUNK = V7X_MXU_DIM
COL_GROUP = 2 * V7X_MXU_DIM
CONV_PROJ_CHAIN = 3


def _stage1_pieces(x_ref, wslot, carry, refs):
    (win_ref, gmat_ref, nmix_ref, qgain_ref, kgain_ref, mqgain_ref,
     qnt_ref, qmnt_ref, kpad_ref, vtpad_ref, upad_ref, cb_ref, xres_ref) = refs
    seq_tile = cb_ref.shape[1]
    xn, val = {}, {}
    conv_base = ATTN_WIDTH + 2 * KV_WIDTH

    def carried(ref_slice_fn, shape, dtype):
        if carry is None:
            return jnp.zeros(shape, dtype)
        prev_slot, first_of_seq = carry
        return jnp.where(first_of_seq, jnp.zeros(shape, dtype), ref_slice_fn(prev_slot))

    def heads_t(t, gain_ref, out_ref, r):
        t = t.T
        gain = gain_ref[...] * HEAD_DIM ** -0.5
        gain = jnp.concatenate([gain] * (ROW_CHUNK // BLOCK), axis=1)
        out_ref[wslot, :, r * ROW_CHUNK:(r + 1) * ROW_CHUNK] = jnp.concatenate(
            _head_rms_rows(t, gain), axis=0).astype(BF16)

    norms, dots, posts, narrow_dots, narrow_posts = [], [], [], [], []
    for r in range(seq_tile // ROW_CHUNK):
        rows = slice(r * ROW_CHUNK, (r + 1) * ROW_CHUNK)

        def norm(r=r, rows=rows):
            x = x_ref[0, rows, :]
            xres_ref[wslot, rows, :] = x
            xn[r] = _rms(x, nmix_ref[...]).astype(BF16)

        def dot_piece(name, start, width, r=r):
            def run():
                val[name, r] = jnp.dot(xn[r], win_ref[:, start:start + width], preferred_element_type=F32)
            return run

        def post_q(r=r):
            heads_t(val.pop(("q", r)), qgain_ref, qnt_ref, r)

        def post_kv(r=r):
            t = val.pop(("kvch", r))
            kv_a = t[:, :2 * KV_WIDTH]
            val["ch", r] = t[:, 2 * KV_WIDTH:]
            scale = _head_rms_scale(kv_a, gmat_ref[...])
            kn = kv_a[:, :KV_WIDTH] * scale[:, :KV_WIDTH] * kgain_ref[...]
            kpad_ref[wslot, BLOCK + r * ROW_CHUNK:BLOCK + (r + 1) * ROW_CHUNK, :] = kn.astype(BF16)
            vtpad_ref[wslot, :, BLOCK + r * ROW_CHUNK:BLOCK + (r + 1) * ROW_CHUNK] = (
                kv_a[:, KV_WIDTH:].T.astype(BF16))
            if r == 0:
                kpad_ref[wslot, 0:BLOCK, :] = carried(
                    lambda s: kpad_ref[s, seq_tile:seq_tile + BLOCK, :], (BLOCK, KV_WIDTH), BF16)
                vtpad_ref[wslot, :, 0:BLOCK] = carried(
                    lambda s: vtpad_ref[s, :, seq_tile:seq_tile + BLOCK], (KV_WIDTH, BLOCK), BF16)

        def post_conv(r=r):
            t = val.pop(("cbcc", r))
            cb_ref[wslot, r * ROW_CHUNK:(r + 1) * ROW_CHUNK, :] = t[:, :CONV_WIDTH]
            u = t[:, CONV_WIDTH:] * val.pop(("ch", r))
            upad_ref[wslot, V7X_SUBLANES + r * ROW_CHUNK:V7X_SUBLANES + (r + 1) * ROW_CHUNK, :] = u
            if r == 0:
                upad_ref[wslot, 0:V7X_SUBLANES, :] = carried(
                    lambda s: upad_ref[s, seq_tile:seq_tile + V7X_SUBLANES, :],
                    (V7X_SUBLANES, CONV_WIDTH), F32)

        def post_qm(r=r):
            heads_t(val.pop(("qm", r)), mqgain_ref, qmnt_ref, r)

        norms.append(norm)
        dots += [dot_piece("q", 0, COL_GROUP), dot_piece("kvch", ATTN_WIDTH, COL_GROUP),
                 dot_piece("cbcc", conv_base + CONV_WIDTH, COL_GROUP)]
        posts += [post_q, post_kv, post_conv]
        narrow_dots.append(dot_piece("qm", IN_PROJ_WIDTH - MEM_WIDTH, MEM_WIDTH))
        narrow_posts.append(post_qm)
    return norms, dots + narrow_dots, posts + narrow_posts


def _mixer_kernel(seq_tile, tiles_per_seq,
                  xfirst_ref, xnext_ref, kblk_ref, vblkt_ref, win_ref, wout_ref, gmat_ref,
                  nmix_ref, qgain_ref, kgain_ref, mqgain_ref, sinks_ref, convw_ref, convb_ref,
                  ona_ref, onc_ref, onm_ref, wg_ref, wu_ref, wd_ref,
                  out_ref, wg16_ref, wu16_ref, wd16_ref,
                  qnt_ref, qmnt_ref, kpad_ref, vtpad_ref, upad_ref, cb_ref, xres_ref):
    wg16_ref[...] = wg_ref[...].astype(BF16)
    wu16_ref[...] = wu_ref[...].astype(BF16)
    wd16_ref[...] = wd_ref[...].astype(BF16)
    t = pl.program_id(0)
    slot = t % 2
    n_blocks = seq_tile // BLOCK
    first_of_seq = (t % tiles_per_seq) == 0
    stage1_refs = (win_ref, gmat_ref, nmix_ref, qgain_ref, kgain_ref, mqgain_ref,
                   qnt_ref, qmnt_ref, kpad_ref, vtpad_ref, upad_ref, cb_ref, xres_ref)

    @pl.when(t == 0)
    def _():
        norms, dots, posts = _stage1_pieces(xfirst_ref, 0, None, stage1_refs)
        for piece in norms + [p for pair in zip(dots, posts) for p in pair]:
            piece()

    norms, dots, posts = _stage1_pieces(
        xnext_ref, 1 - slot, (slot, ((t + 1) % tiles_per_seq) == 0), stage1_refs)
    n_chains = n_blocks * N_KV_HEADS
    held = 2
    chain_fill = [[] for _ in range(n_chains)]
    chain_fill[0] = [dots[0], dots[1]]
    n_early = len(dots) - held
    for i in range(1, n_chains):
        chain_fill[i] = ([posts[i - 1]] if i - 1 < n_early else []) + ([dots[i + 1]] if i + 1 < n_early else [])
    tail_dots, tail_posts = dots[n_early:], posts[n_early:]

    key = lax.broadcasted_iota(jnp.int32, (BLOCK, GQA_GROUP * BLOCK), 0)
    qry = lax.broadcasted_iota(jnp.int32, (BLOCK, GQA_GROUP * BLOCK), 1) % BLOCK
    from_prev = key > qry
    dist = jnp.where(from_prev, qry + BLOCK - key, qry - key).astype(F32)
    dist_first = jnp.where(jnp.logical_and(from_prev, first_of_seq), MASKED_DIST, dist)
    zeros_q = jnp.zeros((HEAD_DIM, GQA_GROUP * BLOCK), BF16)

    chains = [(b, g) for b in range(n_blocks) for g in range(N_KV_HEADS)]
    vt_cat = {(b, g): vtpad_ref[slot, g * HEAD_DIM:(g + 1) * HEAD_DIM, b * BLOCK:(b + 2) * BLOCK]
              for b, g in chains}
    qmn = qmnt_ref[slot]
    n_halves = seq_tile // V7X_MXU_DIM
    u = upad_ref[slot, V7X_SUBLANES:V7X_SUBLANES + seq_tile, :]
    u1 = upad_ref[slot, V7X_SUBLANES - 1:V7X_SUBLANES - 1 + seq_tile, :]
    u2 = upad_ref[slot, V7X_SUBLANES - 2:V7X_SUBLANES - 2 + seq_tile, :]
    cw = convw_ref[...]
    conv = cw[0:1] * u2 + cw[1:2] * u1 + cw[2:3] * u + convb_ref[...]
    conv_n = _rms(cb_ref[slot] * conv, onc_ref[...])

    scores = {}

    def score_chain(b, g):
        cols = slice(b * BLOCK, (b + 1) * BLOCK)
        k_cat = kpad_ref[slot, b * BLOCK:(b + 2) * BLOCK, :]
        heads = range(g * GQA_GROUP, (g + 1) * GQA_GROUP)
        q4 = jnp.concatenate(
            [qnt_ref[slot, h * HEAD_DIM:(h + 1) * HEAD_DIM, cols] for h in heads], axis=1)
        w_q = jnp.concatenate([q4, zeros_q] if g == 0 else [zeros_q, q4], axis=0)
        scores[b, g] = jnp.dot(k_cat, w_q, preferred_element_type=F32)

    for b, g in chains:
        score_chain(b, g)
    late = {}

    def conv_proj():
        late["y_conv"] = jnp.dot(conv_n.astype(BF16), wout_ref[ATTN_WIDTH:ATTN_WIDTH + CONV_WIDTH, :],
                                 preferred_element_type=F32)

    chain_fill[CONV_PROJ_CHAIN].insert(0, conv_proj)
    kblk = kblk_ref[0]
    vblkt = vblkt_ref[0]
    mem_scores = jnp.dot(kblk, qmn, preferred_element_type=F32)
    mem_units = [(c, h) for c in range(n_halves) for h in range(N_MEM_HEADS)]
    mem_p, mem_inv_l = {}, {}

    def mem_softmax(c, h):
        s = mem_scores[h * N_MEM:(h + 1) * N_MEM, c * V7X_MXU_DIM:(c + 1) * V7X_MXU_DIM]
        m = jnp.max(s, axis=0, keepdims=True)
        p = jnp.exp(s - m)
        mem_inv_l[c, h] = 1.0 / jnp.sum(p, axis=0, keepdims=True)
        mem_p[c, h] = p.astype(BF16)

    def finish_rows(c):
        rows = slice(c * V7X_MXU_DIM, (c + 1) * V7X_MXU_DIM)
        blocks = range(c * V7X_MXU_DIM // BLOCK, (c + 1) * V7X_MXU_DIM // BLOCK)
        attn_rows = jnp.concatenate(
            [jnp.concatenate([row for g in range(N_KV_HEADS) for row in head_rows[b, g]], axis=0)
             for b in blocks], axis=1).T
        p_all = jnp.concatenate([mem_p[c, h] for h in range(N_MEM_HEADS)], axis=0)
        o_t = jnp.dot(vblkt, p_all, preferred_element_type=F32)
        mem_rows = jnp.concatenate(
            [o_t[h * HEAD_DIM:(h + 1) * HEAD_DIM] * mem_inv_l[c, h] for h in range(N_MEM_HEADS)],
            axis=0).T
        y = late["y_conv"][rows] + jnp.dot(_rms(attn_rows, ona_ref[...]).astype(BF16), wout_ref[0:ATTN_WIDTH, :],
                                   preferred_element_type=F32)
        y = y + jnp.dot(_rms(mem_rows, onm_ref[...]).astype(BF16), wout_ref[ATTN_WIDTH + CONV_WIDTH:, :],
                        preferred_element_type=F32)
        out_ref[0, rows, :] = xres_ref[slot, rows, :] + y

    for piece in norms:
        piece()
    head_rows = {}
    chains_per_half = len(chains) // n_halves
    assert len(mem_units) == len(chains)
    for i, (b, g) in enumerate(chains):
        heads = range(g * GQA_GROUP, (g + 1) * GQA_GROUP)
        dist_b = dist_first if b == 0 else dist
        neg_slope = jnp.concatenate(
            [jnp.full((1, BLOCK), -(2.0 ** (-8.0 * (h + 1) / N_ATTN_HEADS)), F32) for h in heads], axis=1)
        sink = sinks_ref[:, g * GQA_GROUP * BLOCK:(g + 1) * GQA_GROUP * BLOCK]
        s2 = scores.pop((b, g))
        s = jnp.where(from_prev, s2[:BLOCK], s2[BLOCK:]) + dist_b * neg_slope
        m = jnp.maximum(jnp.max(s, axis=0, keepdims=True), sink)
        p = jnp.exp(s - m)
        l = jnp.sum(p, axis=0, keepdims=True) + jnp.exp(sink - m)
        p_t = jnp.concatenate(
            [jnp.where(from_prev, p, 0.0).astype(BF16), jnp.where(from_prev, 0.0, p).astype(BF16)],
            axis=0)
        for piece in chain_fill[i]:
            piece()
        o_t = jnp.dot(vt_cat[b, g], p_t, preferred_element_type=F32) * (1.0 / l)
        head_rows[b, g] = [o_t[:, hh * BLOCK:(hh + 1) * BLOCK] for hh in range(GQA_GROUP)]
        mem_softmax(*mem_units[i])
        if i == chains_per_half:
            finish_rows(0)
    for piece in tail_dots:
        piece()
    finish_rows(1)
    for piece in tail_posts:
        piece()


def _ffn_kernel(x_ref, gain_ref, wg_ref, wu_ref, wd_ref, out_ref):
    x = x_ref[...]
    h = _rms(x, gain_ref[...]).astype(BF16)
    y = x
    for c in range(wg_ref.shape[1] // V7X_MXU_DIM):
        cols = slice(c * V7X_MXU_DIM, (c + 1) * V7X_MXU_DIM)
        gate = jnp.dot(h, wg_ref[:, cols], preferred_element_type=F32)
        up = jnp.dot(h, wu_ref[:, cols], preferred_element_type=F32)
        act = (gate * jax.nn.sigmoid(gate) * up).astype(BF16)
        y = y + jnp.dot(act, wd_ref[cols, :], preferred_element_type=F32)
    out_ref[...] = y


def _const_spec(shape):
    return pl.BlockSpec(shape, lambda *_: (0,) * len(shape), pipeline_mode=pl.Buffered(1))


def _row(a):
    return a.reshape(1, -1)


def _head_column(gain, heads):
    return jnp.broadcast_to(jnp.tile(gain, heads)[:, None], (heads * HEAD_DIM, BLOCK))


def _layer(x, mem_blocks, lp, plan):
    seq_tile, ffn_tile, vmem_limit = plan
    batch, seq, _ = x.shape
    kblk, vblkt, w_in16, w_out16 = mem_blocks
    d_ff = lp["w_gate"].shape[1]

    tiles_per_seq = seq // seq_tile
    n_tiles = batch * tiles_per_seq
    slab_spec = pl.BlockSpec((D_MODEL // n_tiles, d_ff), lambda t: (t, 0))
    down_slabs = d_ff // BLOCK
    assert down_slabs <= n_tiles
    down_spec = pl.BlockSpec((BLOCK, D_MODEL), lambda t: (jnp.minimum(t, down_slabs - 1), 0))
    up_w16 = jax.ShapeDtypeStruct((D_MODEL, d_ff), BF16)
    down_w16 = jax.ShapeDtypeStruct((d_ff, D_MODEL), BF16)

    def tile_index(t):
        return (t // tiles_per_seq, t % tiles_per_seq, 0)

    mixer = pl.pallas_call(
        functools.partial(_mixer_kernel, seq_tile, tiles_per_seq),
        out_shape=(jax.ShapeDtypeStruct(x.shape, F32), up_w16, up_w16, down_w16),
        grid=(n_tiles,),
        in_specs=[
            _const_spec((1, seq_tile, D_MODEL)),
            pl.BlockSpec((1, seq_tile, D_MODEL), lambda t: tile_index(jnp.minimum(t + 1, n_tiles - 1))),
            pl.BlockSpec((1, N_MEM_HEADS * N_MEM, MEM_WIDTH), lambda t: (t // tiles_per_seq, 0, 0)),
            pl.BlockSpec((1, MEM_WIDTH, N_MEM_HEADS * N_MEM), lambda t: (t // tiles_per_seq, 0, 0)),
            _const_spec((D_MODEL, IN_PROJ_WIDTH)),
            _const_spec((D_MODEL, D_MODEL)),
            _const_spec((V7X_MXU_DIM, V7X_MXU_DIM)),
            _const_spec((1, D_MODEL)),
            _const_spec((ATTN_WIDTH, BLOCK)),
            _const_spec((1, KV_WIDTH)),
            _const_spec((MEM_WIDTH, BLOCK)),
            _const_spec((1, N_ATTN_HEADS * BLOCK)),
            _const_spec((CONV_K, CONV_WIDTH)),
            _const_spec((1, CONV_WIDTH)),
            _const_spec((1, ATTN_WIDTH)),
            _const_spec((1, CONV_WIDTH)),
            _const_spec((1, MEM_WIDTH)),
            slab_spec, slab_spec, down_spec,
        ],
        out_specs=(pl.BlockSpec((1, seq_tile, D_MODEL), tile_index), slab_spec, slab_spec, down_spec),
        scratch_shapes=[
            pltpu.VMEM((2, ATTN_WIDTH, seq_tile), BF16),
            pltpu.VMEM((2, MEM_WIDTH, seq_tile), BF16),
            pltpu.VMEM((2, seq_tile + BLOCK, KV_WIDTH), BF16),
            pltpu.VMEM((2, KV_WIDTH, seq_tile + BLOCK), BF16),
            pltpu.VMEM((2, seq_tile + V7X_SUBLANES, CONV_WIDTH), F32),
            pltpu.VMEM((2, seq_tile, CONV_WIDTH), F32),
            pltpu.VMEM((2, seq_tile, D_MODEL), F32),
        ],
        compiler_params=pltpu.CompilerParams(
            dimension_semantics=("arbitrary",), vmem_limit_bytes=vmem_limit),
        name="mixer",
    )
    x1, w_gate16, w_up16, w_down16 = mixer(
        x, x, kblk, vblkt, w_in16, w_out16, lp["gmat"],
        _row(lp["norm_mix"]), _head_column(lp["q_norm"], N_ATTN_HEADS),
        jnp.tile(_row(lp["k_norm"]), (1, N_KV_HEADS)), _head_column(lp["mem_q_norm"], N_MEM_HEADS),
        _row(jnp.repeat(lp["attn_sinks"], BLOCK)),
        lp["conv_w"], _row(lp["conv_b"]),
        _row(lp["out_norm_attn"]), _row(lp["out_norm_conv"]), _row(lp["out_norm_mem"]),
        lp["w_gate"], lp["w_up"], lp["w_down"])

    tokens = batch * seq
    ffn = pl.pallas_call(
        _ffn_kernel,
        out_shape=jax.ShapeDtypeStruct((tokens, D_MODEL), F32),
        grid=(tokens // ffn_tile,),
        in_specs=[
            pl.BlockSpec((ffn_tile, D_MODEL), lambda i: (i, 0)),
            _const_spec((1, D_MODEL)),
            _const_spec((D_MODEL, d_ff)),
            _const_spec((D_MODEL, d_ff)),
            _const_spec((d_ff, D_MODEL)),
        ],
        out_specs=pl.BlockSpec((ffn_tile, D_MODEL), lambda i: (i, 0)),
        compiler_params=pltpu.CompilerParams(
            dimension_semantics=("arbitrary",), vmem_limit_bytes=vmem_limit),
        name="ffn",
    )
    y = ffn(x1.reshape(tokens, D_MODEL), _row(lp["norm_ffn"]), w_gate16, w_up16, w_down16)
    return y.reshape(x.shape)


def _mem_kv(mem, lp, vmem_limit):
    batch = mem.shape[0]
    seqs_per_step = 4
    steps = batch // seqs_per_step
    kblk_shape = (batch, N_MEM_HEADS * N_MEM, MEM_WIDTH)
    vblkt_shape = (batch, MEM_WIDTH, N_MEM_HEADS * N_MEM)
    slab = D_MODEL // steps
    win_spec = pl.BlockSpec((slab, IN_PROJ_WIDTH), lambda b: (b, 0))
    wout_spec = pl.BlockSpec((slab, D_MODEL), lambda b: (b, 0))
    call = pl.pallas_call(
        _mem_kv_kernel,
        out_shape=(jax.ShapeDtypeStruct(kblk_shape, BF16), jax.ShapeDtypeStruct(vblkt_shape, BF16),
                   jax.ShapeDtypeStruct((D_MODEL, IN_PROJ_WIDTH), BF16),
                   jax.ShapeDtypeStruct((D_MODEL, D_MODEL), BF16)),
        grid=(steps,),
        in_specs=[
            pl.BlockSpec((seqs_per_step, N_MEM, D_MODEL), lambda b: (b, 0, 0)),
            _const_spec((1, D_MODEL)),
            _const_spec((D_MODEL, 2 * MEM_WIDTH)),
            _const_spec((1, MEM_WIDTH)),
            _const_spec((V7X_MXU_DIM, V7X_MXU_DIM)),
            win_spec, wout_spec,
        ],
        out_specs=(pl.BlockSpec((seqs_per_step,) + kblk_shape[1:], lambda b: (b, 0, 0)),
                   pl.BlockSpec((seqs_per_step,) + vblkt_shape[1:], lambda b: (b, 0, 0)),
                   win_spec, wout_spec),
        compiler_params=pltpu.CompilerParams(
            dimension_semantics=("arbitrary",), vmem_limit_bytes=vmem_limit),
        name="mem_kv",
    )
    return call(mem, _row(lp["norm_mem"]), lp["w_mem_kv"],
                jnp.tile(_row(lp["mem_k_norm"]), (1, N_MEM_HEADS)), lp["gmat"], lp["w_in"], lp["w_out"])


def kernel(x, mem, norm_mix, w_in, q_norm, k_norm, attn_sinks, conv_w, conv_b, norm_mem, w_mem_kv,
           mem_q_norm, mem_k_norm, out_norm_attn, out_norm_conv, out_norm_mem, w_out, norm_ffn,
           w_gate, w_up, w_down):
    plan = _plan()
    head_of_lane = np.arange(V7X_MXU_DIM) // HEAD_DIM
    gmat = jnp.asarray(head_of_lane[:, None] == head_of_lane[None, :], dtype=BF16)
    depth = w_in.shape[0]
    for l in range(depth):
        lp = dict(
            norm_mix=norm_mix[l], w_in=w_in[l], q_norm=q_norm[l], k_norm=k_norm[l],
            attn_sinks=attn_sinks[l], conv_w=conv_w[l], conv_b=conv_b[l], norm_mem=norm_mem[l],
            w_mem_kv=w_mem_kv[l], mem_q_norm=mem_q_norm[l], mem_k_norm=mem_k_norm[l],
            out_norm_attn=out_norm_attn[l], out_norm_conv=out_norm_conv[l],
            out_norm_mem=out_norm_mem[l], w_out=w_out[l], norm_ffn=norm_ffn[l],
            w_gate=w_gate[l], w_up=w_up[l], w_down=w_down[l],
            gmat=gmat)
        x = _layer(x, _mem_kv(mem, lp, plan[2]), lp, plan)
    return x
```

```python
import functools

import jax
import jax.numpy as jnp
import numpy as np
from jax import lax
from jax.experimental import pallas as pl
from jax.experimental.pallas import tpu as pltpu

D_MODEL = 1024
HEAD_DIM = 64
N_ATTN_HEADS = 8
N_KV_HEADS = 2
GQA_GROUP = N_ATTN_HEADS // N_KV_HEADS
BLOCK = 128
N_MEM_HEADS = 4
N_MEM = 256
CONV_K = 3
ATTN_WIDTH = N_ATTN_HEADS * HEAD_DIM
KV_WIDTH = N_KV_HEADS * HEAD_DIM
CONV_WIDTH = 256
MEM_WIDTH = N_MEM_HEADS * HEAD_DIM
IN_PROJ_WIDTH = ATTN_WIDTH + 2 * KV_WIDTH + 3 * CONV_WIDTH + MEM_WIDTH
EPS = 1e-6
MASKED_DIST = 2.0 ** 110

V7X_VMEM_BYTES = 64 * 1024 * 1024
V7X_SUBLANES = 8
V7X_LANES = 128
V7X_MXU_DIM = 256

F32 = jnp.float32
BF16 = jnp.bfloat16


def _plan():
    seq_tile = 4 * BLOCK
    ffn_tile = 1024
    vmem_limit = V7X_VMEM_BYTES - 8 * 1024 * 1024
    return seq_tile, ffn_tile, vmem_limit


def _rms(a, gain):
    return a * lax.rsqrt(jnp.mean(a * a, axis=-1, keepdims=True) + EPS) * gain


def _head_rms_scale(t, gmat):
    sq = (t * t).astype(BF16)
    ss = jnp.dot(sq, gmat, preferred_element_type=F32)
    return lax.rsqrt(ss * (1.0 / HEAD_DIM) + EPS)


def _head_rms_rows(t, gain):
    heads = t.shape[0] // HEAD_DIM
    out = []
    for h in range(heads):
        th = t[h * HEAD_DIM:(h + 1) * HEAD_DIM]
        ss = jnp.sum(th * th, axis=0, keepdims=True)
        out.append(th * lax.rsqrt(ss * (1.0 / HEAD_DIM) + EPS) * gain[h * HEAD_DIM:(h + 1) * HEAD_DIM])
    return out


def _mem_kv_kernel(mem_ref, gain_ref, w_ref, kgain_ref, gmat_ref, win_ref, wout_ref,
                   kblk_ref, vblkt_ref, win16_ref, wout16_ref):
    win16_ref[...] = win_ref[...].astype(BF16)
    wout16_ref[...] = wout_ref[...].astype(BF16)
    n_seq = mem_ref.shape[0]
    m = mem_ref[...].reshape(n_seq * N_MEM, D_MODEL)
    mn = _rms(m, gain_ref[...]).astype(BF16)
    kv = jnp.dot(mn, w_ref[...].astype(BF16), preferred_element_type=F32)
    k = kv[:, :MEM_WIDTH]
    kn = k * _head_rms_scale(k, gmat_ref[...]) * kgain_ref[...]
    lane_head = lax.broadcasted_iota(jnp.int32, (N_MEM, MEM_WIDTH), 1) // HEAD_DIM
    row_head = lax.broadcasted_iota(jnp.int32, (MEM_WIDTH, N_MEM), 0) // HEAD_DIM
    for s in range(n_seq):
        rows = slice(s * N_MEM, (s + 1) * N_MEM)
        kn_s = kn[rows]
        vt = kv[rows, MEM_WIDTH:].T
        for h in range(N_MEM_HEADS):
            kblk_ref[s, h * N_MEM:(h + 1) * N_MEM, :] = jnp.where(lane_head == h, kn_s, 0.0).astype(BF16)
            vblkt_ref[s, :, h * N_MEM:(h + 1) * N_MEM] = jnp.where(row_head == h, vt, 0.0).astype(BF16)


ROW_CHUNK = V7X_MXU_DIM
COL_GROUP = 2 * V7X_MXU_DIM
SCORE_LOOKAHEAD = 4
STAGE1_START_CHAIN = 0
CONV_PROJ_CHAIN = 3


def _stage1_pieces(x_ref, wslot, carry, refs):
    (win_ref, gmat_ref, nmix_ref, qgain_ref, kgain_ref, mqgain_ref,
     qnt_ref, qmnt_ref, kpad_ref, vtpad_ref, upad_ref, cb_ref, xres_ref) = refs
    seq_tile = cb_ref.shape[1]
    xn, val = {}, {}
    conv_base = ATTN_WIDTH + 2 * KV_WIDTH

    def carried(ref_slice_fn, shape, dtype):
        if carry is None:
            return jnp.zeros(shape, dtype)
        prev_slot, first_of_seq = carry
        return jnp.where(first_of_seq, jnp.zeros(shape, dtype), ref_slice_fn(prev_slot))

    def heads_t(t, gain_ref, out_ref, r):
        t = t.T
        gain = gain_ref[...] * HEAD_DIM ** -0.5
        gain = jnp.concatenate([gain] * (ROW_CHUNK // BLOCK), axis=1)
        out_ref[wslot, :, r * ROW_CHUNK:(r + 1) * ROW_CHUNK] = jnp.concatenate(
            _head_rms_rows(t, gain), axis=0).astype(BF16)

    norms, dots, posts, narrow_dots, narrow_posts = [], [], [], [], []
    for r in range(seq_tile // ROW_CHUNK):
        rows = slice(r * ROW_CHUNK, (r + 1) * ROW_CHUNK)

        def norm(r=r, rows=rows):
            x = x_ref[0, rows, :]
            xres_ref[wslot, rows, :] = x
            xn[r] = _rms(x, nmix_ref[...]).astype(BF16)

        def dot_piece(name, start, width, r=r):
            def run():
                val[name, r] = jnp.dot(xn[r], win_ref[:, start:start + width], preferred_element_type=F32)
            return run

        def post_q(r=r):
            heads_t(val.pop(("q", r)), qgain_ref, qnt_ref, r)

        def post_kv(r=r):
            t = val.pop(("kvch", r))
            kv_a = t[:, :2 * KV_WIDTH]
            val["ch", r] = t[:, 2 * KV_WIDTH:]
            scale = _head_rms_scale(kv_a, gmat_ref[...])
            kn = kv_a[:, :KV_WIDTH] * scale[:, :KV_WIDTH] * kgain_ref[...]
            kpad_ref[wslot, BLOCK + r * ROW_CHUNK:BLOCK + (r + 1) * ROW_CHUNK, :] = kn.astype(BF16)
            vtpad_ref[wslot, :, BLOCK + r * ROW_CHUNK:BLOCK + (r + 1) * ROW_CHUNK] = (
                kv_a[:, KV_WIDTH:].T.astype(BF16))
            if r == 0:
                kpad_ref[wslot, 0:BLOCK, :] = carried(
                    lambda s: kpad_ref[s, seq_tile:seq_tile + BLOCK, :], (BLOCK, KV_WIDTH), BF16)
                vtpad_ref[wslot, :, 0:BLOCK] = carried(
                    lambda s: vtpad_ref[s, :, seq_tile:seq_tile + BLOCK], (KV_WIDTH, BLOCK), BF16)

        def post_conv(r=r):
            t = val.pop(("cbcc", r))
            cb_ref[wslot, r * ROW_CHUNK:(r + 1) * ROW_CHUNK, :] = t[:, :CONV_WIDTH]
            u = t[:, CONV_WIDTH:] * val.pop(("ch", r))
            upad_ref[wslot, V7X_SUBLANES + r * ROW_CHUNK:V7X_SUBLANES + (r + 1) * ROW_CHUNK, :] = u
            if r == 0:
                upad_ref[wslot, 0:V7X_SUBLANES, :] = carried(
                    lambda s: upad_ref[s, seq_tile:seq_tile + V7X_SUBLANES, :],
                    (V7X_SUBLANES, CONV_WIDTH), F32)

        def post_qm(r=r):
            heads_t(val.pop(("qm", r)), mqgain_ref, qmnt_ref, r)

        norms.append(norm)
        dots += [dot_piece("q", 0, COL_GROUP), dot_piece("kvch", ATTN_WIDTH, COL_GROUP),
                 dot_piece("cbcc", conv_base + CONV_WIDTH, COL_GROUP)]
        posts += [post_q, post_kv, post_conv]
        narrow_dots.append(dot_piece("qm", IN_PROJ_WIDTH - MEM_WIDTH, MEM_WIDTH))
        narrow_posts.append(post_qm)
    return norms, dots + narrow_dots, posts + narrow_posts


def _mixer_kernel(seq_tile, tiles_per_seq,
                  xfirst_ref, xnext_ref, kblk_ref, vblkt_ref, win_ref, wout_ref, gmat_ref,
                  nmix_ref, qgain_ref, kgain_ref, mqgain_ref, sinks_ref, convw_ref, convb_ref,
                  ona_ref, onc_ref, onm_ref, wg_ref, wu_ref, wd_ref,
                  out_ref, wg16_ref, wu16_ref, wd16_ref,
                  qnt_ref, qmnt_ref, kpad_ref, vtpad_ref, upad_ref, cb_ref, xres_ref):
    wg16_ref[...] = wg_ref[...].astype(BF16)
    wu16_ref[...] = wu_ref[...].astype(BF16)
    wd16_ref[...] = wd_ref[...].astype(BF16)
    t = pl.program_id(0)
    slot = t % 2
    n_blocks = seq_tile // BLOCK
    first_of_seq = (t % tiles_per_seq) == 0
    stage1_refs = (win_ref, gmat_ref, nmix_ref, qgain_ref, kgain_ref, mqgain_ref,
                   qnt_ref, qmnt_ref, kpad_ref, vtpad_ref, upad_ref, cb_ref, xres_ref)

    @pl.when(t == 0)
    def _():
        norms, dots, posts = _stage1_pieces(xfirst_ref, 0, None, stage1_refs)
        for piece in norms + [p for pair in zip(dots, posts) for p in pair]:
            piece()

    norms, dots, posts = _stage1_pieces(
        xnext_ref, 1 - slot, (slot, ((t + 1) % tiles_per_seq) == 0), stage1_refs)
    n_chains = n_blocks * N_KV_HEADS
    held = 2
    n_early = len(dots) - held
    early = [[dots[0], dots[1]]] + [
        ([posts[k - 1]] if k - 1 < n_early else []) + ([dots[k + 1]] if k + 1 < n_early else [])
        for k in range(1, n_early + 1)]
    chain_fill = [[] for _ in range(n_chains)]
    for k, pieces in enumerate(early):
        if STAGE1_START_CHAIN + k < n_chains:
            chain_fill[STAGE1_START_CHAIN + k] = pieces
    leftover = [p for pieces in early[n_chains - STAGE1_START_CHAIN:] for p in pieces]
    tail_dots, tail_posts = leftover + dots[n_early:], posts[n_early:]

    key = lax.broadcasted_iota(jnp.int32, (BLOCK, GQA_GROUP * BLOCK), 0)
    qry = lax.broadcasted_iota(jnp.int32, (BLOCK, GQA_GROUP * BLOCK), 1) % BLOCK
    from_prev = key > qry
    dist = jnp.where(from_prev, qry + BLOCK - key, qry - key).astype(F32)
    dist_first = jnp.where(jnp.logical_and(from_prev, first_of_seq), MASKED_DIST, dist)
    zeros_q = jnp.zeros((HEAD_DIM, GQA_GROUP * BLOCK), BF16)

    chains = [(b, g) for b in range(n_blocks) for g in range(N_KV_HEADS)]
    vt_cat = {(b, g): vtpad_ref[slot, g * HEAD_DIM:(g + 1) * HEAD_DIM, b * BLOCK:(b + 2) * BLOCK]
              for b, g in chains}
    qmn = qmnt_ref[slot]
    n_halves = seq_tile // V7X_MXU_DIM
    u = upad_ref[slot, V7X_SUBLANES:V7X_SUBLANES + seq_tile, :]
    u1 = upad_ref[slot, V7X_SUBLANES - 1:V7X_SUBLANES - 1 + seq_tile, :]
    u2 = upad_ref[slot, V7X_SUBLANES - 2:V7X_SUBLANES - 2 + seq_tile, :]
    cw = convw_ref[...]
    conv = cw[0:1] * u2 + cw[1:2] * u1 + cw[2:3] * u + convb_ref[...]
    conv_n = _rms(cb_ref[slot] * conv, onc_ref[...])

    scores, score_ops = {}, {}
    for b, g in chains:
        cols = slice(b * BLOCK, (b + 1) * BLOCK)
        k_cat = kpad_ref[slot, b * BLOCK:(b + 2) * BLOCK, :]
        heads = range(g * GQA_GROUP, (g + 1) * GQA_GROUP)
        q4 = jnp.concatenate(
            [qnt_ref[slot, h * HEAD_DIM:(h + 1) * HEAD_DIM, cols] for h in heads], axis=1)
        score_ops[b, g] = (k_cat, jnp.concatenate([q4, zeros_q] if g == 0 else [zeros_q, q4], axis=0))

    def score_chain(b, g):
        k_cat, w_q = score_ops.pop((b, g))
        scores[b, g] = jnp.dot(k_cat, w_q, preferred_element_type=F32)

    for b, g in chains[:SCORE_LOOKAHEAD]:
        score_chain(b, g)
    late = {}

    def conv_proj():
        late["y_conv"] = jnp.dot(conv_n.astype(BF16), wout_ref[ATTN_WIDTH:ATTN_WIDTH + CONV_WIDTH, :],
                                 preferred_element_type=F32)

    chain_fill[CONV_PROJ_CHAIN].insert(0, conv_proj)
    kblk = kblk_ref[0]
    vblkt = vblkt_ref[0]
    mem_scores = jnp.dot(kblk, qmn, preferred_element_type=F32)
    mem_units = [(c, h) for c in range(n_halves) for h in range(N_MEM_HEADS)]
    mem_p, mem_inv_l = {}, {}

    def mem_softmax(c, h):
        s = mem_scores[h * N_MEM:(h + 1) * N_MEM, c * V7X_MXU_DIM:(c + 1) * V7X_MXU_DIM]
        m = jnp.max(s, axis=0, keepdims=True)
        p = jnp.exp(s - m)
        mem_inv_l[c, h] = 1.0 / jnp.sum(p, axis=0, keepdims=True)
        mem_p[c, h] = p.astype(BF16)

    def finish_rows(c):
        rows = slice(c * V7X_MXU_DIM, (c + 1) * V7X_MXU_DIM)
        blocks = range(c * V7X_MXU_DIM // BLOCK, (c + 1) * V7X_MXU_DIM // BLOCK)
        attn_rows = jnp.concatenate(
            [jnp.concatenate([row for g in range(N_KV_HEADS) for row in head_rows[b, g]], axis=0)
             for b in blocks], axis=1).T
        p_all = jnp.concatenate([mem_p[c, h] for h in range(N_MEM_HEADS)], axis=0)
        o_t = jnp.dot(vblkt, p_all, preferred_element_type=F32)
        mem_rows = jnp.concatenate(
            [o_t[h * HEAD_DIM:(h + 1) * HEAD_DIM] * mem_inv_l[c, h] for h in range(N_MEM_HEADS)],
            axis=0).T
        y = late["y_conv"][rows] + jnp.dot(_rms(attn_rows, ona_ref[...]).astype(BF16), wout_ref[0:ATTN_WIDTH, :],
                                   preferred_element_type=F32)
        y = y + jnp.dot(_rms(mem_rows, onm_ref[...]).astype(BF16), wout_ref[ATTN_WIDTH + CONV_WIDTH:, :],
                        preferred_element_type=F32)
        out_ref[0, rows, :] = xres_ref[slot, rows, :] + y

    for piece in norms:
        piece()
    head_rows = {}
    chains_per_half = len(chains) // n_halves
    assert len(mem_units) == len(chains)
    for i, (b, g) in enumerate(chains):
        heads = range(g * GQA_GROUP, (g + 1) * GQA_GROUP)
        dist_b = dist_first if b == 0 else dist
        neg_slope = jnp.concatenate(
            [jnp.full((1, BLOCK), -(2.0 ** (-8.0 * (h + 1) / N_ATTN_HEADS)), F32) for h in heads], axis=1)
        sink = sinks_ref[:, g * GQA_GROUP * BLOCK:(g + 1) * GQA_GROUP * BLOCK]
        if i + SCORE_LOOKAHEAD < len(chains):
            score_chain(*chains[i + SCORE_LOOKAHEAD])
        s2 = scores.pop((b, g))
        s = jnp.where(from_prev, s2[:BLOCK], s2[BLOCK:]) + dist_b * neg_slope
        m = jnp.maximum(jnp.max(s, axis=0, keepdims=True), sink)
        p = jnp.exp(s - m)
        l = jnp.sum(p, axis=0, keepdims=True) + jnp.exp(sink - m)
        p_t = jnp.concatenate(
            [jnp.where(from_prev, p, 0.0).astype(BF16), jnp.where(from_prev, 0.0, p).astype(BF16)],
            axis=0)
        for piece in chain_fill[i]:
            piece()
        o_t = jnp.dot(vt_cat[b, g], p_t, preferred_element_type=F32) * (1.0 / l)
        head_rows[b, g] = [o_t[:, hh * BLOCK:(hh + 1) * BLOCK] for hh in range(GQA_GROUP)]
        mem_softmax(*mem_units[i])
        if i == chains_per_half:
            finish_rows(0)
    for piece in tail_dots:
        piece()
    finish_rows(1)
    for piece in tail_posts:
        piece()


def _ffn_kernel(x_ref, gain_ref, wg_ref, wu_ref, wd_ref, out_ref):
    x = x_ref[...]
    h = _rms(x, gain_ref[...]).astype(BF16)
    y = x
    for c in range(wg_ref.shape[1] // V7X_MXU_DIM):
        cols = slice(c * V7X_MXU_DIM, (c + 1) * V7X_MXU_DIM)
        gate = jnp.dot(h, wg_ref[:, cols], preferred_element_type=F32)
        up = jnp.dot(h, wu_ref[:, cols], preferred_element_type=F32)
        act = (gate * jax.nn.sigmoid(gate) * up).astype(BF16)
        y = y + jnp.dot(act, wd_ref[cols, :], preferred_element_type=F32)
    out_ref[...] = y


def _const_spec(shape):
    return pl.BlockSpec(shape, lambda *_: (0,) * len(shape), pipeline_mode=pl.Buffered(1))


def _row(a):
    return a.reshape(1, -1)


def _head_column(gain, heads):
    return jnp.broadcast_to(jnp.tile(gain, heads)[:, None], (heads * HEAD_DIM, BLOCK))


def _layer(x, mem_blocks, lp, plan):
    seq_tile, ffn_tile, vmem_limit = plan
    batch, seq, _ = x.shape
    kblk, vblkt, w_in16, w_out16 = mem_blocks
    d_ff = lp["w_gate"].shape[1]

    tiles_per_seq = seq // seq_tile
    n_tiles = batch * tiles_per_seq
    slab_spec = pl.BlockSpec((D_MODEL // n_tiles, d_ff), lambda t: (t, 0))
    down_slabs = d_ff // BLOCK
    assert down_slabs <= n_tiles
    down_spec = pl.BlockSpec((BLOCK, D_MODEL), lambda t: (jnp.minimum(t, down_slabs - 1), 0))
    up_w16 = jax.ShapeDtypeStruct((D_MODEL, d_ff), BF16)
    down_w16 = jax.ShapeDtypeStruct((d_ff, D_MODEL), BF16)

    def tile_index(t):
        return (t // tiles_per_seq, t % tiles_per_seq, 0)

    mixer = pl.pallas_call(
        functools.partial(_mixer_kernel, seq_tile, tiles_per_seq),
        out_shape=(jax.ShapeDtypeStruct(x.shape, F32), up_w16, up_w16, down_w16),
        grid=(n_tiles,),
        in_specs=[
            _const_spec((1, seq_tile, D_MODEL)),
            pl.BlockSpec((1, seq_tile, D_MODEL), lambda t: tile_index(jnp.minimum(t + 1, n_tiles - 1))),
            pl.BlockSpec((1, N_MEM_HEADS * N_MEM, MEM_WIDTH), lambda t: (t // tiles_per_seq, 0, 0)),
            pl.BlockSpec((1, MEM_WIDTH, N_MEM_HEADS * N_MEM), lambda t: (t // tiles_per_seq, 0, 0)),
            _const_spec((D_MODEL, IN_PROJ_WIDTH)),
            _const_spec((D_MODEL, D_MODEL)),
            _const_spec((V7X_MXU_DIM, V7X_MXU_DIM)),
            _const_spec((1, D_MODEL)),
            _const_spec((ATTN_WIDTH, BLOCK)),
            _const_spec((1, KV_WIDTH)),
            _const_spec((MEM_WIDTH, BLOCK)),
            _const_spec((1, N_ATTN_HEADS * BLOCK)),
            _const_spec((CONV_K, CONV_WIDTH)),
            _const_spec((1, CONV_WIDTH)),
            _const_spec((1, ATTN_WIDTH)),
            _const_spec((1, CONV_WIDTH)),
            _const_spec((1, MEM_WIDTH)),
            slab_spec, slab_spec, down_spec,
        ],
        out_specs=(pl.BlockSpec((1, seq_tile, D_MODEL), tile_index), slab_spec, slab_spec, down_spec),
        scratch_shapes=[
            pltpu.VMEM((2, ATTN_WIDTH, seq_tile), BF16),
            pltpu.VMEM((2, MEM_WIDTH, seq_tile), BF16),
            pltpu.VMEM((2, seq_tile + BLOCK, KV_WIDTH), BF16),
            pltpu.VMEM((2, KV_WIDTH, seq_tile + BLOCK), BF16),
            pltpu.VMEM((2, seq_tile + V7X_SUBLANES, CONV_WIDTH), F32),
            pltpu.VMEM((2, seq_tile, CONV_WIDTH), F32),
            pltpu.VMEM((2, seq_tile, D_MODEL), F32),
        ],
        compiler_params=pltpu.CompilerParams(
            dimension_semantics=("arbitrary",), vmem_limit_bytes=vmem_limit),
        name="mixer",
    )
    x1, w_gate16, w_up16, w_down16 = mixer(
        x, x, kblk, vblkt, w_in16, w_out16, lp["gmat"],
        _row(lp["norm_mix"]), _head_column(lp["q_norm"], N_ATTN_HEADS),
        jnp.tile(_row(lp["k_norm"]), (1, N_KV_HEADS)), _head_column(lp["mem_q_norm"], N_MEM_HEADS),
        _row(jnp.repeat(lp["attn_sinks"], BLOCK)),
        lp["conv_w"], _row(lp["conv_b"]),
        _row(lp["out_norm_attn"]), _row(lp["out_norm_conv"]), _row(lp["out_norm_mem"]),
        lp["w_gate"], lp["w_up"], lp["w_down"])

    tokens = batch * seq
    ffn = pl.pallas_call(
        _ffn_kernel,
        out_shape=jax.ShapeDtypeStruct((tokens, D_MODEL), F32),
        grid=(tokens // ffn_tile,),
        in_specs=[
            pl.BlockSpec((ffn_tile, D_MODEL), lambda i: (i, 0)),
            _const_spec((1, D_MODEL)),
            _const_spec((D_MODEL, d_ff)),
            _const_spec((D_MODEL, d_ff)),
            _const_spec((d_ff, D_MODEL)),
        ],
        out_specs=pl.BlockSpec((ffn_tile, D_MODEL), lambda i: (i, 0)),
        compiler_params=pltpu.CompilerParams(
            dimension_semantics=("arbitrary",), vmem_limit_bytes=vmem_limit),
        name="ffn",
    )
    y = ffn(x1.reshape(tokens, D_MODEL), _row(lp["norm_ffn"]), w_gate16, w_up16, w_down16)
    return y.reshape(x.shape)


def _mem_kv(mem, lp, vmem_limit):
    batch = mem.shape[0]
    seqs_per_step = 4
    steps = batch // seqs_per_step
    kblk_shape = (batch, N_MEM_HEADS * N_MEM, MEM_WIDTH)
    vblkt_shape = (batch, MEM_WIDTH, N_MEM_HEADS * N_MEM)
    slab = D_MODEL // steps
    win_spec = pl.BlockSpec((slab, IN_PROJ_WIDTH), lambda b: (b, 0))
    wout_spec = pl.BlockSpec((slab, D_MODEL), lambda b: (b, 0))
    call = pl.pallas_call(
        _mem_kv_kernel,
        out_shape=(jax.ShapeDtypeStruct(kblk_shape, BF16), jax.ShapeDtypeStruct(vblkt_shape, BF16),
                   jax.ShapeDtypeStruct((D_MODEL, IN_PROJ_WIDTH), BF16),
                   jax.ShapeDtypeStruct((D_MODEL, D_MODEL), BF16)),
        grid=(steps,),
        in_specs=[
            pl.BlockSpec((seqs_per_step, N_MEM, D_MODEL), lambda b: (b, 0, 0)),
            _const_spec((1, D_MODEL)),
            _const_spec((D_MODEL, 2 * MEM_WIDTH)),
            _const_spec((1, MEM_WIDTH)),
            _const_spec((V7X_MXU_DIM, V7X_MXU_DIM)),
            win_spec, wout_spec,
        ],
        out_specs=(pl.BlockSpec((seqs_per_step,) + kblk_shape[1:], lambda b: (b, 0, 0)),
                   pl.BlockSpec((seqs_per_step,) + vblkt_shape[1:], lambda b: (b, 0, 0)),
                   win_spec, wout_spec),
        compiler_params=pltpu.CompilerParams(
            dimension_semantics=("arbitrary",), vmem_limit_bytes=vmem_limit),
        name="mem_kv",
    )
    return call(mem, _row(lp["norm_mem"]), lp["w_mem_kv"],
                jnp.tile(_row(lp["mem_k_norm"]), (1, N_MEM_HEADS)), lp["gmat"], lp["w_in"], lp["w_out"])


def kernel(x, mem, norm_mix, w_in, q_norm, k_norm, attn_sinks, conv_w, conv_b, norm_mem, w_mem_kv,
           mem_q_norm, mem_k_norm, out_norm_attn, out_norm_conv, out_norm_mem, w_out, norm_ffn,
           w_gate, w_up, w_down):
    plan = _plan()
    head_of_lane = np.arange(V7X_MXU_DIM) // HEAD_DIM
    gmat = jnp.asarray(head_of_lane[:, None] == head_of_lane[None, :], dtype=BF16)
    depth = w_in.shape[0]
    for l in range(depth):
        lp = dict(
            norm_mix=norm_mix[l], w_in=w_in[l], q_norm=q_norm[l], k_norm=k_norm[l],
            attn_sinks=attn_sinks[l], conv_w=conv_w[l], conv_b=conv_b[l], norm_mem=norm_mem[l],
            w_mem_kv=w_mem_kv[l], mem_q_norm=mem_q_norm[l], mem_k_norm=mem_k_norm[l],
            out_norm_attn=out_norm_attn[l], out_norm_conv=out_norm_conv[l],
            out_norm_mem=out_norm_mem[l], w_out=w_out[l], norm_ffn=norm_ffn[l],
            w_gate=w_gate[l], w_up=w_up[l], w_down=w_down[l],
            gmat=gmat)
        x = _layer(x, _mem_kv(mem, lp, plan[2]), lp, plan)
    return x
```

```python
import functools

import jax
import jax.numpy as jnp
import numpy as np
from jax import lax
from jax.experimental import pallas as pl
from jax.experimental.pallas import tpu as pltpu

D_MODEL = 1024
HEAD_DIM = 64
N_ATTN_HEADS = 8
N_KV_HEADS = 2
GQA_GROUP = N_ATTN_HEADS // N_KV_HEADS
BLOCK = 128
N_MEM_HEADS = 4
N_MEM = 256
CONV_K = 3
ATTN_WIDTH = N_ATTN_HEADS * HEAD_DIM
KV_WIDTH = N_KV_HEADS * HEAD_DIM
CONV_WIDTH = 256
MEM_WIDTH = N_MEM_HEADS * HEAD_DIM
IN_PROJ_WIDTH = ATTN_WIDTH + 2 * KV_WIDTH + 3 * CONV_WIDTH + MEM_WIDTH
EPS = 1e-6
LOG2E = 1.4426950408889634
MASKED_DIST = 2.0 ** 110

V7X_VMEM_BYTES = 64 * 1024 * 1024
V7X_SUBLANES = 8
V7X_LANES = 128
V7X_MXU_DIM = 256

F32 = jnp.float32
BF16 = jnp.bfloat16


def _plan():
    seq_tile = 4 * BLOCK
    ffn_tile = 1024
    vmem_limit = V7X_VMEM_BYTES - 8 * 1024 * 1024
    return seq_tile, ffn_tile, vmem_limit


def _rms(a, gain):
    return a * lax.rsqrt(jnp.mean(a * a, axis=-1, keepdims=True) + EPS) * gain


def _head_rms_scale(t, gmat):
    sq = (t * t).astype(BF16)
    ss = jnp.dot(sq, gmat, preferred_element_type=F32)
    return lax.rsqrt(ss * (1.0 / HEAD_DIM) + EPS)


def _head_rms_rows(t, gain):
    heads = t.shape[0] // HEAD_DIM
    out = []
    for h in range(heads):
        th = t[h * HEAD_DIM:(h + 1) * HEAD_DIM]
        ss = jnp.sum(th * th, axis=0, keepdims=True)
        out.append(th * lax.rsqrt(ss * (1.0 / HEAD_DIM) + EPS) * gain[h * HEAD_DIM:(h + 1) * HEAD_DIM])
    return out


def _mem_kv_kernel(mem_ref, gain_ref, w_ref, kgain_ref, gmat_ref, win_ref, wout_ref,
                   kblk_ref, vblkt_ref, win16_ref, wout16_ref):
    win16_ref[...] = win_ref[...].astype(BF16)
    wout16_ref[...] = wout_ref[...].astype(BF16)
    n_seq = mem_ref.shape[0]
    m = mem_ref[...].reshape(n_seq * N_MEM, D_MODEL)
    mn = _rms(m, gain_ref[...]).astype(BF16)
    kv = jnp.dot(mn, w_ref[...].astype(BF16), preferred_element_type=F32)
    k = kv[:, :MEM_WIDTH]
    kn = k * _head_rms_scale(k, gmat_ref[...]) * kgain_ref[...]
    lane_head = lax.broadcasted_iota(jnp.int32, (N_MEM, MEM_WIDTH), 1) // HEAD_DIM
    row_head = lax.broadcasted_iota(jnp.int32, (MEM_WIDTH, N_MEM), 0) // HEAD_DIM
    for s in range(n_seq):
        rows = slice(s * N_MEM, (s + 1) * N_MEM)
        kn_s = kn[rows]
        vt = kv[rows, MEM_WIDTH:].T
        for h in range(N_MEM_HEADS):
            kblk_ref[s, h * N_MEM:(h + 1) * N_MEM, :] = jnp.where(lane_head == h, kn_s, 0.0).astype(BF16)
            vblkt_ref[s, :, h * N_MEM:(h + 1) * N_MEM] = jnp.where(row_head == h, vt, 0.0).astype(BF16)


ROW_CHUNK = V7X_MXU_DIM
COL_GROUP = 2 * V7X_MXU_DIM
SCORE_LOOKAHEAD = 4
POST_LAG = 1
STAGE1_START_CHAIN = 0
CONV_PROJ_CHAIN = 3


def _stage1_pieces(x_ref, wslot, carry, refs):
    (win_ref, gmat_ref, nmix_ref, qgain_ref, kgain_ref, mqgain_ref,
     qnt_ref, qmnt_ref, kpad_ref, vtpad_ref, upad_ref, cb_ref, xres_ref) = refs
    seq_tile = cb_ref.shape[1]
    xn, val = {}, {}
    conv_base = ATTN_WIDTH + 2 * KV_WIDTH

    def carried(ref_slice_fn, shape, dtype):
        if carry is None:
            return jnp.zeros(shape, dtype)
        prev_slot, first_of_seq = carry
        return jnp.where(first_of_seq, jnp.zeros(shape, dtype), ref_slice_fn(prev_slot))

    def heads_t(t, gain_ref, out_ref, r):
        t = t.T
        gain = gain_ref[...] * (HEAD_DIM ** -0.5 * LOG2E)
        gain = jnp.concatenate([gain] * (ROW_CHUNK // BLOCK), axis=1)
        out_ref[wslot, :, r * ROW_CHUNK:(r + 1) * ROW_CHUNK] = jnp.concatenate(
            _head_rms_rows(t, gain), axis=0).astype(BF16)

    norms, dots, posts, narrow_dots, narrow_posts = [], [], [], [], []
    for r in range(seq_tile // ROW_CHUNK):
        rows = slice(r * ROW_CHUNK, (r + 1) * ROW_CHUNK)

        def norm(r=r, rows=rows):
            x = x_ref[0, rows, :]
            xres_ref[wslot, rows, :] = x
            xn[r] = _rms(x, nmix_ref[...]).astype(BF16)

        def dot_piece(name, start, width, r=r):
            def run():
                val[name, r] = jnp.dot(xn[r], win_ref[:, start:start + width], preferred_element_type=F32)
            return run

        def post_q(r=r):
            heads_t(val.pop(("q", r)), qgain_ref, qnt_ref, r)

        def post_kv(r=r):
            t = val.pop(("kvch", r))
            kv_a = t[:, :2 * KV_WIDTH]
            val["ch", r] = t[:, 2 * KV_WIDTH:]
            scale = _head_rms_scale(kv_a, gmat_ref[...])
            kn = kv_a[:, :KV_WIDTH] * scale[:, :KV_WIDTH] * kgain_ref[...]
            kpad_ref[wslot, BLOCK + r * ROW_CHUNK:BLOCK + (r + 1) * ROW_CHUNK, :] = kn.astype(BF16)
            vtpad_ref[wslot, :, BLOCK + r * ROW_CHUNK:BLOCK + (r + 1) * ROW_CHUNK] = (
                kv_a[:, KV_WIDTH:].T.astype(BF16))
            if r == 0:
                kpad_ref[wslot, 0:BLOCK, :] = carried(
                    lambda s: kpad_ref[s, seq_tile:seq_tile + BLOCK, :], (BLOCK, KV_WIDTH), BF16)
                vtpad_ref[wslot, :, 0:BLOCK] = carried(
                    lambda s: vtpad_ref[s, :, seq_tile:seq_tile + BLOCK], (KV_WIDTH, BLOCK), BF16)

        def post_conv(r=r):
            t = val.pop(("cbcc", r))
            cb_ref[wslot, r * ROW_CHUNK:(r + 1) * ROW_CHUNK, :] = t[:, :CONV_WIDTH]
            u = t[:, CONV_WIDTH:] * val.pop(("ch", r))
            upad_ref[wslot, V7X_SUBLANES + r * ROW_CHUNK:V7X_SUBLANES + (r + 1) * ROW_CHUNK, :] = u
            if r == 0:
                upad_ref[wslot, 0:V7X_SUBLANES, :] = carried(
                    lambda s: upad_ref[s, seq_tile:seq_tile + V7X_SUBLANES, :],
                    (V7X_SUBLANES, CONV_WIDTH), F32)

        def post_qm(r=r):
            heads_t(val.pop(("qm", r)), mqgain_ref, qmnt_ref, r)

        norms.append(norm)
        dots += [dot_piece("q", 0, COL_GROUP), dot_piece("kvch", ATTN_WIDTH, COL_GROUP),
                 dot_piece("cbcc", conv_base + CONV_WIDTH, COL_GROUP)]
        posts += [post_q, post_kv, post_conv]
        narrow_dots.append(dot_piece("qm", IN_PROJ_WIDTH - MEM_WIDTH, MEM_WIDTH))
        narrow_posts.append(post_qm)
    return norms, dots + narrow_dots, posts + narrow_posts


def _mixer_kernel(seq_tile, tiles_per_seq,
                  xfirst_ref, xnext_ref, kblk_ref, vblkt_ref, win_ref, wout_ref, gmat_ref,
                  nmix_ref, qgain_ref, kgain_ref, mqgain_ref, sinks_ref, convw_ref, convb_ref,
                  ona_ref, onc_ref, onm_ref, wg_ref, wu_ref, wd_ref,
                  out_ref, wg16_ref, wu16_ref, wd16_ref,
                  qnt_ref, qmnt_ref, kpad_ref, vtpad_ref, upad_ref, cb_ref, xres_ref):
    wg16_ref[...] = wg_ref[...].astype(BF16)
    wu16_ref[...] = wu_ref[...].astype(BF16)
    wd16_ref[...] = wd_ref[...].astype(BF16)
    t = pl.program_id(0)
    slot = t % 2
    n_blocks = seq_tile // BLOCK
    first_of_seq = (t % tiles_per_seq) == 0
    stage1_refs = (win_ref, gmat_ref, nmix_ref, qgain_ref, kgain_ref, mqgain_ref,
                   qnt_ref, qmnt_ref, kpad_ref, vtpad_ref, upad_ref, cb_ref, xres_ref)

    @pl.when(t == 0)
    def _():
        norms, dots, posts = _stage1_pieces(xfirst_ref, 0, None, stage1_refs)
        for piece in norms + [p for pair in zip(dots, posts) for p in pair]:
            piece()

    norms, dots, posts = _stage1_pieces(
        xnext_ref, 1 - slot, (slot, ((t + 1) % tiles_per_seq) == 0), stage1_refs)
    n_chains = n_blocks * N_KV_HEADS
    held = 2
    n_early = len(dots) - held
    early = [[dots[0], dots[1]]] + [
        ([posts[k - POST_LAG]] if 0 <= k - POST_LAG < n_early else [])
        + ([dots[k + 1]] if k + 1 < n_early else [])
        for k in range(1, n_early + POST_LAG)]
    chain_fill = [[] for _ in range(n_chains)]
    for k, pieces in enumerate(early):
        if STAGE1_START_CHAIN + k < n_chains:
            chain_fill[STAGE1_START_CHAIN + k] = pieces
    leftover = [p for pieces in early[n_chains - STAGE1_START_CHAIN:] for p in pieces]
    tail_dots, tail_posts = leftover + dots[n_early:], posts[n_early:]

    key = lax.broadcasted_iota(jnp.int32, (BLOCK, GQA_GROUP * BLOCK), 0)
    qry = lax.broadcasted_iota(jnp.int32, (BLOCK, GQA_GROUP * BLOCK), 1) % BLOCK
    from_prev = key > qry
    dist = jnp.where(from_prev, qry + BLOCK - key, qry - key).astype(F32)
    dist_first = jnp.where(jnp.logical_and(from_prev, first_of_seq), MASKED_DIST, dist)
    zeros_q = jnp.zeros((HEAD_DIM, GQA_GROUP * BLOCK), BF16)

    chains = [(b, g) for b in range(n_blocks) for g in range(N_KV_HEADS)]
    vt_cat = {(b, g): vtpad_ref[slot, g * HEAD_DIM:(g + 1) * HEAD_DIM, b * BLOCK:(b + 2) * BLOCK]
              for b, g in chains}
    qmn = qmnt_ref[slot]
    n_halves = seq_tile // V7X_MXU_DIM
    u = upad_ref[slot, V7X_SUBLANES:V7X_SUBLANES + seq_tile, :]
    u1 = upad_ref[slot, V7X_SUBLANES - 1:V7X_SUBLANES - 1 + seq_tile, :]
    u2 = upad_ref[slot, V7X_SUBLANES - 2:V7X_SUBLANES - 2 + seq_tile, :]
    cw = convw_ref[...]
    conv = cw[0:1] * u2 + cw[1:2] * u1 + cw[2:3] * u + convb_ref[...]
    conv_n = _rms(cb_ref[slot] * conv, onc_ref[...])

    scores, score_ops = {}, {}
    for b, g in chains:
        cols = slice(b * BLOCK, (b + 1) * BLOCK)
        k_cat = kpad_ref[slot, b * BLOCK:(b + 2) * BLOCK, :]
        heads = range(g * GQA_GROUP, (g + 1) * GQA_GROUP)
        q4 = jnp.concatenate(
            [qnt_ref[slot, h * HEAD_DIM:(h + 1) * HEAD_DIM, cols] for h in heads], axis=1)
        score_ops[b, g] = (k_cat, jnp.concatenate([q4, zeros_q] if g == 0 else [zeros_q, q4], axis=0))

    def score_chain(b, g):
        k_cat, w_q = score_ops.pop((b, g))
        scores[b, g] = jnp.dot(k_cat, w_q, preferred_element_type=F32)

    for b, g in chains[:SCORE_LOOKAHEAD]:
        score_chain(b, g)
    late = {}

    def conv_proj():
        late["y_conv"] = jnp.dot(conv_n.astype(BF16), wout_ref[ATTN_WIDTH:ATTN_WIDTH + CONV_WIDTH, :],
                                 preferred_element_type=F32)

    chain_fill[CONV_PROJ_CHAIN].insert(0, conv_proj)
    kblk = kblk_ref[0]
    vblkt = vblkt_ref[0]
    mem_scores = jnp.dot(kblk, qmn, preferred_element_type=F32)
    mem_units = [(c, h) for c in range(n_halves) for h in range(N_MEM_HEADS)]
    mem_p, mem_inv_l = {}, {}

    def mem_softmax(c, h):
        s = mem_scores[h * N_MEM:(h + 1) * N_MEM, c * V7X_MXU_DIM:(c + 1) * V7X_MXU_DIM]
        m = jnp.max(s, axis=0, keepdims=True)
        p = jnp.exp2(s - m)
        mem_inv_l[c, h] = 1.0 / jnp.sum(p, axis=0, keepdims=True)
        mem_p[c, h] = p.astype(BF16)

    def finish_rows(c):
        rows = slice(c * V7X_MXU_DIM, (c + 1) * V7X_MXU_DIM)
        blocks = range(c * V7X_MXU_DIM // BLOCK, (c + 1) * V7X_MXU_DIM // BLOCK)
        attn_rows = jnp.concatenate(
            [jnp.concatenate([row for g in range(N_KV_HEADS) for row in head_rows[b, g]], axis=0)
             for b in blocks], axis=1).T
        p_all = jnp.concatenate([mem_p[c, h] for h in range(N_MEM_HEADS)], axis=0)
        o_t = jnp.dot(vblkt, p_all, preferred_element_type=F32)
        mem_rows = jnp.concatenate(
            [o_t[h * HEAD_DIM:(h + 1) * HEAD_DIM] * mem_inv_l[c, h] for h in range(N_MEM_HEADS)],
            axis=0).T
        y = late["y_conv"][rows] + jnp.dot(_rms(attn_rows, ona_ref[...]).astype(BF16), wout_ref[0:ATTN_WIDTH, :],
                                   preferred_element_type=F32)
        y = y + jnp.dot(_rms(mem_rows, onm_ref[...]).astype(BF16), wout_ref[ATTN_WIDTH + CONV_WIDTH:, :],
                        preferred_element_type=F32)
        out_ref[0, rows, :] = xres_ref[slot, rows, :] + y

    for piece in norms:
        piece()
    head_rows = {}
    chains_per_half = len(chains) // n_halves
    assert len(mem_units) == len(chains)
    for i, (b, g) in enumerate(chains):
        heads = range(g * GQA_GROUP, (g + 1) * GQA_GROUP)
        dist_b = dist_first if b == 0 else dist
        neg_slope = jnp.concatenate(
            [jnp.full((1, BLOCK), -(2.0 ** (-8.0 * (h + 1) / N_ATTN_HEADS)) * LOG2E, F32) for h in heads], axis=1)
        sink = sinks_ref[:, g * GQA_GROUP * BLOCK:(g + 1) * GQA_GROUP * BLOCK] * LOG2E
        if i + SCORE_LOOKAHEAD < len(chains):
            score_chain(*chains[i + SCORE_LOOKAHEAD])
        s2 = scores.pop((b, g))
        s = jnp.where(from_prev, s2[:BLOCK], s2[BLOCK:]) + dist_b * neg_slope
        m = jnp.maximum(jnp.max(s, axis=0, keepdims=True), sink)
        p = jnp.exp2(s - m)
        l = jnp.sum(p, axis=0, keepdims=True) + jnp.exp2(sink - m)
        p_t = jnp.concatenate(
            [jnp.where(from_prev, p, 0.0).astype(BF16), jnp.where(from_prev, 0.0, p).astype(BF16)],
            axis=0)
        for piece in chain_fill[i]:
            piece()
        o_t = jnp.dot(vt_cat[b, g], p_t, preferred_element_type=F32) * (1.0 / l)
        head_rows[b, g] = [o_t[:, hh * BLOCK:(hh + 1) * BLOCK] for hh in range(GQA_GROUP)]
        mem_softmax(*mem_units[i])
        if i == chains_per_half:
            finish_rows(0)
    for piece in tail_dots:
        piece()
    finish_rows(1)
    for piece in tail_posts:
        piece()


def _ffn_kernel(x_ref, gain_ref, wg_ref, wu_ref, wd_ref, out_ref):
    x = x_ref[...]
    piece_rows = V7X_MXU_DIM
    h_pieces = [_rms(x[r:r + piece_rows], gain_ref[...]).astype(BF16)
                for r in range(0, x.shape[0], piece_rows)]
    h = jnp.concatenate(h_pieces, axis=0)
    y = x
    for c in range(wg_ref.shape[1] // V7X_MXU_DIM):
        cols = slice(c * V7X_MXU_DIM, (c + 1) * V7X_MXU_DIM)
        if c == 0:
            gate = jnp.concatenate(
                [jnp.dot(hp, wg_ref[:, cols], preferred_element_type=F32) for hp in h_pieces], axis=0)
            up = jnp.concatenate(
                [jnp.dot(hp, wu_ref[:, cols], preferred_element_type=F32) for hp in h_pieces], axis=0)
        else:
            gate = jnp.dot(h, wg_ref[:, cols], preferred_element_type=F32)
            up = jnp.dot(h, wu_ref[:, cols], preferred_element_type=F32)
        act = (gate * jax.nn.sigmoid(gate) * up).astype(BF16)
        y = y + jnp.dot(act, wd_ref[cols, :], preferred_element_type=F32)
    out_ref[...] = y


def _const_spec(shape):
    return pl.BlockSpec(shape, lambda *_: (0,) * len(shape), pipeline_mode=pl.Buffered(1))


def _row(a):
    return a.reshape(1, -1)


def _head_column(gain, heads):
    return jnp.broadcast_to(jnp.tile(gain, heads)[:, None], (heads * HEAD_DIM, BLOCK))


def _layer(x, mem_blocks, lp, plan):
    seq_tile, ffn_tile, vmem_limit = plan
    batch, seq, _ = x.shape
    kblk, vblkt, w_in16, w_out16 = mem_blocks
    d_ff = lp["w_gate"].shape[1]

    tiles_per_seq = seq // seq_tile
    n_tiles = batch * tiles_per_seq
    slab_spec = pl.BlockSpec((D_MODEL // n_tiles, d_ff), lambda t: (t, 0))
    down_slabs = d_ff // BLOCK
    assert down_slabs <= n_tiles
    down_spec = pl.BlockSpec((BLOCK, D_MODEL), lambda t: (jnp.minimum(t, down_slabs - 1), 0))
    up_w16 = jax.ShapeDtypeStruct((D_MODEL, d_ff), BF16)
    down_w16 = jax.ShapeDtypeStruct((d_ff, D_MODEL), BF16)

    def tile_index(t):
        return (t // tiles_per_seq, t % tiles_per_seq, 0)

    mixer = pl.pallas_call(
        functools.partial(_mixer_kernel, seq_tile, tiles_per_seq),
        out_shape=(jax.ShapeDtypeStruct(x.shape, F32), up_w16, up_w16, down_w16),
        grid=(n_tiles,),
        in_specs=[
            _const_spec((1, seq_tile, D_MODEL)),
            pl.BlockSpec((1, seq_tile, D_MODEL), lambda t: tile_index(jnp.minimum(t + 1, n_tiles - 1))),
            pl.BlockSpec((1, N_MEM_HEADS * N_MEM, MEM_WIDTH), lambda t: (t // tiles_per_seq, 0, 0)),
            pl.BlockSpec((1, MEM_WIDTH, N_MEM_HEADS * N_MEM), lambda t: (t // tiles_per_seq, 0, 0)),
            _const_spec((D_MODEL, IN_PROJ_WIDTH)),
            _const_spec((D_MODEL, D_MODEL)),
            _const_spec((V7X_MXU_DIM, V7X_MXU_DIM)),
            _const_spec((1, D_MODEL)),
            _const_spec((ATTN_WIDTH, BLOCK)),
            _const_spec((1, KV_WIDTH)),
            _const_spec((MEM_WIDTH, BLOCK)),
            _const_spec((1, N_ATTN_HEADS * BLOCK)),
            _const_spec((CONV_K, CONV_WIDTH)),
            _const_spec((1, CONV_WIDTH)),
            _const_spec((1, ATTN_WIDTH)),
            _const_spec((1, CONV_WIDTH)),
            _const_spec((1, MEM_WIDTH)),
            slab_spec, slab_spec, down_spec,
        ],
        out_specs=(pl.BlockSpec((1, seq_tile, D_MODEL), tile_index), slab_spec, slab_spec, down_spec),
        scratch_shapes=[
            pltpu.VMEM((2, ATTN_WIDTH, seq_tile), BF16),
            pltpu.VMEM((2, MEM_WIDTH, seq_tile), BF16),
            pltpu.VMEM((2, seq_tile + BLOCK, KV_WIDTH), BF16),
            pltpu.VMEM((2, KV_WIDTH, seq_tile + BLOCK), BF16),
            pltpu.VMEM((2, seq_tile + V7X_SUBLANES, CONV_WIDTH), F32),
            pltpu.VMEM((2, seq_tile, CONV_WIDTH), F32),
            pltpu.VMEM((2, seq_tile, D_MODEL), F32),
        ],
        compiler_params=pltpu.CompilerParams(
            dimension_semantics=("arbitrary",), vmem_limit_bytes=vmem_limit),
        name="mixer",
    )
    x1, w_gate16, w_up16, w_down16 = mixer(
        x, x, kblk, vblkt, w_in16, w_out16, lp["gmat"],
        _row(lp["norm_mix"]), _head_column(lp["q_norm"], N_ATTN_HEADS),
        jnp.tile(_row(lp["k_norm"]), (1, N_KV_HEADS)), _head_column(lp["mem_q_norm"], N_MEM_HEADS),
        _row(jnp.repeat(lp["attn_sinks"], BLOCK)),
        lp["conv_w"], _row(lp["conv_b"]),
        _row(lp["out_norm_attn"]), _row(lp["out_norm_conv"]), _row(lp["out_norm_mem"]),
        lp["w_gate"], lp["w_up"], lp["w_down"])

    tokens = batch * seq
    ffn = pl.pallas_call(
        _ffn_kernel,
        out_shape=jax.ShapeDtypeStruct((tokens, D_MODEL), F32),
        grid=(tokens // ffn_tile,),
        in_specs=[
            pl.BlockSpec((ffn_tile, D_MODEL), lambda i: (i, 0)),
            _const_spec((1, D_MODEL)),
            _const_spec((D_MODEL, d_ff)),
            _const_spec((D_MODEL, d_ff)),
            _const_spec((d_ff, D_MODEL)),
        ],
        out_specs=pl.BlockSpec((ffn_tile, D_MODEL), lambda i: (i, 0)),
        compiler_params=pltpu.CompilerParams(
            dimension_semantics=("arbitrary",), vmem_limit_bytes=vmem_limit),
        name="ffn",
    )
    y = ffn(x1.reshape(tokens, D_MODEL), _row(lp["norm_ffn"]), w_gate16, w_up16, w_down16)
    return y.reshape(x.shape)


def _mem_kv(mem, lp, vmem_limit):
    batch = mem.shape[0]
    seqs_per_step = 4
    steps = batch // seqs_per_step
    kblk_shape = (batch, N_MEM_HEADS * N_MEM, MEM_WIDTH)
    vblkt_shape = (batch, MEM_WIDTH, N_MEM_HEADS * N_MEM)
    slab = D_MODEL // steps
    win_spec = pl.BlockSpec((slab, IN_PROJ_WIDTH), lambda b: (b, 0))
    wout_spec = pl.BlockSpec((slab, D_MODEL), lambda b: (b, 0))
    call = pl.pallas_call(
        _mem_kv_kernel,
        out_shape=(jax.ShapeDtypeStruct(kblk_shape, BF16), jax.ShapeDtypeStruct(vblkt_shape, BF16),
                   jax.ShapeDtypeStruct((D_MODEL, IN_PROJ_WIDTH), BF16),
                   jax.ShapeDtypeStruct((D_MODEL, D_MODEL), BF16)),
        grid=(steps,),
        in_specs=[
            pl.BlockSpec((seqs_per_step, N_MEM, D_MODEL), lambda b: (b, 0, 0)),
            _const_spec((1, D_MODEL)),
            _const_spec((D_MODEL, 2 * MEM_WIDTH)),
            _const_spec((1, MEM_WIDTH)),
            _const_spec((V7X_MXU_DIM, V7X_MXU_DIM)),
            win_spec, wout_spec,
        ],
        out_specs=(pl.BlockSpec((seqs_per_step,) + kblk_shape[1:], lambda b: (b, 0, 0)),
                   pl.BlockSpec((seqs_per_step,) + vblkt_shape[1:], lambda b: (b, 0, 0)),
                   win_spec, wout_spec),
        compiler_params=pltpu.CompilerParams(
            dimension_semantics=("arbitrary",), vmem_limit_bytes=vmem_limit),
        name="mem_kv",
    )
    return call(mem, _row(lp["norm_mem"]), lp["w_mem_kv"],
                jnp.tile(_row(lp["mem_k_norm"]), (1, N_MEM_HEADS)), lp["gmat"], lp["w_in"], lp["w_out"])


def kernel(x, mem, norm_mix, w_in, q_norm, k_norm, attn_sinks, conv_w, conv_b, norm_mem, w_mem_kv,
           mem_q_norm, mem_k_norm, out_norm_attn, out_norm_conv, out_norm_mem, w_out, norm_ffn,
           w_gate, w_up, w_down):
    plan = _plan()
    head_of_lane = np.arange(V7X_MXU_DIM) // HEAD_DIM
    gmat = jnp.asarray(head_of_lane[:, None] == head_of_lane[None, :], dtype=BF16)
    depth = w_in.shape[0]
    for l in range(depth):
        lp = dict(
            norm_mix=norm_mix[l], w_in=w_in[l], q_norm=q_norm[l], k_norm=k_norm[l],
            attn_sinks=attn_sinks[l], conv_w=conv_w[l], conv_b=conv_b[l], norm_mem=norm_mem[l],
            w_mem_kv=w_mem_kv[l], mem_q_norm=mem_q_norm[l], mem_k_norm=mem_k_norm[l],
            out_norm_attn=out_norm_attn[l], out_norm_conv=out_norm_conv[l],
            out_norm_mem=out_norm_mem[l], w_out=w_out[l], norm_ffn=norm_ffn[l],
            w_gate=w_gate[l], w_up=w_up[l], w_down=w_down[l],
            gmat=gmat)
        x = _layer(x, _mem_kv(mem, lp, plan[2]), lp, plan)
    return x
```

```python
import functools

import jax
import jax.numpy as jnp
from jax import lax
from jax.experimental import pallas as pl
from jax.experimental.pallas import tpu as pltpu

D_MODEL = 1024
HEAD_DIM = 64
N_ATTN_HEADS = 8
N_KV_HEADS = 2
GQA_GROUP = N_ATTN_HEADS // N_KV_HEADS
BLOCK = 128
N_MEM_HEADS = 4
N_MEM = 256
CONV_K = 3
ATTN_WIDTH = N_ATTN_HEADS * HEAD_DIM
KV_WIDTH = N_KV_HEADS * HEAD_DIM
CONV_WIDTH = 256
MEM_WIDTH = N_MEM_HEADS * HEAD_DIM
IN_PROJ_WIDTH = ATTN_WIDTH + 2 * KV_WIDTH + 3 * CONV_WIDTH + MEM_WIDTH
EPS = 1e-6
LOG2E = 1.4426950408889634
MASKED_DIST = 2.0 ** 110

V7X_VMEM_BYTES = 64 * 1024 * 1024
V7X_SUBLANES = 8
V7X_LANES = 128
V7X_MXU_DIM = 256

F32 = jnp.float32
BF16 = jnp.bfloat16


def _plan():
    seq_tile = 4 * BLOCK
    ffn_tile = 1024
    vmem_limit = V7X_VMEM_BYTES - 8 * 1024 * 1024
    return seq_tile, ffn_tile, vmem_limit


def _rms(a, gain):
    return a * lax.rsqrt(jnp.mean(a * a, axis=-1, keepdims=True) + EPS) * gain


def _head_rms_scale(t, gmat):
    sq = (t * t).astype(BF16)
    ss = jnp.dot(sq, gmat, preferred_element_type=F32)
    return lax.rsqrt(ss * (1.0 / HEAD_DIM) + EPS)


def _head_block_ones():
    row = lax.broadcasted_iota(jnp.int32, (V7X_MXU_DIM, V7X_MXU_DIM), 0) // HEAD_DIM
    col = lax.broadcasted_iota(jnp.int32, (V7X_MXU_DIM, V7X_MXU_DIM), 1) // HEAD_DIM
    return jnp.where(row == col, 1.0, 0.0).astype(BF16)


def _head_gain_column(gain_row, heads):
    two_heads = jnp.concatenate([gain_row, gain_row], axis=1)
    col = jnp.broadcast_to(two_heads, (BLOCK, BLOCK)).T
    return jnp.concatenate([col] * (heads * HEAD_DIM // BLOCK), axis=0)


def _head_rms_rows(t, gain):
    heads = t.shape[0] // HEAD_DIM
    out = []
    for h in range(heads):
        th = t[h * HEAD_DIM:(h + 1) * HEAD_DIM]
        ss = jnp.sum(th * th, axis=0, keepdims=True)
        out.append(th * lax.rsqrt(ss * (1.0 / HEAD_DIM) + EPS) * gain[h * HEAD_DIM:(h + 1) * HEAD_DIM])
    return out


def _mem_kv_kernel(mem_ref, gain_ref, w_ref, kgain_ref, kblk_ref, vt_ref):
    n_seq = mem_ref.shape[0]
    m = mem_ref[...].reshape(n_seq * N_MEM, D_MODEL)
    mn = _rms(m, gain_ref[...]).astype(BF16)
    kv = jnp.dot(mn, w_ref[...].astype(BF16), preferred_element_type=F32)
    k = kv[:, :MEM_WIDTH]
    kgain = jnp.concatenate([kgain_ref[...]] * N_MEM_HEADS, axis=1)
    kn = k * _head_rms_scale(k, _head_block_ones()) * kgain
    lane_head = lax.broadcasted_iota(jnp.int32, (N_MEM, MEM_WIDTH), 1) // HEAD_DIM
    for s in range(n_seq):
        rows = slice(s * N_MEM, (s + 1) * N_MEM)
        kn_s = kn[rows]
        vt_ref[s] = kv[rows, MEM_WIDTH:].T.astype(BF16)
        for h in range(N_MEM_HEADS):
            kblk_ref[s, h * N_MEM:(h + 1) * N_MEM, :] = jnp.where(lane_head == h, kn_s, 0.0).astype(BF16)


ROW_CHUNK = V7X_MXU_DIM
COL_GROUP = 2 * V7X_MXU_DIM
SCORE_LOOKAHEAD = 5
STAGE1_HELD_PIECES = 2
CONV_PROJ_CHAIN = 3


def _stage1_pieces(x_ref, wslot, carry, refs):
    (win_ref, nmix_ref, qgain_ref, kgain_ref, mqgain_ref,
     qnt_ref, qmnt_ref, kpad_ref, vtpad_ref, upad_ref, cb_ref, xres_ref) = refs
    seq_tile = cb_ref.shape[1]
    xn, val = {}, {}
    conv_base = ATTN_WIDTH + 2 * KV_WIDTH

    def carried(ref_slice_fn, shape, dtype):
        if carry is None:
            return jnp.zeros(shape, dtype)
        prev_slot, first_of_seq = carry
        return jnp.where(first_of_seq, jnp.zeros(shape, dtype), ref_slice_fn(prev_slot))

    def heads_t(t, gain_ref, heads, out_ref, r):
        t = t.T
        gain = _head_gain_column(gain_ref[...] * (HEAD_DIM ** -0.5 * LOG2E), heads)
        gain = jnp.concatenate([gain] * (ROW_CHUNK // BLOCK), axis=1)
        out_ref[wslot, :, r * ROW_CHUNK:(r + 1) * ROW_CHUNK] = jnp.concatenate(
            _head_rms_rows(t, gain), axis=0).astype(BF16)

    norms, dots, posts, narrow_dots, narrow_posts = [], [], [], [], []
    for r in range(seq_tile // ROW_CHUNK):
        rows = slice(r * ROW_CHUNK, (r + 1) * ROW_CHUNK)

        def norm(r=r, rows=rows):
            x = x_ref[0, rows, :]
            xres_ref[wslot, rows, :] = x
            xn[r] = _rms(x, nmix_ref[...]).astype(BF16)

        def dot_piece(name, start, width, r=r):
            def run():
                val[name, r] = jnp.dot(xn[r], win_ref[:, start:start + width], preferred_element_type=F32)
            return run

        def post_q(r=r):
            heads_t(val.pop(("q", r)), qgain_ref, N_ATTN_HEADS, qnt_ref, r)

        def post_kv(r=r):
            t = val.pop(("kvch", r))
            kv_a = t[:, :2 * KV_WIDTH]
            val["ch", r] = t[:, 2 * KV_WIDTH:]
            scale = _head_rms_scale(kv_a, _head_block_ones())
            kgain = jnp.concatenate([kgain_ref[...]] * N_KV_HEADS, axis=1)
            kn = kv_a[:, :KV_WIDTH] * scale[:, :KV_WIDTH] * kgain
            kpad_ref[wslot, BLOCK + r * ROW_CHUNK:BLOCK + (r + 1) * ROW_CHUNK, :] = kn.astype(BF16)
            vtpad_ref[wslot, :, BLOCK + r * ROW_CHUNK:BLOCK + (r + 1) * ROW_CHUNK] = (
                kv_a[:, KV_WIDTH:].T.astype(BF16))
            if r == 0:
                kpad_ref[wslot, 0:BLOCK, :] = carried(
                    lambda s: kpad_ref[s, seq_tile:seq_tile + BLOCK, :], (BLOCK, KV_WIDTH), BF16)
                vtpad_ref[wslot, :, 0:BLOCK] = carried(
                    lambda s: vtpad_ref[s, :, seq_tile:seq_tile + BLOCK], (KV_WIDTH, BLOCK), BF16)

        def post_conv(r=r):
            t = val.pop(("cbcc", r))
            cb_ref[wslot, r * ROW_CHUNK:(r + 1) * ROW_CHUNK, :] = t[:, :CONV_WIDTH]
            u = t[:, CONV_WIDTH:] * val.pop(("ch", r))
            upad_ref[wslot, V7X_SUBLANES + r * ROW_CHUNK:V7X_SUBLANES + (r + 1) * ROW_CHUNK, :] = u
            if r == 0:
                upad_ref[wslot, 0:V7X_SUBLANES, :] = carried(
                    lambda s: upad_ref[s, seq_tile:seq_tile + V7X_SUBLANES, :],
                    (V7X_SUBLANES, CONV_WIDTH), F32)

        def post_qm(r=r):
            heads_t(val.pop(("qm", r)), mqgain_ref, N_MEM_HEADS, qmnt_ref, r)

        norms.append(norm)
        dots += [dot_piece("q", 0, COL_GROUP), dot_piece("kvch", ATTN_WIDTH, COL_GROUP),
                 dot_piece("cbcc", conv_base + CONV_WIDTH, COL_GROUP)]
        posts += [post_q, post_kv, post_conv]
        narrow_dots.append(dot_piece("qm", IN_PROJ_WIDTH - MEM_WIDTH, MEM_WIDTH))
        narrow_posts.append(post_qm)
    return norms, dots + narrow_dots, posts + narrow_posts


def _mixer_kernel(seq_tile, tiles_per_seq,
                  sinks_ref, xfirst_ref, xnext_ref, kblk_ref, vt_ref, win32_ref, wout32_ref,
                  nmix_ref, qgain_ref, kgain_ref, mqgain_ref, convw_ref, convb_ref,
                  ona_ref, onc_ref, onm_ref, wg_ref, wu_ref, wd_ref,
                  out_ref, wg16_ref, wu16_ref, wd16_ref,
                  qnt_ref, qmnt_ref, kpad_ref, vtpad_ref, upad_ref, cb_ref, xres_ref, win_ref, wout_ref):
    wg16_ref[...] = wg_ref[...].astype(BF16)
    wu16_ref[...] = wu_ref[...].astype(BF16)
    wd16_ref[...] = wd_ref[...].astype(BF16)
    t = pl.program_id(0)
    slot = t % 2
    n_blocks = seq_tile // BLOCK
    first_of_seq = (t % tiles_per_seq) == 0
    stage1_refs = (win_ref, nmix_ref, qgain_ref, kgain_ref, mqgain_ref,
                   qnt_ref, qmnt_ref, kpad_ref, vtpad_ref, upad_ref, cb_ref, xres_ref)

    @pl.when(t == 0)
    def _():
        win_ref[...] = win32_ref[...].astype(BF16)
        wout_ref[...] = wout32_ref[...].astype(BF16)
        norms, dots, posts = _stage1_pieces(xfirst_ref, 0, None, stage1_refs)
        for piece in norms + [p for pair in zip(dots, posts) for p in pair]:
            piece()

    norms, dots, posts = _stage1_pieces(
        xnext_ref, 1 - slot, (slot, ((t + 1) % tiles_per_seq) == 0), stage1_refs)
    n_chains = n_blocks * N_KV_HEADS
    n_early = len(dots) - STAGE1_HELD_PIECES
    assert n_early < n_chains
    chain_fill = [[] for _ in range(n_chains)]
    chain_fill[0] = [dots[0], dots[1]]
    for i in range(1, n_early + 1):
        chain_fill[i] = [posts[i - 1]] + ([dots[i + 1]] if i + 1 < n_early else [])
    tail_dots, tail_posts = dots[n_early:], posts[n_early:]

    key = lax.broadcasted_iota(jnp.int32, (BLOCK, GQA_GROUP * BLOCK), 0)
    qry = lax.broadcasted_iota(jnp.int32, (BLOCK, GQA_GROUP * BLOCK), 1) % BLOCK
    from_prev = key > qry
    dist = jnp.where(from_prev, qry + BLOCK - key, qry - key).astype(F32)
    dist_first = jnp.where(jnp.logical_and(from_prev, first_of_seq), MASKED_DIST, dist)
    zeros_q = jnp.zeros((HEAD_DIM, GQA_GROUP * BLOCK), BF16)

    chains = [(b, g) for b in range(n_blocks) for g in range(N_KV_HEADS)]
    vt_cat = {(b, g): vtpad_ref[slot, g * HEAD_DIM:(g + 1) * HEAD_DIM, b * BLOCK:(b + 2) * BLOCK]
              for b, g in chains}
    qmn = qmnt_ref[slot]
    n_halves = seq_tile // V7X_MXU_DIM
    u = upad_ref[slot, V7X_SUBLANES:V7X_SUBLANES + seq_tile, :]
    u1 = upad_ref[slot, V7X_SUBLANES - 1:V7X_SUBLANES - 1 + seq_tile, :]
    u2 = upad_ref[slot, V7X_SUBLANES - 2:V7X_SUBLANES - 2 + seq_tile, :]
    cw = convw_ref[...]
    conv = cw[0:1] * u2 + cw[1:2] * u1 + cw[2:3] * u + convb_ref[...]
    conv_n = _rms(cb_ref[slot] * conv, onc_ref[...])

    scores, score_ops = {}, {}
    for b, g in chains:
        cols = slice(b * BLOCK, (b + 1) * BLOCK)
        k_cat = kpad_ref[slot, b * BLOCK:(b + 2) * BLOCK, :]
        heads = range(g * GQA_GROUP, (g + 1) * GQA_GROUP)
        q4 = jnp.concatenate(
            [qnt_ref[slot, h * HEAD_DIM:(h + 1) * HEAD_DIM, cols] for h in heads], axis=1)
        score_ops[b, g] = (k_cat, jnp.concatenate([q4, zeros_q] if g == 0 else [zeros_q, q4], axis=0))

    def score_chain(b, g):
        k_cat, w_q = score_ops.pop((b, g))
        scores[b, g] = jnp.dot(k_cat, w_q, preferred_element_type=F32)

    for b, g in chains[:SCORE_LOOKAHEAD]:
        score_chain(b, g)
    late = {}

    def conv_proj():
        late["y_conv"] = jnp.dot(conv_n.astype(BF16), wout_ref[ATTN_WIDTH:ATTN_WIDTH + CONV_WIDTH, :],
                                 preferred_element_type=F32)

    chain_fill[CONV_PROJ_CHAIN].insert(0, conv_proj)
    kblk = kblk_ref[0]
    mem_vt = vt_ref[0]
    mem_scores = jnp.dot(kblk, qmn, preferred_element_type=F32)
    mem_units = [(c, h) for c in range(n_halves) for h in range(N_MEM_HEADS)]
    mem_o = {}

    def mem_softmax(c, h):
        s = mem_scores[h * N_MEM:(h + 1) * N_MEM, c * V7X_MXU_DIM:(c + 1) * V7X_MXU_DIM]
        m = jnp.max(s, axis=0, keepdims=True)
        p = jnp.exp2(s - m)
        inv_l = 1.0 / jnp.sum(p, axis=0, keepdims=True)
        vt_h = mem_vt[h * HEAD_DIM:(h + 1) * HEAD_DIM]
        mem_o[c, h] = jnp.dot(vt_h, p.astype(BF16), preferred_element_type=F32) * inv_l

    def finish_rows(c):
        rows = slice(c * V7X_MXU_DIM, (c + 1) * V7X_MXU_DIM)
        blocks = range(c * V7X_MXU_DIM // BLOCK, (c + 1) * V7X_MXU_DIM // BLOCK)
        attn_rows = jnp.concatenate(
            [jnp.concatenate([row for g in range(N_KV_HEADS) for row in head_rows[b, g]], axis=0)
             for b in blocks], axis=1).T
        mem_rows = jnp.concatenate([mem_o[c, h] for h in range(N_MEM_HEADS)], axis=0).T
        y = late["y_conv"][rows] + jnp.dot(_rms(attn_rows, ona_ref[...]).astype(BF16), wout_ref[0:ATTN_WIDTH, :],
                                   preferred_element_type=F32)
        y = y + jnp.dot(_rms(mem_rows, onm_ref[...]).astype(BF16), wout_ref[ATTN_WIDTH + CONV_WIDTH:, :],
                        preferred_element_type=F32)
        out_ref[0, rows, :] = xres_ref[slot, rows, :] + y

    for piece in norms:
        piece()
    head_rows = {}
    chains_per_half = len(chains) // n_halves
    assert len(mem_units) == len(chains)
    for i, (b, g) in enumerate(chains):
        heads = range(g * GQA_GROUP, (g + 1) * GQA_GROUP)
        dist_b = dist_first if b == 0 else dist
        neg_slope = jnp.concatenate(
            [jnp.full((1, BLOCK), -(2.0 ** (-8.0 * (h + 1) / N_ATTN_HEADS)) * LOG2E, F32) for h in heads], axis=1)
        sink = jnp.concatenate([jnp.full((1, BLOCK), sinks_ref[h], F32) for h in heads], axis=1) * LOG2E
        if i + SCORE_LOOKAHEAD < len(chains):
            score_chain(*chains[i + SCORE_LOOKAHEAD])
        s2 = scores.pop((b, g))
        s = jnp.where(from_prev, s2[:BLOCK], s2[BLOCK:]) + dist_b * neg_slope
        m = jnp.maximum(jnp.max(s, axis=0, keepdims=True), sink)
        p = jnp.exp2(s - m)
        l = jnp.sum(p, axis=0, keepdims=True) + jnp.exp2(sink - m)
        p_t = jnp.concatenate(
            [jnp.where(from_prev, p, 0.0).astype(BF16), jnp.where(from_prev, 0.0, p).astype(BF16)],
            axis=0)
        for piece in chain_fill[i]:
            piece()
        o_t = jnp.dot(vt_cat[b, g], p_t, preferred_element_type=F32) * (1.0 / l)
        head_rows[b, g] = [o_t[:, hh * BLOCK:(hh + 1) * BLOCK] for hh in range(GQA_GROUP)]
        mem_softmax(*mem_units[i])
        if i == chains_per_half:
            finish_rows(0)
    for piece in tail_dots:
        piece()
    finish_rows(1)
    for piece in tail_posts:
        piece()


def _ffn_kernel(x_ref, gain_ref, wg_ref, wu_ref, wd_ref, out_ref):
    x = x_ref[...]
    piece_rows = V7X_MXU_DIM
    h_pieces = [_rms(x[r:r + piece_rows], gain_ref[...]).astype(BF16)
                for r in range(0, x.shape[0], piece_rows)]
    h = jnp.concatenate(h_pieces, axis=0)
    y = x
    for c in range(wg_ref.shape[1] // V7X_MXU_DIM):
        cols = slice(c * V7X_MXU_DIM, (c + 1) * V7X_MXU_DIM)
        if c == 0:
            gate = jnp.concatenate(
                [jnp.dot(hp, wg_ref[:, cols], preferred_element_type=F32) for hp in h_pieces], axis=0)
            up = jnp.concatenate(
                [jnp.dot(hp, wu_ref[:, cols], preferred_element_type=F32) for hp in h_pieces], axis=0)
        else:
            gate = jnp.dot(h, wg_ref[:, cols], preferred_element_type=F32)
            up = jnp.dot(h, wu_ref[:, cols], preferred_element_type=F32)
        act = (gate * jax.nn.sigmoid(gate) * up).astype(BF16)
        y = y + jnp.dot(act, wd_ref[cols, :], preferred_element_type=F32)
    out_ref[...] = y


def _const_spec(shape):
    return pl.BlockSpec(shape, lambda *_: (0,) * len(shape), pipeline_mode=pl.Buffered(1))


def _row(a):
    return a.reshape(1, -1)


def _layer(x, mem_blocks, lp, plan):
    seq_tile, ffn_tile, vmem_limit = plan
    batch, seq, _ = x.shape
    kblk, mem_vt = mem_blocks
    d_ff = lp["w_gate"].shape[1]

    tiles_per_seq = seq // seq_tile
    n_tiles = batch * tiles_per_seq
    slab_spec = pl.BlockSpec((D_MODEL // n_tiles, d_ff), lambda t: (t, 0))
    down_slabs = d_ff // BLOCK
    assert down_slabs <= n_tiles
    down_spec = pl.BlockSpec((BLOCK, D_MODEL), lambda t: (jnp.minimum(t, down_slabs - 1), 0))
    up_w16 = jax.ShapeDtypeStruct((D_MODEL, d_ff), BF16)
    down_w16 = jax.ShapeDtypeStruct((d_ff, D_MODEL), BF16)

    def tile_index(t):
        return (t // tiles_per_seq, t % tiles_per_seq, 0)

    mixer = pl.pallas_call(
        functools.partial(_mixer_kernel, seq_tile, tiles_per_seq),
        out_shape=(jax.ShapeDtypeStruct(x.shape, F32), up_w16, up_w16, down_w16),
        grid=(n_tiles,),
        in_specs=[
            pl.BlockSpec(memory_space=pltpu.SMEM),
            _const_spec((1, seq_tile, D_MODEL)),
            pl.BlockSpec((1, seq_tile, D_MODEL), lambda t: tile_index(jnp.minimum(t + 1, n_tiles - 1))),
            pl.BlockSpec((1, N_MEM_HEADS * N_MEM, MEM_WIDTH), lambda t: (t // tiles_per_seq, 0, 0)),
            pl.BlockSpec((1, MEM_WIDTH, N_MEM), lambda t: (t // tiles_per_seq, 0, 0)),
            _const_spec((D_MODEL, IN_PROJ_WIDTH)),
            _const_spec((D_MODEL, D_MODEL)),
            _const_spec((1, D_MODEL)),
            _const_spec((1, HEAD_DIM)),
            _const_spec((1, HEAD_DIM)),
            _const_spec((1, HEAD_DIM)),
            _const_spec((CONV_K, CONV_WIDTH)),
            _const_spec((1, CONV_WIDTH)),
            _const_spec((1, ATTN_WIDTH)),
            _const_spec((1, CONV_WIDTH)),
            _const_spec((1, MEM_WIDTH)),
            slab_spec, slab_spec, down_spec,
        ],
        out_specs=(pl.BlockSpec((1, seq_tile, D_MODEL), tile_index), slab_spec, slab_spec, down_spec),
        scratch_shapes=[
            pltpu.VMEM((2, ATTN_WIDTH, seq_tile), BF16),
            pltpu.VMEM((2, MEM_WIDTH, seq_tile), BF16),
            pltpu.VMEM((2, seq_tile + BLOCK, KV_WIDTH), BF16),
            pltpu.VMEM((2, KV_WIDTH, seq_tile + BLOCK), BF16),
            pltpu.VMEM((2, seq_tile + V7X_SUBLANES, CONV_WIDTH), F32),
            pltpu.VMEM((2, seq_tile, CONV_WIDTH), F32),
            pltpu.VMEM((2, seq_tile, D_MODEL), F32),
            pltpu.VMEM((D_MODEL, IN_PROJ_WIDTH), BF16),
            pltpu.VMEM((D_MODEL, D_MODEL), BF16),
        ],
        compiler_params=pltpu.CompilerParams(
            dimension_semantics=("arbitrary",), vmem_limit_bytes=vmem_limit),
        name="mixer",
    )
    x1, w_gate16, w_up16, w_down16 = mixer(
        lp["attn_sinks"], x, x, kblk, mem_vt, lp["w_in"], lp["w_out"],
        _row(lp["norm_mix"]), _row(lp["q_norm"]), _row(lp["k_norm"]), _row(lp["mem_q_norm"]),
        lp["conv_w"], _row(lp["conv_b"]),
        _row(lp["out_norm_attn"]), _row(lp["out_norm_conv"]), _row(lp["out_norm_mem"]),
        lp["w_gate"], lp["w_up"], lp["w_down"])

    tokens = batch * seq
    ffn = pl.pallas_call(
        _ffn_kernel,
        out_shape=jax.ShapeDtypeStruct((tokens, D_MODEL), F32),
        grid=(tokens // ffn_tile,),
        in_specs=[
            pl.BlockSpec((ffn_tile, D_MODEL), lambda i: (i, 0)),
            _const_spec((1, D_MODEL)),
            _const_spec((D_MODEL, d_ff)),
            _const_spec((D_MODEL, d_ff)),
            _const_spec((d_ff, D_MODEL)),
        ],
        out_specs=pl.BlockSpec((ffn_tile, D_MODEL), lambda i: (i, 0)),
        compiler_params=pltpu.CompilerParams(
            dimension_semantics=("arbitrary",), vmem_limit_bytes=vmem_limit),
        name="ffn",
    )
    y = ffn(x1.reshape(tokens, D_MODEL), _row(lp["norm_ffn"]), w_gate16, w_up16, w_down16)
    return y.reshape(x.shape)


def _mem_kv(mem, lp, vmem_limit):
    batch = mem.shape[0]
    seqs_per_step = 4
    steps = batch // seqs_per_step
    kblk_shape = (batch, N_MEM_HEADS * N_MEM, MEM_WIDTH)
    vt_shape = (batch, MEM_WIDTH, N_MEM)
    call = pl.pallas_call(
        _mem_kv_kernel,
        out_shape=(jax.ShapeDtypeStruct(kblk_shape, BF16), jax.ShapeDtypeStruct(vt_shape, BF16)),
        grid=(steps,),
        in_specs=[
            pl.BlockSpec((seqs_per_step, N_MEM, D_MODEL), lambda b: (b, 0, 0)),
            _const_spec((1, D_MODEL)),
            _const_spec((D_MODEL, 2 * MEM_WIDTH)),
            _const_spec((1, HEAD_DIM)),
        ],
        out_specs=(pl.BlockSpec((seqs_per_step,) + kblk_shape[1:], lambda b: (b, 0, 0)),
                   pl.BlockSpec((seqs_per_step,) + vt_shape[1:], lambda b: (b, 0, 0))),
        compiler_params=pltpu.CompilerParams(
            dimension_semantics=("arbitrary",), vmem_limit_bytes=vmem_limit),
        name="mem_kv",
    )
    return call(mem, _row(lp["norm_mem"]), lp["w_mem_kv"], _row(lp["mem_k_norm"]))


def kernel(x, mem, norm_mix, w_in, q_norm, k_norm, attn_sinks, conv_w, conv_b, norm_mem, w_mem_kv,
           mem_q_norm, mem_k_norm, out_norm_attn, out_norm_conv, out_norm_mem, w_out, norm_ffn,
           w_gate, w_up, w_down):
    plan = _plan()
    depth = w_in.shape[0]
    for l in range(depth):
        lp = dict(
            norm_mix=norm_mix[l], w_in=w_in[l], q_norm=q_norm[l], k_norm=k_norm[l],
            attn_sinks=attn_sinks[l], conv_w=conv_w[l], conv_b=conv_b[l], norm_mem=norm_mem[l],
            w_mem_kv=w_mem_kv[l], mem_q_norm=mem_q_norm[l], mem_k_norm=mem_k_norm[l],
            out_norm_attn=out_norm_attn[l], out_norm_conv=out_norm_conv[l],
            out_norm_mem=out_norm_mem[l], w_out=w_out[l], norm_ffn=norm_ffn[l],
            w_gate=w_gate[l], w_up=w_up[l], w_down=w_down[l])
        x = _layer(x, _mem_kv(mem, lp, plan[2]), lp, plan)
    return x
```

```python
import functools

import jax
import jax.numpy as jnp
from jax import lax
from jax.experimental import pallas as pl
from jax.experimental.pallas import tpu as pltpu

D_MODEL = 1024
HEAD_DIM = 64
N_ATTN_HEADS = 8
N_KV_HEADS = 2
GQA_GROUP = N_ATTN_HEADS // N_KV_HEADS
BLOCK = 128
N_MEM_HEADS = 4
N_MEM = 256
CONV_K = 3
ATTN_WIDTH = N_ATTN_HEADS * HEAD_DIM
KV_WIDTH = N_KV_HEADS * HEAD_DIM
CONV_WIDTH = 256
MEM_WIDTH = N_MEM_HEADS * HEAD_DIM
IN_PROJ_WIDTH = ATTN_WIDTH + 2 * KV_WIDTH + 3 * CONV_WIDTH + MEM_WIDTH
EPS = 1e-6
LOG2E = 1.4426950408889634
MASKED_DIST = 2.0 ** 110

V7X_VMEM_BYTES = 64 * 1024 * 1024
V7X_SUBLANES = 8
V7X_LANES = 128
V7X_MXU_DIM = 256

F32 = jnp.float32
BF16 = jnp.bfloat16


def _plan():
    seq_tile = 4 * BLOCK
    ffn_tile = 1024
    vmem_limit = V7X_VMEM_BYTES - 8 * 1024 * 1024
    return seq_tile, ffn_tile, vmem_limit


def _rms(a, gain):
    return a * lax.rsqrt(jnp.mean(a * a, axis=-1, keepdims=True) + EPS) * gain


def _head_rms_scale(t, gmat):
    sq = (t * t).astype(BF16)
    ss = jnp.dot(sq, gmat, preferred_element_type=F32)
    return lax.rsqrt(ss * (1.0 / HEAD_DIM) + EPS)


def _head_block_ones():
    row = lax.broadcasted_iota(jnp.int32, (V7X_MXU_DIM, V7X_MXU_DIM), 0) // HEAD_DIM
    col = lax.broadcasted_iota(jnp.int32, (V7X_MXU_DIM, V7X_MXU_DIM), 1) // HEAD_DIM
    return jnp.where(row == col, 1.0, 0.0).astype(BF16)


def _head_gain_column(gain_row, heads):
    two_heads = jnp.concatenate([gain_row, gain_row], axis=1)
    col = jnp.broadcast_to(two_heads, (BLOCK, BLOCK)).T
    return jnp.concatenate([col] * (heads * HEAD_DIM // BLOCK), axis=0)


def _head_rms_rows(t, gain):
    heads = t.shape[0] // HEAD_DIM
    out = []
    for h in range(heads):
        th = t[h * HEAD_DIM:(h + 1) * HEAD_DIM]
        ss = jnp.sum(th * th, axis=0, keepdims=True)
        out.append(th * lax.rsqrt(ss * (1.0 / HEAD_DIM) + EPS) * gain[h * HEAD_DIM:(h + 1) * HEAD_DIM])
    return out


def _mem_kv_kernel(mem_ref, gain_ref, w_ref, kgain_ref, kblk_ref, vt_ref):
    n_seq = mem_ref.shape[0]
    m = mem_ref[...].reshape(n_seq * N_MEM, D_MODEL)
    mn = _rms(m, gain_ref[...]).astype(BF16)
    kv = jnp.dot(mn, w_ref[...].astype(BF16), preferred_element_type=F32)
    k = kv[:, :MEM_WIDTH]
    kgain = jnp.concatenate([kgain_ref[...]] * N_MEM_HEADS, axis=1)
    kn = k * _head_rms_scale(k, _head_block_ones()) * kgain
    lane_head = lax.broadcasted_iota(jnp.int32, (N_MEM, MEM_WIDTH), 1) // HEAD_DIM
    for s in range(n_seq):
        rows = slice(s * N_MEM, (s + 1) * N_MEM)
        kn_s = kn[rows]
        vt_ref[s] = kv[rows, MEM_WIDTH:].T.astype(BF16)
        for h in range(N_MEM_HEADS):
            kblk_ref[s, h * N_MEM:(h + 1) * N_MEM, :] = jnp.where(lane_head == h, kn_s, 0.0).astype(BF16)


ROW_CHUNK = V7X_MXU_DIM
COL_GROUP = 2 * V7X_MXU_DIM
SCORE_LOOKAHEAD = 5
STAGE1_HELD_PIECES = 2
CONV_PROJ_CHAIN = 3


def _stage1_pieces(x_ref, wslot, carry, refs):
    (win_ref, nmix_ref, qgain_ref, kgain_ref, mqgain_ref,
     qnt_ref, qmnt_ref, kpad_ref, vtpad_ref, upad_ref, cb_ref, xres_ref) = refs
    seq_tile = cb_ref.shape[1]
    xn, val = {}, {}
    conv_base = ATTN_WIDTH + 2 * KV_WIDTH

    def carried(ref_slice_fn, shape, dtype):
        if carry is None:
            return jnp.zeros(shape, dtype)
        prev_slot, first_of_seq = carry
        return jnp.where(first_of_seq, jnp.zeros(shape, dtype), ref_slice_fn(prev_slot))

    def heads_t(t, gain_ref, heads, out_ref, r):
        t = t.T
        gain = _head_gain_column(gain_ref[...] * (HEAD_DIM ** -0.5 * LOG2E), heads)
        gain = jnp.concatenate([gain] * (ROW_CHUNK // BLOCK), axis=1)
        out_ref[wslot, :, r * ROW_CHUNK:(r + 1) * ROW_CHUNK] = jnp.concatenate(
            _head_rms_rows(t, gain), axis=0).astype(BF16)

    norms, dots, posts, narrow_dots, narrow_posts = [], [], [], [], []
    for r in range(seq_tile // ROW_CHUNK):
        rows = slice(r * ROW_CHUNK, (r + 1) * ROW_CHUNK)

        def norm(r=r, rows=rows):
            x = x_ref[0, rows, :]
            xres_ref[wslot, rows, :] = x
            xn[r] = _rms(x, nmix_ref[...]).astype(BF16)

        def dot_piece(name, start, width, r=r):
            def run():
                val[name, r] = jnp.dot(xn[r], win_ref[:, start:start + width], preferred_element_type=F32)
            return run

        def post_q(r=r):
            heads_t(val.pop(("q", r)), qgain_ref, N_ATTN_HEADS, qnt_ref, r)

        def post_kv(r=r):
            t = val.pop(("kvch", r))
            kv_a = t[:, :2 * KV_WIDTH]
            val["ch", r] = t[:, 2 * KV_WIDTH:]
            scale = _head_rms_scale(kv_a, _head_block_ones())
            kgain = jnp.concatenate([kgain_ref[...]] * N_KV_HEADS, axis=1)
            kn = kv_a[:, :KV_WIDTH] * scale[:, :KV_WIDTH] * kgain
            kpad_ref[wslot, BLOCK + r * ROW_CHUNK:BLOCK + (r + 1) * ROW_CHUNK, :] = kn.astype(BF16)
            vtpad_ref[wslot, :, BLOCK + r * ROW_CHUNK:BLOCK + (r + 1) * ROW_CHUNK] = (
                kv_a[:, KV_WIDTH:].T.astype(BF16))
            if r == 0:
                kpad_ref[wslot, 0:BLOCK, :] = carried(
                    lambda s: kpad_ref[s, seq_tile:seq_tile + BLOCK, :], (BLOCK, KV_WIDTH), BF16)
                vtpad_ref[wslot, :, 0:BLOCK] = carried(
                    lambda s: vtpad_ref[s, :, seq_tile:seq_tile + BLOCK], (KV_WIDTH, BLOCK), BF16)

        def post_conv(r=r):
            t = val.pop(("cbcc", r))
            cb_ref[wslot, r * ROW_CHUNK:(r + 1) * ROW_CHUNK, :] = t[:, :CONV_WIDTH]
            u = t[:, CONV_WIDTH:] * val.pop(("ch", r))
            upad_ref[wslot, V7X_SUBLANES + r * ROW_CHUNK:V7X_SUBLANES + (r + 1) * ROW_CHUNK, :] = u
            if r == 0:
                upad_ref[wslot, 0:V7X_SUBLANES, :] = carried(
                    lambda s: upad_ref[s, seq_tile:seq_tile + V7X_SUBLANES, :],
                    (V7X_SUBLANES, CONV_WIDTH), F32)

        def post_qm(r=r):
            heads_t(val.pop(("qm", r)), mqgain_ref, N_MEM_HEADS, qmnt_ref, r)

        norms.append(norm)
        dots += [dot_piece("q", 0, COL_GROUP), dot_piece("kvch", ATTN_WIDTH, COL_GROUP),
                 dot_piece("cbcc", conv_base + CONV_WIDTH, COL_GROUP)]
        posts += [post_q, post_kv, post_conv]
        narrow_dots.append(dot_piece("qm", IN_PROJ_WIDTH - MEM_WIDTH, MEM_WIDTH))
        narrow_posts.append(post_qm)
    return norms, dots + narrow_dots, posts + narrow_posts


def _mixer_kernel(seq_tile, tiles_per_seq,
                  sinks_ref, xfirst_ref, xnext_ref, kblk_ref, vt_ref, win32_ref, wout32_ref,
                  nmix_ref, qgain_ref, kgain_ref, mqgain_ref, convw_ref, convb_ref,
                  ona_ref, onc_ref, onm_ref, wg_ref, wu_ref, wd_ref,
                  out_ref, wg16_ref, wu16_ref, wd16_ref,
                  qnt_ref, qmnt_ref, kpad_ref, vtpad_ref, upad_ref, cb_ref, xres_ref, win_ref, wout_ref):
    wg16_ref[...] = wg_ref[...].astype(BF16)
    wu16_ref[...] = wu_ref[...].astype(BF16)
    wd16_ref[...] = wd_ref[...].astype(BF16)
    t = pl.program_id(0)
    slot = t % 2
    n_blocks = seq_tile // BLOCK
    first_of_seq = (t % tiles_per_seq) == 0
    stage1_refs = (win_ref, nmix_ref, qgain_ref, kgain_ref, mqgain_ref,
                   qnt_ref, qmnt_ref, kpad_ref, vtpad_ref, upad_ref, cb_ref, xres_ref)

    @pl.when(t == 0)
    def _():
        win_ref[...] = win32_ref[...].astype(BF16)
        wout_ref[...] = wout32_ref[...].astype(BF16)
        norms, dots, posts = _stage1_pieces(xfirst_ref, 0, None, stage1_refs)
        for piece in norms + [p for pair in zip(dots, posts) for p in pair]:
            piece()

    norms, dots, posts = _stage1_pieces(
        xnext_ref, 1 - slot, (slot, ((t + 1) % tiles_per_seq) == 0), stage1_refs)
    n_chains = n_blocks * N_KV_HEADS
    n_early = len(dots) - STAGE1_HELD_PIECES
    assert n_early < n_chains
    chain_fill = [[] for _ in range(n_chains)]
    chain_fill[0] = [dots[0], dots[1]]
    for i in range(1, n_early + 1):
        chain_fill[i] = [posts[i - 1]] + ([dots[i + 1]] if i + 1 < n_early else [])
    tail_dots, tail_posts = dots[n_early:], posts[n_early:]

    key = lax.broadcasted_iota(jnp.int32, (BLOCK, GQA_GROUP * BLOCK), 0)
    qry = lax.broadcasted_iota(jnp.int32, (BLOCK, GQA_GROUP * BLOCK), 1) % BLOCK
    from_prev = key > qry
    dist = jnp.where(from_prev, qry + BLOCK - key, qry - key).astype(F32)
    dist_first = jnp.where(jnp.logical_and(from_prev, first_of_seq), MASKED_DIST, dist)
    zeros_q = jnp.zeros((HEAD_DIM, GQA_GROUP * BLOCK), BF16)

    chains = [(b, g) for b in range(n_blocks) for g in range(N_KV_HEADS)]
    qmn = qmnt_ref[slot]
    n_halves = seq_tile // V7X_MXU_DIM

    scores = {}

    def score_chain(b, g):
        cols = slice(b * BLOCK, (b + 1) * BLOCK)
        k_cat = kpad_ref[slot, b * BLOCK:(b + 2) * BLOCK, :]
        heads = range(g * GQA_GROUP, (g + 1) * GQA_GROUP)
        q4 = jnp.concatenate(
            [qnt_ref[slot, h * HEAD_DIM:(h + 1) * HEAD_DIM, cols] for h in heads], axis=1)
        w_q = jnp.concatenate([q4, zeros_q] if g == 0 else [zeros_q, q4], axis=0)
        scores[b, g] = jnp.dot(k_cat, w_q, preferred_element_type=F32)

    for b, g in chains[:SCORE_LOOKAHEAD]:
        score_chain(b, g)
    late = {}

    def conv_proj():
        u = upad_ref[slot, V7X_SUBLANES:V7X_SUBLANES + seq_tile, :]
        u1 = upad_ref[slot, V7X_SUBLANES - 1:V7X_SUBLANES - 1 + seq_tile, :]
        u2 = upad_ref[slot, V7X_SUBLANES - 2:V7X_SUBLANES - 2 + seq_tile, :]
        cw = convw_ref[...]
        conv = cw[0:1] * u2 + cw[1:2] * u1 + cw[2:3] * u + convb_ref[...]
        conv_n = _rms(cb_ref[slot] * conv, onc_ref[...])
        late["y_conv"] = jnp.dot(conv_n.astype(BF16), wout_ref[ATTN_WIDTH:ATTN_WIDTH + CONV_WIDTH, :],
                                 preferred_element_type=F32)

    chain_fill[CONV_PROJ_CHAIN].insert(0, conv_proj)
    kblk = kblk_ref[0]
    mem_vt = vt_ref[0]
    mem_scores = jnp.dot(kblk, qmn, preferred_element_type=F32)
    mem_units = [(c, h) for c in range(n_halves) for h in range(N_MEM_HEADS)]
    mem_o = {}

    def mem_softmax(c, h):
        s = mem_scores[h * N_MEM:(h + 1) * N_MEM, c * V7X_MXU_DIM:(c + 1) * V7X_MXU_DIM]
        m = jnp.max(s, axis=0, keepdims=True)
        p = jnp.exp2(s - m)
        inv_l = 1.0 / jnp.sum(p, axis=0, keepdims=True)
        vt_h = mem_vt[h * HEAD_DIM:(h + 1) * HEAD_DIM]
        mem_o[c, h] = jnp.dot(vt_h, p.astype(BF16), preferred_element_type=F32) * inv_l

    def finish_rows(c):
        rows = slice(c * V7X_MXU_DIM, (c + 1) * V7X_MXU_DIM)
        blocks = range(c * V7X_MXU_DIM // BLOCK, (c + 1) * V7X_MXU_DIM // BLOCK)
        attn_rows = jnp.concatenate(
            [jnp.concatenate([row for g in range(N_KV_HEADS) for row in head_rows[b, g]], axis=0)
             for b in blocks], axis=1).T
        mem_rows = jnp.concatenate([mem_o[c, h] for h in range(N_MEM_HEADS)], axis=0).T
        y = late["y_conv"][rows] + jnp.dot(_rms(attn_rows, ona_ref[...]).astype(BF16), wout_ref[0:ATTN_WIDTH, :],
                                   preferred_element_type=F32)
        y = y + jnp.dot(_rms(mem_rows, onm_ref[...]).astype(BF16), wout_ref[ATTN_WIDTH + CONV_WIDTH:, :],
                        preferred_element_type=F32)
        out_ref[0, rows, :] = xres_ref[slot, rows, :] + y

    for piece in norms:
        piece()
    head_rows = {}
    chains_per_half = len(chains) // n_halves
    assert len(mem_units) == len(chains)
    for i, (b, g) in enumerate(chains):
        heads = range(g * GQA_GROUP, (g + 1) * GQA_GROUP)
        dist_b = dist_first if b == 0 else dist
        neg_slope = jnp.concatenate(
            [jnp.full((1, BLOCK), -(2.0 ** (-8.0 * (h + 1) / N_ATTN_HEADS)) * LOG2E, F32) for h in heads], axis=1)
        sink = jnp.concatenate([jnp.full((1, BLOCK), sinks_ref[h], F32) for h in heads], axis=1) * LOG2E
        if i + SCORE_LOOKAHEAD < len(chains):
            score_chain(*chains[i + SCORE_LOOKAHEAD])
        s2 = scores.pop((b, g))
        s = jnp.where(from_prev, s2[:BLOCK], s2[BLOCK:]) + dist_b * neg_slope
        m = jnp.maximum(jnp.max(s, axis=0, keepdims=True), sink)
        p = jnp.exp2(s - m)
        l = jnp.sum(p, axis=0, keepdims=True) + jnp.exp2(sink - m)
        p_t = jnp.concatenate(
            [jnp.where(from_prev, p, 0.0).astype(BF16), jnp.where(from_prev, 0.0, p).astype(BF16)],
            axis=0)
        for piece in chain_fill[i]:
            piece()
        vt_bg = vtpad_ref[slot, g * HEAD_DIM:(g + 1) * HEAD_DIM, b * BLOCK:(b + 2) * BLOCK]
        o_t = jnp.dot(vt_bg, p_t, preferred_element_type=F32) * (1.0 / l)
        head_rows[b, g] = [o_t[:, hh * BLOCK:(hh + 1) * BLOCK] for hh in range(GQA_GROUP)]
        mem_softmax(*mem_units[i])
        if i == chains_per_half:
            finish_rows(0)
    for piece in tail_dots:
        piece()
    finish_rows(1)
    for piece in tail_posts:
        piece()


def _ffn_kernel(x_ref, gain_ref, wg_ref, wu_ref, wd_ref, out_ref):
    x = x_ref[...]
    piece_rows = V7X_MXU_DIM
    h_pieces = [_rms(x[r:r + piece_rows], gain_ref[...]).astype(BF16)
                for r in range(0, x.shape[0], piece_rows)]
    h = jnp.concatenate(h_pieces, axis=0)
    y = x
    for c in range(wg_ref.shape[1] // V7X_MXU_DIM):
        cols = slice(c * V7X_MXU_DIM, (c + 1) * V7X_MXU_DIM)
        if c == 0:
            gate = jnp.concatenate(
                [jnp.dot(hp, wg_ref[:, cols], preferred_element_type=F32) for hp in h_pieces], axis=0)
            up = jnp.concatenate(
                [jnp.dot(hp, wu_ref[:, cols], preferred_element_type=F32) for hp in h_pieces], axis=0)
        else:
            gate = jnp.dot(h, wg_ref[:, cols], preferred_element_type=F32)
            up = jnp.dot(h, wu_ref[:, cols], preferred_element_type=F32)
        act = (gate * jax.nn.sigmoid(gate) * up).astype(BF16)
        y = y + jnp.dot(act, wd_ref[cols, :], preferred_element_type=F32)
    out_ref[...] = y


def _const_spec(shape):
    return pl.BlockSpec(shape, lambda *_: (0,) * len(shape), pipeline_mode=pl.Buffered(1))


def _row(a):
    return a.reshape(1, -1)


def _layer(x, mem_blocks, lp, plan):
    seq_tile, ffn_tile, vmem_limit = plan
    batch, seq, _ = x.shape
    kblk, mem_vt = mem_blocks
    d_ff = lp["w_gate"].shape[1]

    tiles_per_seq = seq // seq_tile
    n_tiles = batch * tiles_per_seq
    slab_spec = pl.BlockSpec((D_MODEL // n_tiles, d_ff), lambda t: (t, 0))
    down_slabs = d_ff // BLOCK
    assert down_slabs <= n_tiles
    down_spec = pl.BlockSpec((BLOCK, D_MODEL), lambda t: (jnp.minimum(t, down_slabs - 1), 0))
    up_w16 = jax.ShapeDtypeStruct((D_MODEL, d_ff), BF16)
    down_w16 = jax.ShapeDtypeStruct((d_ff, D_MODEL), BF16)

    def tile_index(t):
        return (t // tiles_per_seq, t % tiles_per_seq, 0)

    mixer = pl.pallas_call(
        functools.partial(_mixer_kernel, seq_tile, tiles_per_seq),
        out_shape=(jax.ShapeDtypeStruct(x.shape, F32), up_w16, up_w16, down_w16),
        grid=(n_tiles,),
        in_specs=[
            pl.BlockSpec(memory_space=pltpu.SMEM),
            _const_spec((1, seq_tile, D_MODEL)),
            pl.BlockSpec((1, seq_tile, D_MODEL), lambda t: tile_index(jnp.minimum(t + 1, n_tiles - 1))),
            pl.BlockSpec((1, N_MEM_HEADS * N_MEM, MEM_WIDTH), lambda t: (t // tiles_per_seq, 0, 0)),
            pl.BlockSpec((1, MEM_WIDTH, N_MEM), lambda t: (t // tiles_per_seq, 0, 0)),
            _const_spec((D_MODEL, IN_PROJ_WIDTH)),
            _const_spec((D_MODEL, D_MODEL)),
            _const_spec((1, D_MODEL)),
            _const_spec((1, HEAD_DIM)),
            _const_spec((1, HEAD_DIM)),
            _const_spec((1, HEAD_DIM)),
            _const_spec((CONV_K, CONV_WIDTH)),
            _const_spec((1, CONV_WIDTH)),
            _const_spec((1, ATTN_WIDTH)),
            _const_spec((1, CONV_WIDTH)),
            _const_spec((1, MEM_WIDTH)),
            slab_spec, slab_spec, down_spec,
        ],
        out_specs=(pl.BlockSpec((1, seq_tile, D_MODEL), tile_index), slab_spec, slab_spec, down_spec),
        scratch_shapes=[
            pltpu.VMEM((2, ATTN_WIDTH, seq_tile), BF16),
            pltpu.VMEM((2, MEM_WIDTH, seq_tile), BF16),
            pltpu.VMEM((2, seq_tile + BLOCK, KV_WIDTH), BF16),
            pltpu.VMEM((2, KV_WIDTH, seq_tile + BLOCK), BF16),
            pltpu.VMEM((2, seq_tile + V7X_SUBLANES, CONV_WIDTH), F32),
            pltpu.VMEM((2, seq_tile, CONV_WIDTH), F32),
            pltpu.VMEM((2, seq_tile, D_MODEL), F32),
            pltpu.VMEM((D_MODEL, IN_PROJ_WIDTH), BF16),
            pltpu.VMEM((D_MODEL, D_MODEL), BF16),
        ],
        compiler_params=pltpu.CompilerParams(
            dimension_semantics=("arbitrary",), vmem_limit_bytes=vmem_limit),
        name="mixer",
    )
    x1, w_gate16, w_up16, w_down16 = mixer(
        lp["attn_sinks"], x, x, kblk, mem_vt, lp["w_in"], lp["w_out"],
        _row(lp["norm_mix"]), _row(lp["q_norm"]), _row(lp["k_norm"]), _row(lp["mem_q_norm"]),
        lp["conv_w"], _row(lp["conv_b"]),
        _row(lp["out_norm_attn"]), _row(lp["out_norm_conv"]), _row(lp["out_norm_mem"]),
        lp["w_gate"], lp["w_up"], lp["w_down"])

    tokens = batch * seq
    ffn = pl.pallas_call(
        _ffn_kernel,
        out_shape=jax.ShapeDtypeStruct((tokens, D_MODEL), F32),
        grid=(tokens // ffn_tile,),
        in_specs=[
            pl.BlockSpec((ffn_tile, D_MODEL), lambda i: (i, 0)),
            _const_spec((1, D_MODEL)),
            _const_spec((D_MODEL, d_ff)),
            _const_spec((D_MODEL, d_ff)),
            _const_spec((d_ff, D_MODEL)),
        ],
        out_specs=pl.BlockSpec((ffn_tile, D_MODEL), lambda i: (i, 0)),
        compiler_params=pltpu.CompilerParams(
            dimension_semantics=("arbitrary",), vmem_limit_bytes=vmem_limit),
        name="ffn",
    )
    y = ffn(x1.reshape(tokens, D_MODEL), _row(lp["norm_ffn"]), w_gate16, w_up16, w_down16)
    return y.reshape(x.shape)


def _mem_kv(mem, lp, vmem_limit):
    batch = mem.shape[0]
    seqs_per_step = 4
    steps = batch // seqs_per_step
    kblk_shape = (batch, N_MEM_HEADS * N_MEM, MEM_WIDTH)
    vt_shape = (batch, MEM_WIDTH, N_MEM)
    call = pl.pallas_call(
        _mem_kv_kernel,
        out_shape=(jax.ShapeDtypeStruct(kblk_shape, BF16), jax.ShapeDtypeStruct(vt_shape, BF16)),
        grid=(steps,),
        in_specs=[
            pl.BlockSpec((seqs_per_step, N_MEM, D_MODEL), lambda b: (b, 0, 0)),
            _const_spec((1, D_MODEL)),
            _const_spec((D_MODEL, 2 * MEM_WIDTH)),
            _const_spec((1, HEAD_DIM)),
        ],
        out_specs=(pl.BlockSpec((seqs_per_step,) + kblk_shape[1:], lambda b: (b, 0, 0)),
                   pl.BlockSpec((seqs_per_step,) + vt_shape[1:], lambda b: (b, 0, 0))),
        compiler_params=pltpu.CompilerParams(
            dimension_semantics=("arbitrary",), vmem_limit_bytes=vmem_limit),
        name="mem_kv",
    )
    return call(mem, _row(lp["norm_mem"]), lp["w_mem_kv"], _row(lp["mem_k_norm"]))


def kernel(x, mem, norm_mix, w_in, q_norm, k_norm, attn_sinks, conv_w, conv_b, norm_mem, w_mem_kv,
           mem_q_norm, mem_k_norm, out_norm_attn, out_norm_conv, out_norm_mem, w_out, norm_ffn,
           w_gate, w_up, w_down):
    plan = _plan()
    depth = w_in.shape[0]
    for l in range(depth):
        lp = dict(
            norm_mix=norm_mix[l], w_in=w_in[l], q_norm=q_norm[l], k_norm=k_norm[l],
            attn_sinks=attn_sinks[l], conv_w=conv_w[l], conv_b=conv_b[l], norm_mem=norm_mem[l],
            w_mem_kv=w_mem_kv[l], mem_q_norm=mem_q_norm[l], mem_k_norm=mem_k_norm[l],
            out_norm_attn=out_norm_attn[l], out_norm_conv=out_norm_conv[l],
            out_norm_mem=out_norm_mem[l], w_out=w_out[l], norm_ffn=norm_ffn[l],
            w_gate=w_gate[l], w_up=w_up[l], w_down=w_down[l])
        x = _layer(x, _mem_kv(mem, lp, plan[2]), lp, plan)
    return x
```

```python
import functools

import jax
import jax.numpy as jnp
from jax import lax
from jax.experimental import pallas as pl
from jax.experimental.pallas import tpu as pltpu

D_MODEL = 1024
HEAD_DIM = 64
N_ATTN_HEADS = 8
N_KV_HEADS = 2
GQA_GROUP = N_ATTN_HEADS // N_KV_HEADS
BLOCK = 128
N_MEM_HEADS = 4
N_MEM = 256
CONV_K = 3
ATTN_WIDTH = N_ATTN_HEADS * HEAD_DIM
KV_WIDTH = N_KV_HEADS * HEAD_DIM
CONV_WIDTH = 256
MEM_WIDTH = N_MEM_HEADS * HEAD_DIM
IN_PROJ_WIDTH = ATTN_WIDTH + 2 * KV_WIDTH + 3 * CONV_WIDTH + MEM_WIDTH
EPS = 1e-6
LOG2E = 1.4426950408889634
MASKED_DIST = 2.0 ** 110

V7X_VMEM_BYTES = 64 * 1024 * 1024
V7X_SUBLANES = 8
V7X_LANES = 128
V7X_MXU_DIM = 256

F32 = jnp.float32
BF16 = jnp.bfloat16


def _plan():
    seq_tile = 4 * BLOCK
    ffn_tile = 1024
    vmem_limit = V7X_VMEM_BYTES - 8 * 1024 * 1024
    return seq_tile, ffn_tile, vmem_limit


def _rms(a, gain):
    return a * lax.rsqrt(jnp.mean(a * a, axis=-1, keepdims=True) + EPS) * gain


def _head_rms_scale(t, gmat):
    sq = (t * t).astype(BF16)
    ss = jnp.dot(sq, gmat, preferred_element_type=F32)
    return lax.rsqrt(ss * (1.0 / HEAD_DIM) + EPS)


def _head_block_ones():
    row = lax.broadcasted_iota(jnp.int32, (V7X_MXU_DIM, V7X_MXU_DIM), 0) // HEAD_DIM
    col = lax.broadcasted_iota(jnp.int32, (V7X_MXU_DIM, V7X_MXU_DIM), 1) // HEAD_DIM
    return jnp.where(row == col, 1.0, 0.0).astype(BF16)


def _head_gain_column(gain_row, heads):
    two_heads = jnp.concatenate([gain_row, gain_row], axis=1)
    col = jnp.broadcast_to(two_heads, (BLOCK, BLOCK)).T
    return jnp.concatenate([col] * (heads * HEAD_DIM // BLOCK), axis=0)


def _head_rms_rows(t, gain):
    heads = t.shape[0] // HEAD_DIM
    out = []
    for h in range(heads):
        th = t[h * HEAD_DIM:(h + 1) * HEAD_DIM]
        ss = jnp.sum(th * th, axis=0, keepdims=True)
        out.append(th * lax.rsqrt(ss * (1.0 / HEAD_DIM) + EPS) * gain[h * HEAD_DIM:(h + 1) * HEAD_DIM])
    return out


def _mem_kv_kernel(mem_ref, gain_ref, w_ref, kgain_ref, kblk_ref, vt_ref):
    n_seq = mem_ref.shape[0]
    m = mem_ref[...].reshape(n_seq * N_MEM, D_MODEL)
    mn = _rms(m, gain_ref[...]).astype(BF16)
    kv = jnp.dot(mn, w_ref[...].astype(BF16), preferred_element_type=F32)
    k = kv[:, :MEM_WIDTH]
    kgain = jnp.concatenate([kgain_ref[...]] * N_MEM_HEADS, axis=1)
    kn = k * _head_rms_scale(k, _head_block_ones()) * kgain
    lane_head = lax.broadcasted_iota(jnp.int32, (N_MEM, MEM_WIDTH), 1) // HEAD_DIM
    for s in range(n_seq):
        rows = slice(s * N_MEM, (s + 1) * N_MEM)
        kn_s = kn[rows]
        vt_ref[s] = kv[rows, MEM_WIDTH:].T.astype(BF16)
        for h in range(N_MEM_HEADS):
            kblk_ref[s, h * N_MEM:(h + 1) * N_MEM, :] = jnp.where(lane_head == h, kn_s, 0.0).astype(BF16)


ROW_CHUNK = V7X_MXU_DIM
COL_GROUP = 2 * V7X_MXU_DIM
MEM_SCORE_LOOKAHEAD = 5
SCORE_LOOKAHEAD = 5
STAGE1_HELD_PIECES = 2
CONV_PROJ_CHAIN = 3


def _stage1_pieces(x_ref, wslot, carry, refs):
    (win_ref, nmix_ref, qgain_ref, kgain_ref, mqgain_ref,
     qnt_ref, qmnt_ref, kpad_ref, vtpad_ref, upad_ref, cb_ref, xres_ref) = refs
    seq_tile = cb_ref.shape[1]
    xn, val = {}, {}
    conv_base = ATTN_WIDTH + 2 * KV_WIDTH

    def carried(ref_slice_fn, shape, dtype):
        if carry is None:
            return jnp.zeros(shape, dtype)
        prev_slot, first_of_seq = carry
        return jnp.where(first_of_seq, jnp.zeros(shape, dtype), ref_slice_fn(prev_slot))

    def heads_t(t, gain_ref, heads, out_ref, r):
        t = t.T
        gain = _head_gain_column(gain_ref[...] * (HEAD_DIM ** -0.5 * LOG2E), heads)
        gain = jnp.concatenate([gain] * (ROW_CHUNK // BLOCK), axis=1)
        out_ref[wslot, :, r * ROW_CHUNK:(r + 1) * ROW_CHUNK] = jnp.concatenate(
            _head_rms_rows(t, gain), axis=0).astype(BF16)

    norms, dots, posts, narrow_dots, narrow_posts = [], [], [], [], []
    for r in range(seq_tile // ROW_CHUNK):
        rows = slice(r * ROW_CHUNK, (r + 1) * ROW_CHUNK)

        def norm(r=r, rows=rows):
            x = x_ref[0, rows, :]
            xres_ref[wslot, rows, :] = x
            xn[r] = _rms(x, nmix_ref[...]).astype(BF16)

        def dot_piece(name, start, width, r=r):
            def run():
                val[name, r] = jnp.dot(xn[r], win_ref[:, start:start + width], preferred_element_type=F32)
            return run

        def post_q(r=r):
            heads_t(val.pop(("q", r)), qgain_ref, N_ATTN_HEADS, qnt_ref, r)

        def post_kv(r=r):
            t = val.pop(("kvch", r))
            kv_a = t[:, :2 * KV_WIDTH]
            val["ch", r] = t[:, 2 * KV_WIDTH:]
            scale = _head_rms_scale(kv_a, _head_block_ones())
            kgain = jnp.concatenate([kgain_ref[...]] * N_KV_HEADS, axis=1)
            kn = kv_a[:, :KV_WIDTH] * scale[:, :KV_WIDTH] * kgain
            kpad_ref[wslot, BLOCK + r * ROW_CHUNK:BLOCK + (r + 1) * ROW_CHUNK, :] = kn.astype(BF16)
            vtpad_ref[wslot, :, BLOCK + r * ROW_CHUNK:BLOCK + (r + 1) * ROW_CHUNK] = (
                kv_a[:, KV_WIDTH:].T.astype(BF16))
            if r == 0:
                kpad_ref[wslot, 0:BLOCK, :] = carried(
                    lambda s: kpad_ref[s, seq_tile:seq_tile + BLOCK, :], (BLOCK, KV_WIDTH), BF16)
                vtpad_ref[wslot, :, 0:BLOCK] = carried(
                    lambda s: vtpad_ref[s, :, seq_tile:seq_tile + BLOCK], (KV_WIDTH, BLOCK), BF16)

        def post_conv(r=r):
            t = val.pop(("cbcc", r))
            cb_ref[wslot, r * ROW_CHUNK:(r + 1) * ROW_CHUNK, :] = t[:, :CONV_WIDTH]
            u = t[:, CONV_WIDTH:] * val.pop(("ch", r))
            upad_ref[wslot, V7X_SUBLANES + r * ROW_CHUNK:V7X_SUBLANES + (r + 1) * ROW_CHUNK, :] = u
            if r == 0:
                upad_ref[wslot, 0:V7X_SUBLANES, :] = carried(
                    lambda s: upad_ref[s, seq_tile:seq_tile + V7X_SUBLANES, :],
                    (V7X_SUBLANES, CONV_WIDTH), F32)

        def post_qm(r=r):
            heads_t(val.pop(("qm", r)), mqgain_ref, N_MEM_HEADS, qmnt_ref, r)

        norms.append(norm)
        dots += [dot_piece("q", 0, COL_GROUP), dot_piece("kvch", ATTN_WIDTH, COL_GROUP),
                 dot_piece("cbcc", conv_base + CONV_WIDTH, COL_GROUP)]
        posts += [post_q, post_kv, post_conv]
        narrow_dots.append(dot_piece("qm", IN_PROJ_WIDTH - MEM_WIDTH, MEM_WIDTH))
        narrow_posts.append(post_qm)
    return norms, dots + narrow_dots, posts + narrow_posts


def _mixer_kernel(seq_tile, tiles_per_seq,
                  sinks_ref, xfirst_ref, xnext_ref, kblk_ref, vt_ref, win32_ref, wout32_ref,
                  nmix_ref, qgain_ref, kgain_ref, mqgain_ref, convw_ref, convb_ref,
                  ona_ref, onc_ref, onm_ref, wg_ref, wu_ref, wd_ref,
                  out_ref, wg16_ref, wu16_ref, wd16_ref,
                  qnt_ref, qmnt_ref, kpad_ref, vtpad_ref, upad_ref, cb_ref, xres_ref, win_ref, wout_ref):
    wg16_ref[...] = wg_ref[...].astype(BF16)
    wu16_ref[...] = wu_ref[...].astype(BF16)
    wd16_ref[...] = wd_ref[...].astype(BF16)
    t = pl.program_id(0)
    slot = t % 2
    n_blocks = seq_tile // BLOCK
    first_of_seq = (t % tiles_per_seq) == 0
    stage1_refs = (win_ref, nmix_ref, qgain_ref, kgain_ref, mqgain_ref,
                   qnt_ref, qmnt_ref, kpad_ref, vtpad_ref, upad_ref, cb_ref, xres_ref)

    @pl.when(t == 0)
    def _():
        win_ref[...] = win32_ref[...].astype(BF16)
        wout_ref[...] = wout32_ref[...].astype(BF16)
        norms, dots, posts = _stage1_pieces(xfirst_ref, 0, None, stage1_refs)
        for piece in norms + [p for pair in zip(dots, posts) for p in pair]:
            piece()

    norms, dots, posts = _stage1_pieces(
        xnext_ref, 1 - slot, (slot, ((t + 1) % tiles_per_seq) == 0), stage1_refs)
    n_chains = n_blocks * N_KV_HEADS
    n_early = len(dots) - STAGE1_HELD_PIECES
    assert n_early < n_chains
    chain_fill = [[] for _ in range(n_chains)]
    chain_fill[0] = [dots[0], dots[1]]
    for i in range(1, n_early + 1):
        chain_fill[i] = [posts[i - 1]] + ([dots[i + 1]] if i + 1 < n_early else [])
    tail_dots, tail_posts = dots[n_early:], posts[n_early:]

    key = lax.broadcasted_iota(jnp.int32, (BLOCK, GQA_GROUP * BLOCK), 0)
    qry = lax.broadcasted_iota(jnp.int32, (BLOCK, GQA_GROUP * BLOCK), 1) % BLOCK
    from_prev = key > qry
    dist = jnp.where(from_prev, qry + BLOCK - key, qry - key).astype(F32)
    dist_first = jnp.where(jnp.logical_and(from_prev, first_of_seq), MASKED_DIST, dist)
    zeros_q = jnp.zeros((HEAD_DIM, GQA_GROUP * BLOCK), BF16)

    chains = [(b, g) for b in range(n_blocks) for g in range(N_KV_HEADS)]
    n_halves = seq_tile // V7X_MXU_DIM

    scores = {}

    def score_chain(b, g):
        cols = slice(b * BLOCK, (b + 1) * BLOCK)
        k_cat = kpad_ref[slot, b * BLOCK:(b + 2) * BLOCK, :]
        heads = range(g * GQA_GROUP, (g + 1) * GQA_GROUP)
        q4 = jnp.concatenate(
            [qnt_ref[slot, h * HEAD_DIM:(h + 1) * HEAD_DIM, cols] for h in heads], axis=1)
        w_q = jnp.concatenate([q4, zeros_q] if g == 0 else [zeros_q, q4], axis=0)
        scores[b, g] = jnp.dot(k_cat, w_q, preferred_element_type=F32)

    for b, g in chains[:SCORE_LOOKAHEAD]:
        score_chain(b, g)
    late = {}

    def conv_proj():
        u = upad_ref[slot, V7X_SUBLANES:V7X_SUBLANES + seq_tile, :]
        u1 = upad_ref[slot, V7X_SUBLANES - 1:V7X_SUBLANES - 1 + seq_tile, :]
        u2 = upad_ref[slot, V7X_SUBLANES - 2:V7X_SUBLANES - 2 + seq_tile, :]
        cw = convw_ref[...]
        conv = cw[0:1] * u2 + cw[1:2] * u1 + cw[2:3] * u + convb_ref[...]
        conv_n = _rms(cb_ref[slot] * conv, onc_ref[...])
        late["y_conv"] = jnp.dot(conv_n.astype(BF16), wout_ref[ATTN_WIDTH:ATTN_WIDTH + CONV_WIDTH, :],
                                 preferred_element_type=F32)

    chain_fill[CONV_PROJ_CHAIN].insert(0, conv_proj)
    mem_vt = vt_ref[0]
    mem_units = [(c, h) for c in range(n_halves) for h in range(N_MEM_HEADS)]
    mem_scores, mem_o = {}, {}

    def mem_score(c, h):
        mem_scores[c, h] = jnp.dot(kblk_ref[0, h * N_MEM:(h + 1) * N_MEM, :],
                                   qmnt_ref[slot, :, c * V7X_MXU_DIM:(c + 1) * V7X_MXU_DIM],
                                   preferred_element_type=F32)

    for unit in mem_units[:MEM_SCORE_LOOKAHEAD]:
        mem_score(*unit)

    def mem_softmax(c, h):
        s = mem_scores.pop((c, h))
        m = jnp.max(s, axis=0, keepdims=True)
        p = jnp.exp2(s - m)
        inv_l = 1.0 / jnp.sum(p, axis=0, keepdims=True)
        vt_h = mem_vt[h * HEAD_DIM:(h + 1) * HEAD_DIM]
        mem_o[c, h] = jnp.dot(vt_h, p.astype(BF16), preferred_element_type=F32) * inv_l

    def finish_rows(c):
        rows = slice(c * V7X_MXU_DIM, (c + 1) * V7X_MXU_DIM)
        blocks = range(c * V7X_MXU_DIM // BLOCK, (c + 1) * V7X_MXU_DIM // BLOCK)
        attn_rows = jnp.concatenate(
            [jnp.concatenate([row for g in range(N_KV_HEADS) for row in head_rows[b, g]], axis=0)
             for b in blocks], axis=1).T
        mem_rows = jnp.concatenate([mem_o[c, h] for h in range(N_MEM_HEADS)], axis=0).T
        y = late["y_conv"][rows] + jnp.dot(_rms(attn_rows, ona_ref[...]).astype(BF16), wout_ref[0:ATTN_WIDTH, :],
                                   preferred_element_type=F32)
        y = y + jnp.dot(_rms(mem_rows, onm_ref[...]).astype(BF16), wout_ref[ATTN_WIDTH + CONV_WIDTH:, :],
                        preferred_element_type=F32)
        out_ref[0, rows, :] = xres_ref[slot, rows, :] + y

    for piece in norms:
        piece()
    head_rows = {}
    chains_per_half = len(chains) // n_halves
    assert len(mem_units) == len(chains)
    for i, (b, g) in enumerate(chains):
        heads = range(g * GQA_GROUP, (g + 1) * GQA_GROUP)
        dist_b = dist_first if b == 0 else dist
        neg_slope = jnp.concatenate(
            [jnp.full((1, BLOCK), -(2.0 ** (-8.0 * (h + 1) / N_ATTN_HEADS)) * LOG2E, F32) for h in heads], axis=1)
        sink = jnp.concatenate([jnp.full((1, BLOCK), sinks_ref[h], F32) for h in heads], axis=1) * LOG2E
        if i + SCORE_LOOKAHEAD < len(chains):
            score_chain(*chains[i + SCORE_LOOKAHEAD])
        s2 = scores.pop((b, g))
        s = jnp.where(from_prev, s2[:BLOCK], s2[BLOCK:]) + dist_b * neg_slope
        m = jnp.maximum(jnp.max(s, axis=0, keepdims=True), sink)
        p = jnp.exp2(s - m)
        l = jnp.sum(p, axis=0, keepdims=True) + jnp.exp2(sink - m)
        p_t = jnp.concatenate(
            [jnp.where(from_prev, p, 0.0).astype(BF16), jnp.where(from_prev, 0.0, p).astype(BF16)],
            axis=0)
        for piece in chain_fill[i]:
            piece()
        vt_bg = vtpad_ref[slot, g * HEAD_DIM:(g + 1) * HEAD_DIM, b * BLOCK:(b + 2) * BLOCK]
        o_t = jnp.dot(vt_bg, p_t, preferred_element_type=F32) * (1.0 / l)
        head_rows[b, g] = [o_t[:, hh * BLOCK:(hh + 1) * BLOCK] for hh in range(GQA_GROUP)]
        if i + MEM_SCORE_LOOKAHEAD < len(mem_units):
            mem_score(*mem_units[i + MEM_SCORE_LOOKAHEAD])
        mem_softmax(*mem_units[i])
        if i == chains_per_half:
            finish_rows(0)
    for piece in tail_dots:
        piece()
    finish_rows(1)
    for piece in tail_posts:
        piece()


def _ffn_kernel(x_ref, gain_ref, wg_ref, wu_ref, wd_ref, out_ref):
    x = x_ref[...]
    piece_rows = V7X_MXU_DIM
    h_pieces = [_rms(x[r:r + piece_rows], gain_ref[...]).astype(BF16)
                for r in range(0, x.shape[0], piece_rows)]
    h = jnp.concatenate(h_pieces, axis=0)
    y = x
    for c in range(wg_ref.shape[1] // V7X_MXU_DIM):
        cols = slice(c * V7X_MXU_DIM, (c + 1) * V7X_MXU_DIM)
        if c == 0:
            gate = jnp.concatenate(
                [jnp.dot(hp, wg_ref[:, cols], preferred_element_type=F32) for hp in h_pieces], axis=0)
            up = jnp.concatenate(
                [jnp.dot(hp, wu_ref[:, cols], preferred_element_type=F32) for hp in h_pieces], axis=0)
        else:
            gate = jnp.dot(h, wg_ref[:, cols], preferred_element_type=F32)
            up = jnp.dot(h, wu_ref[:, cols], preferred_element_type=F32)
        act = (gate * jax.nn.sigmoid(gate) * up).astype(BF16)
        y = y + jnp.dot(act, wd_ref[cols, :], preferred_element_type=F32)
    out_ref[...] = y


def _const_spec(shape):
    return pl.BlockSpec(shape, lambda *_: (0,) * len(shape), pipeline_mode=pl.Buffered(1))


def _row(a):
    return a.reshape(1, -1)


def _layer(x, mem_blocks, lp, plan):
    seq_tile, ffn_tile, vmem_limit = plan
    batch, seq, _ = x.shape
    kblk, mem_vt = mem_blocks
    d_ff = lp["w_gate"].shape[1]

    tiles_per_seq = seq // seq_tile
    n_tiles = batch * tiles_per_seq
    slab_spec = pl.BlockSpec((D_MODEL // n_tiles, d_ff), lambda t: (t, 0))
    down_slabs = d_ff // BLOCK
    assert down_slabs <= n_tiles
    down_spec = pl.BlockSpec((BLOCK, D_MODEL), lambda t: (jnp.minimum(t, down_slabs - 1), 0))
    up_w16 = jax.ShapeDtypeStruct((D_MODEL, d_ff), BF16)
    down_w16 = jax.ShapeDtypeStruct((d_ff, D_MODEL), BF16)

    def tile_index(t):
        return (t // tiles_per_seq, t % tiles_per_seq, 0)

    mixer = pl.pallas_call(
        functools.partial(_mixer_kernel, seq_tile, tiles_per_seq),
        out_shape=(jax.ShapeDtypeStruct(x.shape, F32), up_w16, up_w16, down_w16),
        grid=(n_tiles,),
        in_specs=[
            pl.BlockSpec(memory_space=pltpu.SMEM),
            _const_spec((1, seq_tile, D_MODEL)),
            pl.BlockSpec((1, seq_tile, D_MODEL), lambda t: tile_index(jnp.minimum(t + 1, n_tiles - 1))),
            pl.BlockSpec((1, N_MEM_HEADS * N_MEM, MEM_WIDTH), lambda t: (t // tiles_per_seq, 0, 0)),
            pl.BlockSpec((1, MEM_WIDTH, N_MEM), lambda t: (t // tiles_per_seq, 0, 0)),
            _const_spec((D_MODEL, IN_PROJ_WIDTH)),
            _const_spec((D_MODEL, D_MODEL)),
            _const_spec((1, D_MODEL)),
            _const_spec((1, HEAD_DIM)),
            _const_spec((1, HEAD_DIM)),
            _const_spec((1, HEAD_DIM)),
            _const_spec((CONV_K, CONV_WIDTH)),
            _const_spec((1, CONV_WIDTH)),
            _const_spec((1, ATTN_WIDTH)),
            _const_spec((1, CONV_WIDTH)),
            _const_spec((1, MEM_WIDTH)),
            slab_spec, slab_spec, down_spec,
        ],
        out_specs=(pl.BlockSpec((1, seq_tile, D_MODEL), tile_index), slab_spec, slab_spec, down_spec),
        scratch_shapes=[
            pltpu.VMEM((2, ATTN_WIDTH, seq_tile), BF16),
            pltpu.VMEM((2, MEM_WIDTH, seq_tile), BF16),
            pltpu.VMEM((2, seq_tile + BLOCK, KV_WIDTH), BF16),
            pltpu.VMEM((2, KV_WIDTH, seq_tile + BLOCK), BF16),
            pltpu.VMEM((2, seq_tile + V7X_SUBLANES, CONV_WIDTH), F32),
            pltpu.VMEM((2, seq_tile, CONV_WIDTH), F32),
            pltpu.VMEM((2, seq_tile, D_MODEL), F32),
            pltpu.VMEM((D_MODEL, IN_PROJ_WIDTH), BF16),
            pltpu.VMEM((D_MODEL, D_MODEL), BF16),
        ],
        compiler_params=pltpu.CompilerParams(
            dimension_semantics=("arbitrary",), vmem_limit_bytes=vmem_limit),
        name="mixer",
    )
    x1, w_gate16, w_up16, w_down16 = mixer(
        lp["attn_sinks"], x, x, kblk, mem_vt, lp["w_in"], lp["w_out"],
        _row(lp["norm_mix"]), _row(lp["q_norm"]), _row(lp["k_norm"]), _row(lp["mem_q_norm"]),
        lp["conv_w"], _row(lp["conv_b"]),
        _row(lp["out_norm_attn"]), _row(lp["out_norm_conv"]), _row(lp["out_norm_mem"]),
        lp["w_gate"], lp["w_up"], lp["w_down"])

    tokens = batch * seq
    ffn = pl.pallas_call(
        _ffn_kernel,
        out_shape=jax.ShapeDtypeStruct((tokens, D_MODEL), F32),
        grid=(tokens // ffn_tile,),
        in_specs=[
            pl.BlockSpec((ffn_tile, D_MODEL), lambda i: (i, 0)),
            _const_spec((1, D_MODEL)),
            _const_spec((D_MODEL, d_ff)),
            _const_spec((D_MODEL, d_ff)),
            _const_spec((d_ff, D_MODEL)),
        ],
        out_specs=pl.BlockSpec((ffn_tile, D_MODEL), lambda i: (i, 0)),
        compiler_params=pltpu.CompilerParams(
            dimension_semantics=("arbitrary",), vmem_limit_bytes=vmem_limit),
        name="ffn",
    )
    y = ffn(x1.reshape(tokens, D_MODEL), _row(lp["norm_ffn"]), w_gate16, w_up16, w_down16)
    return y.reshape(x.shape)


def _mem_kv(mem, lp, vmem_limit):
    batch = mem.shape[0]
    seqs_per_step = 4
    steps = batch // seqs_per_step
    kblk_shape = (batch, N_MEM_HEADS * N_MEM, MEM_WIDTH)
    vt_shape = (batch, MEM_WIDTH, N_MEM)
    call = pl.pallas_call(
        _mem_kv_kernel,
        out_shape=(jax.ShapeDtypeStruct(kblk_shape, BF16), jax.ShapeDtypeStruct(vt_shape, BF16)),
        grid=(steps,),
        in_specs=[
            pl.BlockSpec((seqs_per_step, N_MEM, D_MODEL), lambda b: (b, 0, 0)),
            _const_spec((1, D_MODEL)),
            _const_spec((D_MODEL, 2 * MEM_WIDTH)),
            _const_spec((1, HEAD_DIM)),
        ],
        out_specs=(pl.BlockSpec((seqs_per_step,) + kblk_shape[1:], lambda b: (b, 0, 0)),
                   pl.BlockSpec((seqs_per_step,) + vt_shape[1:], lambda b: (b, 0, 0))),
        compiler_params=pltpu.CompilerParams(
            dimension_semantics=("arbitrary",), vmem_limit_bytes=vmem_limit),
        name="mem_kv",
    )
    return call(mem, _row(lp["norm_mem"]), lp["w_mem_kv"], _row(lp["mem_k_norm"]))


def kernel(x, mem, norm_mix, w_in, q_norm, k_norm, attn_sinks, conv_w, conv_b, norm_mem, w_mem_kv,
           mem_q_norm, mem_k_norm, out_norm_attn, out_norm_conv, out_norm_mem, w_out, norm_ffn,
           w_gate, w_up, w_down):
    plan = _plan()
    depth = w_in.shape[0]
    for l in range(depth):
        lp = dict(
            norm_mix=norm_mix[l], w_in=w_in[l], q_norm=q_norm[l], k_norm=k_norm[l],
            attn_sinks=attn_sinks[l], conv_w=conv_w[l], conv_b=conv_b[l], norm_mem=norm_mem[l],
            w_mem_kv=w_mem_kv[l], mem_q_norm=mem_q_norm[l], mem_k_norm=mem_k_norm[l],
            out_norm_attn=out_norm_attn[l], out_norm_conv=out_norm_conv[l],
            out_norm_mem=out_norm_mem[l], w_out=w_out[l], norm_ffn=norm_ffn[l],
            w_gate=w_gate[l], w_up=w_up[l], w_down=w_down[l])
        x = _layer(x, _mem_kv(mem, lp, plan[2]), lp, plan)
    return x
```

```python
import functools

import jax
import jax.numpy as jnp
from jax import lax
from jax.experimental import pallas as pl
from jax.experimental.pallas import tpu as pltpu

D_MODEL = 1024
HEAD_DIM = 64
N_ATTN_HEADS = 8
N_KV_HEADS = 2
GQA_GROUP = N_ATTN_HEADS // N_KV_HEADS
BLOCK = 128
N_MEM_HEADS = 4
N_MEM = 256
CONV_K = 3
ATTN_WIDTH = N_ATTN_HEADS * HEAD_DIM
KV_WIDTH = N_KV_HEADS * HEAD_DIM
CONV_WIDTH = 256
MEM_WIDTH = N_MEM_HEADS * HEAD_DIM
IN_PROJ_WIDTH = ATTN_WIDTH + 2 * KV_WIDTH + 3 * CONV_WIDTH + MEM_WIDTH
EPS = 1e-6
LOG2E = 1.4426950408889634
MASKED_DIST = 2.0 ** 110

V7X_VMEM_BYTES = 64 * 1024 * 1024
V7X_SUBLANES = 8
V7X_LANES = 128
V7X_MXU_DIM = 256

F32 = jnp.float32
BF16 = jnp.bfloat16


def _plan():
    seq_tile = 4 * BLOCK
    ffn_tile = 1024
    vmem_limit = V7X_VMEM_BYTES - 8 * 1024 * 1024
    return seq_tile, ffn_tile, vmem_limit


def _rms(a, gain):
    return a * lax.rsqrt(jnp.mean(a * a, axis=-1, keepdims=True) + EPS) * gain


def _head_rms_scale(t, gmat):
    sq = (t * t).astype(BF16)
    ss = jnp.dot(sq, gmat, preferred_element_type=F32)
    return lax.rsqrt(ss * (1.0 / HEAD_DIM) + EPS)


def _head_block_ones():
    row = lax.broadcasted_iota(jnp.int32, (V7X_MXU_DIM, V7X_MXU_DIM), 0) // HEAD_DIM
    col = lax.broadcasted_iota(jnp.int32, (V7X_MXU_DIM, V7X_MXU_DIM), 1) // HEAD_DIM
    return jnp.where(row == col, 1.0, 0.0).astype(BF16)


def _head_gain_column(gain_row, heads):
    two_heads = jnp.concatenate([gain_row, gain_row], axis=1)
    col = jnp.broadcast_to(two_heads, (BLOCK, BLOCK)).T
    return jnp.concatenate([col] * (heads * HEAD_DIM // BLOCK), axis=0)


def _head_rms_rows(t, gain):
    heads = t.shape[0] // HEAD_DIM
    out = []
    for h in range(heads):
        th = t[h * HEAD_DIM:(h + 1) * HEAD_DIM]
        ss = jnp.sum(th * th, axis=0, keepdims=True)
        out.append(th * lax.rsqrt(ss * (1.0 / HEAD_DIM) + EPS) * gain[h * HEAD_DIM:(h + 1) * HEAD_DIM])
    return out


def _mem_kv_kernel(mem_ref, gain_ref, w_ref, kgain_ref, kblk_ref, vt_ref):
    n_seq = mem_ref.shape[0]
    m = mem_ref[...].reshape(n_seq * N_MEM, D_MODEL)
    mn = _rms(m, gain_ref[...]).astype(BF16)
    kv = jnp.dot(mn, w_ref[...].astype(BF16), preferred_element_type=F32)
    k = kv[:, :MEM_WIDTH]
    kgain = jnp.concatenate([kgain_ref[...]] * N_MEM_HEADS, axis=1)
    kn = k * _head_rms_scale(k, _head_block_ones()) * kgain
    lane_head = lax.broadcasted_iota(jnp.int32, (N_MEM, MEM_WIDTH), 1) // HEAD_DIM
    for s in range(n_seq):
        rows = slice(s * N_MEM, (s + 1) * N_MEM)
        kn_s = kn[rows]
        vt_ref[s] = kv[rows, MEM_WIDTH:].T.astype(BF16)
        for h in range(N_MEM_HEADS):
            kblk_ref[s, h * N_MEM:(h + 1) * N_MEM, :] = jnp.where(lane_head == h, kn_s, 0.0).astype(BF16)


ROW_CHUNK = V7X_MXU_DIM
COL_GROUP = 2 * V7X_MXU_DIM
MEM_SCORE_LOOKAHEAD = 5
SCORE_LOOKAHEAD = 5
STAGE1_HELD_PIECES = 2
CONV_PROJ_CHAIN = 3


def _stage1_pieces(x_ref, wslot, carry, refs):
    (win_ref, nmix_ref, qgain_ref, kgain_ref, mqgain_ref,
     qnt_ref, qmnt_ref, kpad_ref, vtpad_ref, upad_ref, cb_ref, xres_ref) = refs
    seq_tile = cb_ref.shape[1]
    xn, val = {}, {}
    conv_base = ATTN_WIDTH + 2 * KV_WIDTH

    def carried(ref_slice_fn, shape, dtype):
        if carry is None:
            return jnp.zeros(shape, dtype)
        prev_slot, first_of_seq = carry
        return jnp.where(first_of_seq, jnp.zeros(shape, dtype), ref_slice_fn(prev_slot))

    def heads_t(t, gain_ref, heads, out_ref, r):
        t = t.T
        gain = _head_gain_column(gain_ref[...] * (HEAD_DIM ** -0.5 * LOG2E), heads)
        gain = jnp.concatenate([gain] * (ROW_CHUNK // BLOCK), axis=1)
        out_ref[wslot, :, r * ROW_CHUNK:(r + 1) * ROW_CHUNK] = jnp.concatenate(
            _head_rms_rows(t, gain), axis=0).astype(BF16)

    norms, dots, posts, narrow_dots, narrow_posts = [], [], [], [], []
    for r in range(seq_tile // ROW_CHUNK):
        rows = slice(r * ROW_CHUNK, (r + 1) * ROW_CHUNK)

        def norm(r=r, rows=rows):
            x = x_ref[0, rows, :]
            xres_ref[wslot, rows, :] = x
            xn[r] = _rms(x, nmix_ref[...]).astype(BF16)

        def dot_piece(name, start, width, r=r):
            def run():
                val[name, r] = jnp.dot(xn[r], win_ref[:, start:start + width], preferred_element_type=F32)
            return run

        def post_q(r=r):
            heads_t(val.pop(("q", r)), qgain_ref, N_ATTN_HEADS, qnt_ref, r)

        def post_kv(r=r):
            t = val.pop(("kvch", r))
            kv_a = t[:, :2 * KV_WIDTH]
            val["ch", r] = t[:, 2 * KV_WIDTH:]
            scale = _head_rms_scale(kv_a, _head_block_ones())
            kgain = jnp.concatenate([kgain_ref[...]] * N_KV_HEADS, axis=1)
            kn = kv_a[:, :KV_WIDTH] * scale[:, :KV_WIDTH] * kgain
            kpad_ref[wslot, BLOCK + r * ROW_CHUNK:BLOCK + (r + 1) * ROW_CHUNK, :] = kn.astype(BF16)
            vtpad_ref[wslot, :, BLOCK + r * ROW_CHUNK:BLOCK + (r + 1) * ROW_CHUNK] = (
                kv_a[:, KV_WIDTH:].T.astype(BF16))
            if r == 0:
                kpad_ref[wslot, 0:BLOCK, :] = carried(
                    lambda s: kpad_ref[s, seq_tile:seq_tile + BLOCK, :], (BLOCK, KV_WIDTH), BF16)
                vtpad_ref[wslot, :, 0:BLOCK] = carried(
                    lambda s: vtpad_ref[s, :, seq_tile:seq_tile + BLOCK], (KV_WIDTH, BLOCK), BF16)

        def post_conv(r=r):
            t = val.pop(("cbcc", r))
            cb_ref[wslot, r * ROW_CHUNK:(r + 1) * ROW_CHUNK, :] = t[:, :CONV_WIDTH]
            u = t[:, CONV_WIDTH:] * val.pop(("ch", r))
            upad_ref[wslot, V7X_SUBLANES + r * ROW_CHUNK:V7X_SUBLANES + (r + 1) * ROW_CHUNK, :] = u
            if r == 0:
                upad_ref[wslot, 0:V7X_SUBLANES, :] = carried(
                    lambda s: upad_ref[s, seq_tile:seq_tile + V7X_SUBLANES, :],
                    (V7X_SUBLANES, CONV_WIDTH), F32)

        def post_qm(r=r):
            heads_t(val.pop(("qm", r)), mqgain_ref, N_MEM_HEADS, qmnt_ref, r)

        norms.append(norm)
        dots += [dot_piece("q", 0, COL_GROUP), dot_piece("kvch", ATTN_WIDTH, COL_GROUP),
                 dot_piece("cbcc", conv_base + CONV_WIDTH, COL_GROUP)]
        posts += [post_q, post_kv, post_conv]
        narrow_dots.append(dot_piece("qm", IN_PROJ_WIDTH - MEM_WIDTH, MEM_WIDTH))
        narrow_posts.append(post_qm)
    return norms, dots + narrow_dots, posts + narrow_posts


def _mixer_kernel(seq_tile, tiles_per_seq,
                  sinks_ref, xfirst_ref, xnext_ref, kblk_ref, vt_ref, win32_ref, wout32_ref,
                  nmix_ref, qgain_ref, kgain_ref, mqgain_ref, convw_ref, convb_ref,
                  ona_ref, onc_ref, onm_ref, wg_ref, wu_ref, wd_ref,
                  out_ref, wg16_ref, wu16_ref, wd16_ref,
                  qnt_ref, qmnt_ref, kpad_ref, vtpad_ref, upad_ref, cb_ref, xres_ref, win_ref, wout_ref):
    wg16_ref[...] = wg_ref[...].astype(BF16)
    wu16_ref[...] = wu_ref[...].astype(BF16)
    wd16_ref[...] = wd_ref[...].astype(BF16)
    t = pl.program_id(0)
    slot = t % 2
    n_blocks = seq_tile // BLOCK
    first_of_seq = (t % tiles_per_seq) == 0
    stage1_refs = (win_ref, nmix_ref, qgain_ref, kgain_ref, mqgain_ref,
                   qnt_ref, qmnt_ref, kpad_ref, vtpad_ref, upad_ref, cb_ref, xres_ref)

    @pl.when(t == 0)
    def _():
        win_ref[...] = win32_ref[...].astype(BF16)
        wout_ref[...] = wout32_ref[...].astype(BF16)
        norms, dots, posts = _stage1_pieces(xfirst_ref, 0, None, stage1_refs)
        for piece in norms + [p for pair in zip(dots, posts) for p in pair]:
            piece()

    norms, dots, posts = _stage1_pieces(
        xnext_ref, 1 - slot, (slot, ((t + 1) % tiles_per_seq) == 0), stage1_refs)
    n_chains = n_blocks * N_KV_HEADS
    n_early = len(dots) - STAGE1_HELD_PIECES
    assert n_early < n_chains
    chain_fill = [[] for _ in range(n_chains)]
    chain_fill[0] = [dots[0], dots[1]]
    for i in range(1, n_early + 1):
        chain_fill[i] = [posts[i - 1]] + ([dots[i + 1]] if i + 1 < n_early else [])
    tail_dots, tail_posts = dots[n_early:], posts[n_early:]

    key = lax.broadcasted_iota(jnp.int32, (BLOCK, GQA_GROUP * BLOCK), 0)
    qry = lax.broadcasted_iota(jnp.int32, (BLOCK, GQA_GROUP * BLOCK), 1) % BLOCK
    from_prev = key > qry
    dist = jnp.where(from_prev, qry + BLOCK - key, qry - key).astype(F32)
    dist_first = jnp.where(jnp.logical_and(from_prev, first_of_seq), MASKED_DIST, dist)
    zeros_q = jnp.zeros((HEAD_DIM, GQA_GROUP * BLOCK), BF16)

    chains = [(b, g) for b in range(n_blocks) for g in range(N_KV_HEADS)]
    n_halves = seq_tile // V7X_MXU_DIM

    scores = {}

    def score_chain(b, g):
        cols = slice(b * BLOCK, (b + 1) * BLOCK)
        k_cat = kpad_ref[slot, b * BLOCK:(b + 2) * BLOCK, :]
        heads = range(g * GQA_GROUP, (g + 1) * GQA_GROUP)
        q4 = jnp.concatenate(
            [qnt_ref[slot, h * HEAD_DIM:(h + 1) * HEAD_DIM, cols] for h in heads], axis=1)
        w_q = jnp.concatenate([q4, zeros_q] if g == 0 else [zeros_q, q4], axis=0)
        scores[b, g] = jnp.dot(k_cat, w_q, preferred_element_type=F32)

    for b, g in chains[:SCORE_LOOKAHEAD]:
        score_chain(b, g)
    late = {}

    def conv_proj():
        u = upad_ref[slot, V7X_SUBLANES:V7X_SUBLANES + seq_tile, :]
        u1 = upad_ref[slot, V7X_SUBLANES - 1:V7X_SUBLANES - 1 + seq_tile, :]
        u2 = upad_ref[slot, V7X_SUBLANES - 2:V7X_SUBLANES - 2 + seq_tile, :]
        cw = convw_ref[...]
        conv = cw[0:1] * u2 + cw[1:2] * u1 + cw[2:3] * u + convb_ref[...]
        conv_n = _rms(cb_ref[slot] * conv, onc_ref[...])
        late["y_conv"] = jnp.dot(conv_n.astype(BF16), wout_ref[ATTN_WIDTH:ATTN_WIDTH + CONV_WIDTH, :],
                                 preferred_element_type=F32)

    chain_fill[CONV_PROJ_CHAIN].insert(0, conv_proj)
    mem_vt = vt_ref[0]
    mem_units = [(c, h) for c in range(n_halves) for h in range(N_MEM_HEADS)]
    mem_scores, mem_o = {}, {}

    def mem_score(c, h):
        mem_scores[c, h] = jnp.dot(kblk_ref[0, h * N_MEM:(h + 1) * N_MEM, :],
                                   qmnt_ref[slot, :, c * V7X_MXU_DIM:(c + 1) * V7X_MXU_DIM],
                                   preferred_element_type=F32)

    for unit in mem_units[:MEM_SCORE_LOOKAHEAD]:
        mem_score(*unit)

    def mem_softmax(c, h):
        s = mem_scores.pop((c, h))
        m = jnp.max(s, axis=0, keepdims=True)
        p = jnp.exp2(s - m)
        inv_l = 1.0 / jnp.sum(p, axis=0, keepdims=True)
        vt_h = mem_vt[h * HEAD_DIM:(h + 1) * HEAD_DIM]
        mem_o[c, h] = jnp.dot(vt_h, p.astype(BF16), preferred_element_type=F32) * inv_l

    def finish_rows(c):
        rows = slice(c * V7X_MXU_DIM, (c + 1) * V7X_MXU_DIM)
        blocks = range(c * V7X_MXU_DIM // BLOCK, (c + 1) * V7X_MXU_DIM // BLOCK)
        attn_rows = jnp.concatenate(
            [jnp.concatenate([row for g in range(N_KV_HEADS) for row in head_rows[b, g]], axis=0)
             for b in blocks], axis=1).T
        mem_rows = jnp.concatenate([mem_o[c, h] for h in range(N_MEM_HEADS)], axis=0).T
        y = late["y_conv"][rows] + jnp.dot(_rms(attn_rows, ona_ref[...]).astype(BF16), wout_ref[0:ATTN_WIDTH, :],
                                   preferred_element_type=F32)
        y = y + jnp.dot(_rms(mem_rows, onm_ref[...]).astype(BF16), wout_ref[ATTN_WIDTH + CONV_WIDTH:, :],
                        preferred_element_type=F32)
        out_ref[0, rows, :] = xres_ref[slot, rows, :] + y

    for piece in norms:
        piece()
    head_rows = {}
    chains_per_half = len(chains) // n_halves
    assert len(mem_units) == len(chains)
    for i, (b, g) in enumerate(chains):
        heads = range(g * GQA_GROUP, (g + 1) * GQA_GROUP)
        dist_b = dist_first if b == 0 else dist
        neg_slope = jnp.concatenate(
            [jnp.full((1, BLOCK), -(2.0 ** (-8.0 * (h + 1) / N_ATTN_HEADS)) * LOG2E, F32) for h in heads], axis=1)
        sink = jnp.concatenate([jnp.full((1, BLOCK), sinks_ref[h], F32) for h in heads], axis=1) * LOG2E
        if i + SCORE_LOOKAHEAD < len(chains):
            score_chain(*chains[i + SCORE_LOOKAHEAD])
        s2 = scores.pop((b, g))
        s = jnp.where(from_prev, s2[:BLOCK], s2[BLOCK:]) + dist_b * neg_slope
        m = jnp.maximum(jnp.max(s, axis=0, keepdims=True), sink)
        p = jnp.exp2(s - m)
        l = jnp.sum(p, axis=0, keepdims=True) + jnp.exp2(sink - m)
        p_t = jnp.concatenate(
            [jnp.where(from_prev, p, 0.0).astype(BF16), jnp.where(from_prev, 0.0, p).astype(BF16)],
            axis=0)
        for piece in chain_fill[i]:
            piece()
        vt_bg = vtpad_ref[slot, g * HEAD_DIM:(g + 1) * HEAD_DIM, b * BLOCK:(b + 2) * BLOCK]
        o_t = jnp.dot(vt_bg, p_t, preferred_element_type=F32) * (1.0 / l)
        head_rows[b, g] = [o_t[:, hh * BLOCK:(hh + 1) * BLOCK] for hh in range(GQA_GROUP)]
        if i + MEM_SCORE_LOOKAHEAD < len(mem_units):
            mem_score(*mem_units[i + MEM_SCORE_LOOKAHEAD])
        mem_softmax(*mem_units[i])
        if i == chains_per_half:
            finish_rows(0)
    for piece in tail_dots + tail_posts:
        piece()
    finish_rows(1)


def _ffn_kernel(x_ref, gain_ref, wg_ref, wu_ref, wd_ref, out_ref):
    x = x_ref[...]
    piece_rows = V7X_MXU_DIM
    h_pieces = [_rms(x[r:r + piece_rows], gain_ref[...]).astype(BF16)
                for r in range(0, x.shape[0], piece_rows)]
    h = jnp.concatenate(h_pieces, axis=0)
    y = x
    for c in range(wg_ref.shape[1] // V7X_MXU_DIM):
        cols = slice(c * V7X_MXU_DIM, (c + 1) * V7X_MXU_DIM)
        if c == 0:
            gate = jnp.concatenate(
                [jnp.dot(hp, wg_ref[:, cols], preferred_element_type=F32) for hp in h_pieces], axis=0)
            up = jnp.concatenate(
                [jnp.dot(hp, wu_ref[:, cols], preferred_element_type=F32) for hp in h_pieces], axis=0)
        else:
            gate = jnp.dot(h, wg_ref[:, cols], preferred_element_type=F32)
            up = jnp.dot(h, wu_ref[:, cols], preferred_element_type=F32)
        act = (gate * jax.nn.sigmoid(gate) * up).astype(BF16)
        y = y + jnp.dot(act, wd_ref[cols, :], preferred_element_type=F32)
    out_ref[...] = y


def _const_spec(shape):
    return pl.BlockSpec(shape, lambda *_: (0,) * len(shape), pipeline_mode=pl.Buffered(1))


def _row(a):
    return a.reshape(1, -1)


def _layer(x, mem_blocks, lp, plan):
    seq_tile, ffn_tile, vmem_limit = plan
    batch, seq, _ = x.shape
    kblk, mem_vt = mem_blocks
    d_ff = lp["w_gate"].shape[1]

    tiles_per_seq = seq // seq_tile
    n_tiles = batch * tiles_per_seq
    slab_spec = pl.BlockSpec((D_MODEL // n_tiles, d_ff), lambda t: (t, 0))
    down_slabs = d_ff // BLOCK
    assert down_slabs <= n_tiles
    down_spec = pl.BlockSpec((BLOCK, D_MODEL), lambda t: (jnp.minimum(t, down_slabs - 1), 0))
    up_w16 = jax.ShapeDtypeStruct((D_MODEL, d_ff), BF16)
    down_w16 = jax.ShapeDtypeStruct((d_ff, D_MODEL), BF16)

    def tile_index(t):
        return (t // tiles_per_seq, t % tiles_per_seq, 0)

    mixer = pl.pallas_call(
        functools.partial(_mixer_kernel, seq_tile, tiles_per_seq),
        out_shape=(jax.ShapeDtypeStruct(x.shape, F32), up_w16, up_w16, down_w16),
        grid=(n_tiles,),
        in_specs=[
            pl.BlockSpec(memory_space=pltpu.SMEM),
            _const_spec((1, seq_tile, D_MODEL)),
            pl.BlockSpec((1, seq_tile, D_MODEL), lambda t: tile_index(jnp.minimum(t + 1, n_tiles - 1))),
            pl.BlockSpec((1, N_MEM_HEADS * N_MEM, MEM_WIDTH), lambda t: (t // tiles_per_seq, 0, 0)),
            pl.BlockSpec((1, MEM_WIDTH, N_MEM), lambda t: (t // tiles_per_seq, 0, 0)),
            _const_spec((D_MODEL, IN_PROJ_WIDTH)),
            _const_spec((D_MODEL, D_MODEL)),
            _const_spec((1, D_MODEL)),
            _const_spec((1, HEAD_DIM)),
            _const_spec((1, HEAD_DIM)),
            _const_spec((1, HEAD_DIM)),
            _const_spec((CONV_K, CONV_WIDTH)),
            _const_spec((1, CONV_WIDTH)),
            _const_spec((1, ATTN_WIDTH)),
            _const_spec((1, CONV_WIDTH)),
            _const_spec((1, MEM_WIDTH)),
            slab_spec, slab_spec, down_spec,
        ],
        out_specs=(pl.BlockSpec((1, seq_tile, D_MODEL), tile_index), slab_spec, slab_spec, down_spec),
        scratch_shapes=[
            pltpu.VMEM((2, ATTN_WIDTH, seq_tile), BF16),
            pltpu.VMEM((2, MEM_WIDTH, seq_tile), BF16),
            pltpu.VMEM((2, seq_tile + BLOCK, KV_WIDTH), BF16),
            pltpu.VMEM((2, KV_WIDTH, seq_tile + BLOCK), BF16),
            pltpu.VMEM((2, seq_tile + V7X_SUBLANES, CONV_WIDTH), F32),
            pltpu.VMEM((2, seq_tile, CONV_WIDTH), F32),
            pltpu.VMEM((2, seq_tile, D_MODEL), F32),
            pltpu.VMEM((D_MODEL, IN_PROJ_WIDTH), BF16),
            pltpu.VMEM((D_MODEL, D_MODEL), BF16),
        ],
        compiler_params=pltpu.CompilerParams(
            dimension_semantics=("arbitrary",), vmem_limit_bytes=vmem_limit),
        name="mixer",
    )
    x1, w_gate16, w_up16, w_down16 = mixer(
        lp["attn_sinks"], x, x, kblk, mem_vt, lp["w_in"], lp["w_out"],
        _row(lp["norm_mix"]), _row(lp["q_norm"]), _row(lp["k_norm"]), _row(lp["mem_q_norm"]),
        lp["conv_w"], _row(lp["conv_b"]),
        _row(lp["out_norm_attn"]), _row(lp["out_norm_conv"]), _row(lp["out_norm_mem"]),
        lp["w_gate"], lp["w_up"], lp["w_down"])

    tokens = batch * seq
    ffn = pl.pallas_call(
        _ffn_kernel,
        out_shape=jax.ShapeDtypeStruct((tokens, D_MODEL), F32),
        grid=(tokens // ffn_tile,),
        in_specs=[
            pl.BlockSpec((ffn_tile, D_MODEL), lambda i: (i, 0)),
            _const_spec((1, D_MODEL)),
            _const_spec((D_MODEL, d_ff)),
            _const_spec((D_MODEL, d_ff)),
            _const_spec((d_ff, D_MODEL)),
        ],
        out_specs=pl.BlockSpec((ffn_tile, D_MODEL), lambda i: (i, 0)),
        compiler_params=pltpu.CompilerParams(
            dimension_semantics=("arbitrary",), vmem_limit_bytes=vmem_limit),
        name="ffn",
    )
    y = ffn(x1.reshape(tokens, D_MODEL), _row(lp["norm_ffn"]), w_gate16, w_up16, w_down16)
    return y.reshape(x.shape)


def _mem_kv(mem, lp, vmem_limit):
    batch = mem.shape[0]
    seqs_per_step = 4
    steps = batch // seqs_per_step
    kblk_shape = (batch, N_MEM_HEADS * N_MEM, MEM_WIDTH)
    vt_shape = (batch, MEM_WIDTH, N_MEM)
    call = pl.pallas_call(
        _mem_kv_kernel,
        out_shape=(jax.ShapeDtypeStruct(kblk_shape, BF16), jax.ShapeDtypeStruct(vt_shape, BF16)),
        grid=(steps,),
        in_specs=[
            pl.BlockSpec((seqs_per_step, N_MEM, D_MODEL), lambda b: (b, 0, 0)),
            _const_spec((1, D_MODEL)),
            _const_spec((D_MODEL, 2 * MEM_WIDTH)),
            _const_spec((1, HEAD_DIM)),
        ],
        out_specs=(pl.BlockSpec((seqs_per_step,) + kblk_shape[1:], lambda b: (b, 0, 0)),
                   pl.BlockSpec((seqs_per_step,) + vt_shape[1:], lambda b: (b, 0, 0))),
        compiler_params=pltpu.CompilerParams(
            dimension_semantics=("arbitrary",), vmem_limit_bytes=vmem_limit),
        name="mem_kv",
    )
    return call(mem, _row(lp["norm_mem"]), lp["w_mem_kv"], _row(lp["mem_k_norm"]))


def kernel(x, mem, norm_mix, w_in, q_norm, k_norm, attn_sinks, conv_w, conv_b, norm_mem, w_mem_kv,
           mem_q_norm, mem_k_norm, out_norm_attn, out_norm_conv, out_norm_mem, w_out, norm_ffn,
           w_gate, w_up, w_down):
    plan = _plan()
    depth = w_in.shape[0]
    for l in range(depth):
        lp = dict(
            norm_mix=norm_mix[l], w_in=w_in[l], q_norm=q_norm[l], k_norm=k_norm[l],
            attn_sinks=attn_sinks[l], conv_w=conv_w[l], conv_b=conv_b[l], norm_mem=norm_mem[l],
            w_mem_kv=w_mem_kv[l], mem_q_norm=mem_q_norm[l], mem_k_norm=mem_k_norm[l],
            out_norm_attn=out_norm_attn[l], out_norm_conv=out_norm_conv[l],
            out_norm_mem=out_norm_mem[l], w_out=w_out[l], norm_ffn=norm_ffn[l],
            w_gate=w_gate[l], w_up=w_up[l], w_down=w_down[l])
        x = _layer(x, _mem_kv(mem, lp, plan[2]), lp, plan)
    return x
```

```python
import functools

import jax
import jax.numpy as jnp
from jax import lax
from jax.experimental import pallas as pl
from jax.experimental.pallas import tpu as pltpu

D_MODEL = 1024
HEAD_DIM = 64
N_ATTN_HEADS = 8
N_KV_HEADS = 2
GQA_GROUP = N_ATTN_HEADS // N_KV_HEADS
BLOCK = 128
N_MEM_HEADS = 4
N_MEM = 256
CONV_K = 3
ATTN_WIDTH = N_ATTN_HEADS * HEAD_DIM
KV_WIDTH = N_KV_HEADS * HEAD_DIM
CONV_WIDTH = 256
MEM_WIDTH = N_MEM_HEADS * HEAD_DIM
IN_PROJ_WIDTH = ATTN_WIDTH + 2 * KV_WIDTH + 3 * CONV_WIDTH + MEM_WIDTH
EPS = 1e-6
LOG2E = 1.4426950408889634
MASKED_DIST = 2.0 ** 110

V7X_VMEM_BYTES = 64 * 1024 * 1024
V7X_SUBLANES = 8
V7X_LANES = 128
V7X_MXU_DIM = 256

F32 = jnp.float32
BF16 = jnp.bfloat16


def _plan():
    seq_tile = 4 * BLOCK
    ffn_tile = 1024
    vmem_limit = V7X_VMEM_BYTES - 8 * 1024 * 1024
    return seq_tile, ffn_tile, vmem_limit


def _rms(a, gain):
    return a * lax.rsqrt(jnp.mean(a * a, axis=-1, keepdims=True) + EPS) * gain


def _head_rms_scale(t, gmat):
    sq = (t * t).astype(BF16)
    ss = jnp.dot(sq, gmat, preferred_element_type=F32)
    return lax.rsqrt(ss * (1.0 / HEAD_DIM) + EPS)


def _head_block_ones():
    row = lax.broadcasted_iota(jnp.int32, (V7X_MXU_DIM, V7X_MXU_DIM), 0) // HEAD_DIM
    col = lax.broadcasted_iota(jnp.int32, (V7X_MXU_DIM, V7X_MXU_DIM), 1) // HEAD_DIM
    return jnp.where(row == col, 1.0, 0.0).astype(BF16)


def _head_gain_column(gain_row, heads):
    two_heads = jnp.concatenate([gain_row, gain_row], axis=1)
    col = jnp.broadcast_to(two_heads, (BLOCK, BLOCK)).T
    return jnp.concatenate([col] * (heads * HEAD_DIM // BLOCK), axis=0)


def _head_rms_rows(t, gain):
    heads = t.shape[0] // HEAD_DIM
    out = []
    for h in range(heads):
        th = t[h * HEAD_DIM:(h + 1) * HEAD_DIM]
        ss = jnp.sum(th * th, axis=0, keepdims=True)
        out.append(th * lax.rsqrt(ss * (1.0 / HEAD_DIM) + EPS) * gain[h * HEAD_DIM:(h + 1) * HEAD_DIM])
    return out


def _mem_kv_kernel(mem_ref, gain_ref, w_ref, kgain_ref, kblk_ref, vt_ref):
    n_seq = mem_ref.shape[0]
    m = mem_ref[...].reshape(n_seq * N_MEM, D_MODEL)
    mn = _rms(m, gain_ref[...]).astype(BF16)
    kv = jnp.dot(mn, w_ref[...].astype(BF16), preferred_element_type=F32)
    k = kv[:, :MEM_WIDTH]
    kgain = jnp.concatenate([kgain_ref[...]] * N_MEM_HEADS, axis=1)
    kn = k * _head_rms_scale(k, _head_block_ones()) * kgain
    lane_head = lax.broadcasted_iota(jnp.int32, (N_MEM, MEM_WIDTH), 1) // HEAD_DIM
    for s in range(n_seq):
        rows = slice(s * N_MEM, (s + 1) * N_MEM)
        kn_s = kn[rows]
        vt_ref[s] = kv[rows, MEM_WIDTH:].T.astype(BF16)
        for h in range(N_MEM_HEADS):
            kblk_ref[s, h * N_MEM:(h + 1) * N_MEM, :] = jnp.where(lane_head == h, kn_s, 0.0).astype(BF16)


ROW_CHUNK = V7X_MXU_DIM
COL_GROUP = 2 * V7X_MXU_DIM
MEM_SCORE_LOOKAHEAD = 5
SCORE_LOOKAHEAD = 5
STAGE1_HELD_PIECES = 2
CONV_PROJ_CHAIN = 3


def _stage1_pieces(x_ref, wslot, carry, refs):
    (win_ref, nmix_ref, qgain_ref, kgain_ref, mqgain_ref,
     qnt_ref, qmnt_ref, kpad_ref, vtpad_ref, upad_ref, cb_ref, xres_ref) = refs
    seq_tile = cb_ref.shape[1]
    xn, val = {}, {}
    conv_base = ATTN_WIDTH + 2 * KV_WIDTH

    def carried(ref_slice_fn, shape, dtype):
        if carry is None:
            return jnp.zeros(shape, dtype)
        prev_slot, first_of_seq = carry
        return jnp.where(first_of_seq, jnp.zeros(shape, dtype), ref_slice_fn(prev_slot))

    def heads_t(t, gain_ref, heads, out_ref, r):
        t = t.T
        gain = _head_gain_column(gain_ref[...] * (HEAD_DIM ** -0.5 * LOG2E), heads)
        gain = jnp.concatenate([gain] * (ROW_CHUNK // BLOCK), axis=1)
        out_ref[wslot, :, r * ROW_CHUNK:(r + 1) * ROW_CHUNK] = jnp.concatenate(
            _head_rms_rows(t, gain), axis=0).astype(BF16)

    norms, dots, posts, narrow_dots, narrow_posts = [], [], [], [], []
    for r in range(seq_tile // ROW_CHUNK):
        rows = slice(r * ROW_CHUNK, (r + 1) * ROW_CHUNK)

        def norm(r=r, rows=rows):
            x = x_ref[0, rows, :]
            xres_ref[wslot, rows, :] = x
            xn[r] = _rms(x, nmix_ref[...]).astype(BF16)

        def dot_piece(name, start, width, r=r):
            def run():
                val[name, r] = jnp.dot(xn[r], win_ref[:, start:start + width], preferred_element_type=F32)
            return run

        def post_q(r=r):
            heads_t(val.pop(("q", r)), qgain_ref, N_ATTN_HEADS, qnt_ref, r)

        def post_kv(r=r):
            t = val.pop(("kvch", r))
            kv_a = t[:, :2 * KV_WIDTH]
            val["ch", r] = t[:, 2 * KV_WIDTH:]
            scale = _head_rms_scale(kv_a, _head_block_ones())
            kgain = jnp.concatenate([kgain_ref[...]] * N_KV_HEADS, axis=1)
            kn = kv_a[:, :KV_WIDTH] * scale[:, :KV_WIDTH] * kgain
            kpad_ref[wslot, BLOCK + r * ROW_CHUNK:BLOCK + (r + 1) * ROW_CHUNK, :] = kn.astype(BF16)
            vtpad_ref[wslot, :, BLOCK + r * ROW_CHUNK:BLOCK + (r + 1) * ROW_CHUNK] = (
                kv_a[:, KV_WIDTH:].T.astype(BF16))
            if r == 0:
                kpad_ref[wslot, 0:BLOCK, :] = carried(
                    lambda s: kpad_ref[s, seq_tile:seq_tile + BLOCK, :], (BLOCK, KV_WIDTH), BF16)
                vtpad_ref[wslot, :, 0:BLOCK] = carried(
                    lambda s: vtpad_ref[s, :, seq_tile:seq_tile + BLOCK], (KV_WIDTH, BLOCK), BF16)

        def post_conv(r=r):
            t = val.pop(("cbcc", r))
            cb_ref[wslot, r * ROW_CHUNK:(r + 1) * ROW_CHUNK, :] = t[:, :CONV_WIDTH]
            u = t[:, CONV_WIDTH:] * val.pop(("ch", r))
            upad_ref[wslot, V7X_SUBLANES + r * ROW_CHUNK:V7X_SUBLANES + (r + 1) * ROW_CHUNK, :] = u
            if r == 0:
                upad_ref[wslot, 0:V7X_SUBLANES, :] = carried(
                    lambda s: upad_ref[s, seq_tile:seq_tile + V7X_SUBLANES, :],
                    (V7X_SUBLANES, CONV_WIDTH), F32)

        def post_qm(r=r):
            heads_t(val.pop(("qm", r)), mqgain_ref, N_MEM_HEADS, qmnt_ref, r)

        norms.append(norm)
        dots += [dot_piece("q", 0, COL_GROUP), dot_piece("kvch", ATTN_WIDTH, COL_GROUP),
                 dot_piece("cbcc", conv_base + CONV_WIDTH, COL_GROUP)]
        posts += [post_q, post_kv, post_conv]
        narrow_dots.append(dot_piece("qm", IN_PROJ_WIDTH - MEM_WIDTH, MEM_WIDTH))
        narrow_posts.append(post_qm)
    return norms, dots + narrow_dots, posts + narrow_posts


def _mixer_kernel(seq_tile, tiles_per_seq,
                  sinks_ref, xfirst_ref, xnext_ref, kblk_ref, vt_ref, win32_ref, wout32_ref,
                  nmix_ref, qgain_ref, kgain_ref, mqgain_ref, convw_ref, convb_ref,
                  ona_ref, onc_ref, onm_ref, wg_ref, wu_ref, wd_ref,
                  out_ref, wg16_ref, wu16_ref, wd16_ref,
                  qnt_ref, qmnt_ref, kpad_ref, vtpad_ref, upad_ref, cb_ref, xres_ref, win_ref, wout_ref):
    wg16_ref[...] = wg_ref[...].astype(BF16)
    wu16_ref[...] = wu_ref[...].astype(BF16)
    wd16_ref[...] = wd_ref[...].astype(BF16)
    t = pl.program_id(0)
    slot = t % 2
    n_blocks = seq_tile // BLOCK
    first_of_seq = (t % tiles_per_seq) == 0
    stage1_refs = (win_ref, nmix_ref, qgain_ref, kgain_ref, mqgain_ref,
                   qnt_ref, qmnt_ref, kpad_ref, vtpad_ref, upad_ref, cb_ref, xres_ref)

    @pl.when(t == 0)
    def _():
        win_ref[...] = win32_ref[...].astype(BF16)
        wout_ref[...] = wout32_ref[...].astype(BF16)
        norms, dots, posts = _stage1_pieces(xfirst_ref, 0, None, stage1_refs)
        for piece in norms + [p for pair in zip(dots, posts) for p in pair]:
            piece()

    norms, dots, posts = _stage1_pieces(
        xnext_ref, 1 - slot, (slot, ((t + 1) % tiles_per_seq) == 0), stage1_refs)
    n_chains = n_blocks * N_KV_HEADS
    n_early = len(dots) - STAGE1_HELD_PIECES
    assert n_early < n_chains
    chain_fill = [[] for _ in range(n_chains)]
    chain_fill[0] = [dots[0], dots[1]]
    for i in range(1, n_early + 1):
        chain_fill[i] = [posts[i - 1]] + ([dots[i + 1]] if i + 1 < n_early else [])
    tail_dots, tail_posts = dots[n_early:], posts[n_early:]

    key = lax.broadcasted_iota(jnp.int32, (BLOCK, GQA_GROUP * BLOCK), 0)
    qry = lax.broadcasted_iota(jnp.int32, (BLOCK, GQA_GROUP * BLOCK), 1) % BLOCK
    from_prev = key > qry
    dist = jnp.where(from_prev, qry + BLOCK - key, qry - key).astype(F32)
    dist_first = jnp.where(jnp.logical_and(from_prev, first_of_seq), MASKED_DIST, dist)
    zeros_q = jnp.zeros((HEAD_DIM, GQA_GROUP * BLOCK), BF16)

    chains = [(b, g) for b in range(n_blocks) for g in range(N_KV_HEADS)]
    n_halves = seq_tile // V7X_MXU_DIM

    scores = {}

    def score_chain(b, g):
        cols = slice(b * BLOCK, (b + 1) * BLOCK)
        k_cat = kpad_ref[slot, b * BLOCK:(b + 2) * BLOCK, :]
        heads = range(g * GQA_GROUP, (g + 1) * GQA_GROUP)
        q4 = jnp.concatenate(
            [qnt_ref[slot, h * HEAD_DIM:(h + 1) * HEAD_DIM, cols] for h in heads], axis=1)
        w_q = jnp.concatenate([q4, zeros_q] if g == 0 else [zeros_q, q4], axis=0)
        scores[b, g] = jnp.dot(k_cat, w_q, preferred_element_type=F32)

    for b, g in chains[:SCORE_LOOKAHEAD]:
        score_chain(b, g)
    late = {}

    def conv_proj():
        u = upad_ref[slot, V7X_SUBLANES:V7X_SUBLANES + seq_tile, :]
        u1 = upad_ref[slot, V7X_SUBLANES - 1:V7X_SUBLANES - 1 + seq_tile, :]
        u2 = upad_ref[slot, V7X_SUBLANES - 2:V7X_SUBLANES - 2 + seq_tile, :]
        cw = convw_ref[...]
        conv = cw[0:1] * u2 + cw[1:2] * u1 + cw[2:3] * u + convb_ref[...]
        conv_n = _rms(cb_ref[slot] * conv, onc_ref[...])
        late["y_conv"] = jnp.dot(conv_n.astype(BF16), wout_ref[ATTN_WIDTH:ATTN_WIDTH + CONV_WIDTH, :],
                                 preferred_element_type=F32)

    chain_fill[CONV_PROJ_CHAIN].insert(0, conv_proj)
    mem_vt = vt_ref[0]
    mem_units = [(c, h) for c in range(n_halves) for h in range(N_MEM_HEADS)]
    mem_scores, mem_o = {}, {}

    def mem_score(c, h):
        mem_scores[c, h] = jnp.dot(kblk_ref[0, h * N_MEM:(h + 1) * N_MEM, :],
                                   qmnt_ref[slot, :, c * V7X_MXU_DIM:(c + 1) * V7X_MXU_DIM],
                                   preferred_element_type=F32)

    for unit in mem_units[:MEM_SCORE_LOOKAHEAD]:
        mem_score(*unit)

    def mem_softmax(c, h):
        s = mem_scores.pop((c, h))
        m = jnp.max(s, axis=0, keepdims=True)
        p = jnp.exp2(s - m)
        inv_l = 1.0 / jnp.sum(p, axis=0, keepdims=True)
        vt_h = mem_vt[h * HEAD_DIM:(h + 1) * HEAD_DIM]
        mem_o[c, h] = jnp.dot(vt_h, p.astype(BF16), preferred_element_type=F32) * inv_l

    def finish_rows(c):
        rows = slice(c * V7X_MXU_DIM, (c + 1) * V7X_MXU_DIM)
        blocks = range(c * V7X_MXU_DIM // BLOCK, (c + 1) * V7X_MXU_DIM // BLOCK)
        attn_rows = jnp.concatenate(
            [jnp.concatenate([row for g in range(N_KV_HEADS) for row in head_rows[b, g]], axis=0)
             for b in blocks], axis=1).T
        mem_rows = jnp.concatenate([mem_o[c, h] for h in range(N_MEM_HEADS)], axis=0).T
        y = late["y_conv"][rows] + jnp.dot(_rms(attn_rows, ona_ref[...]).astype(BF16), wout_ref[0:ATTN_WIDTH, :],
                                   preferred_element_type=F32)
        y = y + jnp.dot(_rms(mem_rows, onm_ref[...]).astype(BF16), wout_ref[ATTN_WIDTH + CONV_WIDTH:, :],
                        preferred_element_type=F32)
        out_ref[0, rows, :] = xres_ref[slot, rows, :] + y

    for piece in norms:
        piece()
    head_rows = {}
    chains_per_half = len(chains) // n_halves
    assert len(mem_units) == len(chains)
    for i, (b, g) in enumerate(chains):
        heads = range(g * GQA_GROUP, (g + 1) * GQA_GROUP)
        dist_b = dist_first if b == 0 else dist
        neg_slope = jnp.concatenate(
            [jnp.full((1, BLOCK), -(2.0 ** (-8.0 * (h + 1) / N_ATTN_HEADS)) * LOG2E, F32) for h in heads], axis=1)
        sink = jnp.concatenate([jnp.full((1, BLOCK), sinks_ref[h], F32) for h in heads], axis=1) * LOG2E
        if i + SCORE_LOOKAHEAD < len(chains):
            score_chain(*chains[i + SCORE_LOOKAHEAD])
        s2 = scores.pop((b, g))
        s = jnp.where(from_prev, s2[:BLOCK], s2[BLOCK:]) + dist_b * neg_slope
        m = jnp.maximum(jnp.max(s, axis=0, keepdims=True), sink)
        p = jnp.exp2(s - m)
        l = jnp.sum(p, axis=0, keepdims=True) + jnp.exp2(sink - m)
        p_t = jnp.concatenate(
            [jnp.where(from_prev, p, 0.0).astype(BF16), jnp.where(from_prev, 0.0, p).astype(BF16)],
            axis=0)
        for piece in chain_fill[i]:
            piece()
        vt_bg = vtpad_ref[slot, g * HEAD_DIM:(g + 1) * HEAD_DIM, b * BLOCK:(b + 2) * BLOCK]
        o_t = jnp.dot(vt_bg, p_t, preferred_element_type=F32) * (1.0 / l)
        head_rows[b, g] = [o_t[:, hh * BLOCK:(hh + 1) * BLOCK] for hh in range(GQA_GROUP)]
        if i + MEM_SCORE_LOOKAHEAD < len(mem_units):
            mem_score(*mem_units[i + MEM_SCORE_LOOKAHEAD])
        mem_softmax(*mem_units[i])
        if i == chains_per_half:
            finish_rows(0)
    for piece in tail_dots + tail_posts:
        piece()
    finish_rows(1)


def _ffn_kernel(x_ref, gain_ref, wg_ref, wu_ref, wd_ref, out_ref):
    x = x_ref[...]
    piece_rows = V7X_MXU_DIM
    first = slice(0, V7X_MXU_DIM)
    h_pieces, gate_pieces, up_pieces = [], [], []
    for r in range(0, x.shape[0], piece_rows):
        hp = _rms(x[r:r + piece_rows], gain_ref[...]).astype(BF16)
        h_pieces.append(hp)
        gate_pieces.append(jnp.dot(hp, wg_ref[:, first], preferred_element_type=F32))
        up_pieces.append(jnp.dot(hp, wu_ref[:, first], preferred_element_type=F32))
    h = jnp.concatenate(h_pieces, axis=0)
    y = x
    for c in range(wg_ref.shape[1] // V7X_MXU_DIM):
        cols = slice(c * V7X_MXU_DIM, (c + 1) * V7X_MXU_DIM)
        if c == 0:
            gate = jnp.concatenate(gate_pieces, axis=0)
            up = jnp.concatenate(up_pieces, axis=0)
        else:
            gate = jnp.dot(h, wg_ref[:, cols], preferred_element_type=F32)
            up = jnp.dot(h, wu_ref[:, cols], preferred_element_type=F32)
        act = (gate * jax.nn.sigmoid(gate) * up).astype(BF16)
        y = y + jnp.dot(act, wd_ref[cols, :], preferred_element_type=F32)
    out_ref[...] = y


def _const_spec(shape):
    return pl.BlockSpec(shape, lambda *_: (0,) * len(shape), pipeline_mode=pl.Buffered(1))


def _row(a):
    return a.reshape(1, -1)


def _layer(x, mem_blocks, lp, plan):
    seq_tile, ffn_tile, vmem_limit = plan
    batch, seq, _ = x.shape
    kblk, mem_vt = mem_blocks
    d_ff = lp["w_gate"].shape[1]

    tiles_per_seq = seq // seq_tile
    n_tiles = batch * tiles_per_seq
    slab_spec = pl.BlockSpec((D_MODEL // n_tiles, d_ff), lambda t: (t, 0))
    down_slabs = d_ff // BLOCK
    assert down_slabs <= n_tiles
    down_spec = pl.BlockSpec((BLOCK, D_MODEL), lambda t: (jnp.minimum(t, down_slabs - 1), 0))
    up_w16 = jax.ShapeDtypeStruct((D_MODEL, d_ff), BF16)
    down_w16 = jax.ShapeDtypeStruct((d_ff, D_MODEL), BF16)

    def tile_index(t):
        return (t // tiles_per_seq, t % tiles_per_seq, 0)

    mixer = pl.pallas_call(
        functools.partial(_mixer_kernel, seq_tile, tiles_per_seq),
        out_shape=(jax.ShapeDtypeStruct(x.shape, F32), up_w16, up_w16, down_w16),
        grid=(n_tiles,),
        in_specs=[
            pl.BlockSpec(memory_space=pltpu.SMEM),
            _const_spec((1, seq_tile, D_MODEL)),
            pl.BlockSpec((1, seq_tile, D_MODEL), lambda t: tile_index(jnp.minimum(t + 1, n_tiles - 1))),
            pl.BlockSpec((1, N_MEM_HEADS * N_MEM, MEM_WIDTH), lambda t: (t // tiles_per_seq, 0, 0)),
            pl.BlockSpec((1, MEM_WIDTH, N_MEM), lambda t: (t // tiles_per_seq, 0, 0)),
            _const_spec((D_MODEL, IN_PROJ_WIDTH)),
            _const_spec((D_MODEL, D_MODEL)),
            _const_spec((1, D_MODEL)),
            _const_spec((1, HEAD_DIM)),
            _const_spec((1, HEAD_DIM)),
            _const_spec((1, HEAD_DIM)),
            _const_spec((CONV_K, CONV_WIDTH)),
            _const_spec((1, CONV_WIDTH)),
            _const_spec((1, ATTN_WIDTH)),
            _const_spec((1, CONV_WIDTH)),
            _const_spec((1, MEM_WIDTH)),
            slab_spec, slab_spec, down_spec,
        ],
        out_specs=(pl.BlockSpec((1, seq_tile, D_MODEL), tile_index), slab_spec, slab_spec, down_spec),
        scratch_shapes=[
            pltpu.VMEM((2, ATTN_WIDTH, seq_tile), BF16),
            pltpu.VMEM((2, MEM_WIDTH, seq_tile), BF16),
            pltpu.VMEM((2, seq_tile + BLOCK, KV_WIDTH), BF16),
            pltpu.VMEM((2, KV_WIDTH, seq_tile + BLOCK), BF16),
            pltpu.VMEM((2, seq_tile + V7X_SUBLANES, CONV_WIDTH), F32),
            pltpu.VMEM((2, seq_tile, CONV_WIDTH), F32),
            pltpu.VMEM((2, seq_tile, D_MODEL), F32),
            pltpu.VMEM((D_MODEL, IN_PROJ_WIDTH), BF16),
            pltpu.VMEM((D_MODEL, D_MODEL), BF16),
        ],
        compiler_params=pltpu.CompilerParams(
            dimension_semantics=("arbitrary",), vmem_limit_bytes=vmem_limit),
        name="mixer",
    )
    x1, w_gate16, w_up16, w_down16 = mixer(
        lp["attn_sinks"], x, x, kblk, mem_vt, lp["w_in"], lp["w_out"],
        _row(lp["norm_mix"]), _row(lp["q_norm"]), _row(lp["k_norm"]), _row(lp["mem_q_norm"]),
        lp["conv_w"], _row(lp["conv_b"]),
        _row(lp["out_norm_attn"]), _row(lp["out_norm_conv"]), _row(lp["out_norm_mem"]),
        lp["w_gate"], lp["w_up"], lp["w_down"])

    tokens = batch * seq
    ffn = pl.pallas_call(
        _ffn_kernel,
        out_shape=jax.ShapeDtypeStruct((tokens, D_MODEL), F32),
        grid=(tokens // ffn_tile,),
        in_specs=[
            pl.BlockSpec((ffn_tile, D_MODEL), lambda i: (i, 0)),
            _const_spec((1, D_MODEL)),
            _const_spec((D_MODEL, d_ff)),
            _const_spec((D_MODEL, d_ff)),
            _const_spec((d_ff, D_MODEL)),
        ],
        out_specs=pl.BlockSpec((ffn_tile, D_MODEL), lambda i: (i, 0)),
        compiler_params=pltpu.CompilerParams(
            dimension_semantics=("arbitrary",), vmem_limit_bytes=vmem_limit),
        name="ffn",
    )
    y = ffn(x1.reshape(tokens, D_MODEL), _row(lp["norm_ffn"]), w_gate16, w_up16, w_down16)
    return y.reshape(x.shape)


def _mem_kv(mem, lp, vmem_limit):
    batch = mem.shape[0]
    seqs_per_step = 4
    steps = batch // seqs_per_step
    kblk_shape = (batch, N_MEM_HEADS * N_MEM, MEM_WIDTH)
    vt_shape = (batch, MEM_WIDTH, N_MEM)
    call = pl.pallas_call(
        _mem_kv_kernel,
        out_shape=(jax.ShapeDtypeStruct(kblk_shape, BF16), jax.ShapeDtypeStruct(vt_shape, BF16)),
        grid=(steps,),
        in_specs=[
            pl.BlockSpec((seqs_per_step, N_MEM, D_MODEL), lambda b: (b, 0, 0)),
            _const_spec((1, D_MODEL)),
            _const_spec((D_MODEL, 2 * MEM_WIDTH)),
            _const_spec((1, HEAD_DIM)),
        ],
        out_specs=(pl.BlockSpec((seqs_per_step,) + kblk_shape[1:], lambda b: (b, 0, 0)),
                   pl.BlockSpec((seqs_per_step,) + vt_shape[1:], lambda b: (b, 0, 0))),
        compiler_params=pltpu.CompilerParams(
            dimension_semantics=("arbitrary",), vmem_limit_bytes=vmem_limit),
        name="mem_kv",
    )
    return call(mem, _row(lp["norm_mem"]), lp["w_mem_kv"], _row(lp["mem_k_norm"]))


def kernel(x, mem, norm_mix, w_in, q_norm, k_norm, attn_sinks, conv_w, conv_b, norm_mem, w_mem_kv,
           mem_q_norm, mem_k_norm, out_norm_attn, out_norm_conv, out_norm_mem, w_out, norm_ffn,
           w_gate, w_up, w_down):
    plan = _plan()
    depth = w_in.shape[0]
    for l in range(depth):
        lp = dict(
            norm_mix=norm_mix[l], w_in=w_in[l], q_norm=q_norm[l], k_norm=k_norm[l],
            attn_sinks=attn_sinks[l], conv_w=conv_w[l], conv_b=conv_b[l], norm_mem=norm_mem[l],
            w_mem_kv=w_mem_kv[l], mem_q_norm=mem_q_norm[l], mem_k_norm=mem_k_norm[l],
            out_norm_attn=out_norm_attn[l], out_norm_conv=out_norm_conv[l],
            out_norm_mem=out_norm_mem[l], w_out=w_out[l], norm_ffn=norm_ffn[l],
            w_gate=w_gate[l], w_up=w_up[l], w_down=w_down[l])
        x = _layer(x, _mem_kv(mem, lp, plan[2]), lp, plan)
    return x
```

```python
import functools

import jax
import jax.numpy as jnp
from jax import lax
from jax.experimental import pallas as pl
from jax.experimental.pallas import tpu as pltpu

D_MODEL = 1024
HEAD_DIM = 64
N_ATTN_HEADS = 8
N_KV_HEADS = 2
GQA_GROUP = N_ATTN_HEADS // N_KV_HEADS
BLOCK = 128
N_MEM_HEADS = 4
N_MEM = 256
CONV_K = 3
ATTN_WIDTH = N_ATTN_HEADS * HEAD_DIM
KV_WIDTH = N_KV_HEADS * HEAD_DIM
CONV_WIDTH = 256
MEM_WIDTH = N_MEM_HEADS * HEAD_DIM
IN_PROJ_WIDTH = ATTN_WIDTH + 2 * KV_WIDTH + 3 * CONV_WIDTH + MEM_WIDTH
EPS = 1e-6
LOG2E = 1.4426950408889634
MASKED_DIST = 2.0 ** 110

V7X_VMEM_BYTES = 64 * 1024 * 1024
V7X_SUBLANES = 8
V7X_LANES = 128
V7X_MXU_DIM = 256

F32 = jnp.float32
BF16 = jnp.bfloat16


def _plan():
    seq_tile = 4 * BLOCK
    ffn_tile = 1024
    vmem_limit = V7X_VMEM_BYTES - 8 * 1024 * 1024
    return seq_tile, ffn_tile, vmem_limit


def _rms(a, gain):
    return a * lax.rsqrt(jnp.mean(a * a, axis=-1, keepdims=True) + EPS) * gain


def _head_rms_scale(t, gmat):
    sq = (t * t).astype(BF16)
    ss = jnp.dot(sq, gmat, preferred_element_type=F32)
    return lax.rsqrt(ss * (1.0 / HEAD_DIM) + EPS)


def _head_block_ones():
    row = lax.broadcasted_iota(jnp.int32, (V7X_MXU_DIM, V7X_MXU_DIM), 0) // HEAD_DIM
    col = lax.broadcasted_iota(jnp.int32, (V7X_MXU_DIM, V7X_MXU_DIM), 1) // HEAD_DIM
    return jnp.where(row == col, 1.0, 0.0).astype(BF16)


def _head_gain_column(gain_row, heads):
    two_heads = jnp.concatenate([gain_row, gain_row], axis=1)
    col = jnp.broadcast_to(two_heads, (BLOCK, BLOCK)).T
    return jnp.concatenate([col] * (heads * HEAD_DIM // BLOCK), axis=0)


def _head_rms_rows(t, gain):
    heads = t.shape[0] // HEAD_DIM
    out = []
    for h in range(heads):
        th = t[h * HEAD_DIM:(h + 1) * HEAD_DIM]
        ss = jnp.sum(th * th, axis=0, keepdims=True)
        out.append(th * lax.rsqrt(ss * (1.0 / HEAD_DIM) + EPS) * gain[h * HEAD_DIM:(h + 1) * HEAD_DIM])
    return out


def _mem_kv_kernel(mem_ref, gain_ref, w_ref, kgain_ref, kblk_ref, vt_ref):
    n_seq = mem_ref.shape[0]
    m = mem_ref[...].reshape(n_seq * N_MEM, D_MODEL)
    mn = _rms(m, gain_ref[...]).astype(BF16)
    kv = jnp.dot(mn, w_ref[...].astype(BF16), preferred_element_type=F32)
    k = kv[:, :MEM_WIDTH]
    kgain = jnp.concatenate([kgain_ref[...]] * N_MEM_HEADS, axis=1)
    kn = k * _head_rms_scale(k, _head_block_ones()) * kgain
    lane_head = lax.broadcasted_iota(jnp.int32, (N_MEM, MEM_WIDTH), 1) // HEAD_DIM
    for s in range(n_seq):
        rows = slice(s * N_MEM, (s + 1) * N_MEM)
        kn_s = kn[rows]
        vt_ref[s] = kv[rows, MEM_WIDTH:].T.astype(BF16)
        for h in range(N_MEM_HEADS):
            kblk_ref[s, h * N_MEM:(h + 1) * N_MEM, :] = jnp.where(lane_head == h, kn_s, 0.0).astype(BF16)


ROW_CHUNK = V7X_MXU_DIM
COL_GROUP = 2 * V7X_MXU_DIM
MEM_SCORE_LOOKAHEAD = 5
SCORE_LOOKAHEAD = 5
STAGE1_HELD_PIECES = 2
CONV_PROJ_CHAIN = 3


def _stage1_pieces(x_ref, wslot, carry, refs):
    (win_ref, nmix_ref, qgain_ref, kgain_ref, mqgain_ref,
     qnt_ref, qmnt_ref, kpad_ref, vtpad_ref, upad_ref, cb_ref, xres_ref) = refs
    seq_tile = cb_ref.shape[1]
    xn, val = {}, {}
    conv_base = ATTN_WIDTH + 2 * KV_WIDTH

    def carried(ref_slice_fn, shape, dtype):
        if carry is None:
            return jnp.zeros(shape, dtype)
        prev_slot, first_of_seq = carry
        return jnp.where(first_of_seq, jnp.zeros(shape, dtype), ref_slice_fn(prev_slot))

    def heads_t(t, gain_ref, heads, out_ref, r):
        t = t.T
        gain = _head_gain_column(gain_ref[...] * (HEAD_DIM ** -0.5 * LOG2E), heads)
        gain = jnp.concatenate([gain] * (ROW_CHUNK // BLOCK), axis=1)
        out_ref[wslot, :, r * ROW_CHUNK:(r + 1) * ROW_CHUNK] = jnp.concatenate(
            _head_rms_rows(t, gain), axis=0).astype(BF16)

    norms, dots, posts, narrow_dots, narrow_posts = [], [], [], [], []
    for r in range(seq_tile // ROW_CHUNK):
        rows = slice(r * ROW_CHUNK, (r + 1) * ROW_CHUNK)

        def norm(r=r, rows=rows):
            x = x_ref[0, rows, :]
            xres_ref[wslot, rows, :] = x
            xn[r] = _rms(x, nmix_ref[...]).astype(BF16)

        def dot_piece(name, start, width, r=r):
            def run():
                val[name, r] = jnp.dot(xn[r], win_ref[:, start:start + width], preferred_element_type=F32)
            return run

        def post_q(r=r):
            heads_t(val.pop(("q", r)), qgain_ref, N_ATTN_HEADS, qnt_ref, r)

        def post_kv(r=r):
            t = val.pop(("kvch", r))
            kv_a = t[:, :2 * KV_WIDTH]
            val["ch", r] = t[:, 2 * KV_WIDTH:]
            scale = _head_rms_scale(kv_a, _head_block_ones())
            kgain = jnp.concatenate([kgain_ref[...]] * N_KV_HEADS, axis=1)
            kn = kv_a[:, :KV_WIDTH] * scale[:, :KV_WIDTH] * kgain
            kpad_ref[wslot, BLOCK + r * ROW_CHUNK:BLOCK + (r + 1) * ROW_CHUNK, :] = kn.astype(BF16)
            vtpad_ref[wslot, :, BLOCK + r * ROW_CHUNK:BLOCK + (r + 1) * ROW_CHUNK] = (
                kv_a[:, KV_WIDTH:].T.astype(BF16))
            if r == 0:
                kpad_ref[wslot, 0:BLOCK, :] = carried(
                    lambda s: kpad_ref[s, seq_tile:seq_tile + BLOCK, :], (BLOCK, KV_WIDTH), BF16)
                vtpad_ref[wslot, :, 0:BLOCK] = carried(
                    lambda s: vtpad_ref[s, :, seq_tile:seq_tile + BLOCK], (KV_WIDTH, BLOCK), BF16)

        def post_conv(r=r):
            t = val.pop(("cbcc", r))
            cb_ref[wslot, r * ROW_CHUNK:(r + 1) * ROW_CHUNK, :] = t[:, :CONV_WIDTH]
            u = t[:, CONV_WIDTH:] * val.pop(("ch", r))
            upad_ref[wslot, V7X_SUBLANES + r * ROW_CHUNK:V7X_SUBLANES + (r + 1) * ROW_CHUNK, :] = u
            if r == 0:
                upad_ref[wslot, 0:V7X_SUBLANES, :] = carried(
                    lambda s: upad_ref[s, seq_tile:seq_tile + V7X_SUBLANES, :],
                    (V7X_SUBLANES, CONV_WIDTH), F32)

        def post_qm(r=r):
            heads_t(val.pop(("qm", r)), mqgain_ref, N_MEM_HEADS, qmnt_ref, r)

        norms.append(norm)
        dots += [dot_piece("q", 0, COL_GROUP), dot_piece("kvch", ATTN_WIDTH, COL_GROUP),
                 dot_piece("cbcc", conv_base + CONV_WIDTH, COL_GROUP)]
        posts += [post_q, post_kv, post_conv]
        narrow_dots.append(dot_piece("qm", IN_PROJ_WIDTH - MEM_WIDTH, MEM_WIDTH))
        narrow_posts.append(post_qm)
    return norms, dots + narrow_dots, posts + narrow_posts


def _mixer_kernel(seq_tile, tiles_per_seq,
                  sinks_ref, xfirst_ref, xnext_ref, kblk_ref, vt_ref, win32_ref, wout32_ref,
                  nmix_ref, qgain_ref, kgain_ref, mqgain_ref, convw_ref, convb_ref,
                  ona_ref, onc_ref, onm_ref, wg_ref, wu_ref, wd_ref,
                  out_ref, wg16_ref, wu16_ref, wd16_ref,
                  qnt_ref, qmnt_ref, kpad_ref, vtpad_ref, upad_ref, cb_ref, xres_ref, win_ref, wout_ref):
    wg16_ref[...] = wg_ref[...].astype(BF16)
    wu16_ref[...] = wu_ref[...].astype(BF16)
    wd16_ref[...] = wd_ref[...].astype(BF16)
    t = pl.program_id(0)
    slot = t % 2
    n_blocks = seq_tile // BLOCK
    first_of_seq = (t % tiles_per_seq) == 0
    stage1_refs = (win_ref, nmix_ref, qgain_ref, kgain_ref, mqgain_ref,
                   qnt_ref, qmnt_ref, kpad_ref, vtpad_ref, upad_ref, cb_ref, xres_ref)

    @pl.when(t == 0)
    def _():
        win_ref[...] = win32_ref[...].astype(BF16)
        wout_ref[...] = wout32_ref[...].astype(BF16)
        norms, dots, posts = _stage1_pieces(xfirst_ref, 0, None, stage1_refs)
        for piece in norms + [p for pair in zip(dots, posts) for p in pair]:
            piece()

    norms, dots, posts = _stage1_pieces(
        xnext_ref, 1 - slot, (slot, ((t + 1) % tiles_per_seq) == 0), stage1_refs)
    n_chains = n_blocks * N_KV_HEADS
    n_early = len(dots) - STAGE1_HELD_PIECES
    assert n_early < n_chains
    chain_fill = [[] for _ in range(n_chains)]
    chain_fill[0] = [dots[0], dots[1]]
    for i in range(1, n_early + 1):
        chain_fill[i] = [posts[i - 1]] + ([dots[i + 1]] if i + 1 < n_early else [])
    tail_dots, tail_posts = dots[n_early:], posts[n_early:]

    key = lax.broadcasted_iota(jnp.int32, (BLOCK, GQA_GROUP * BLOCK), 0)
    qry = lax.broadcasted_iota(jnp.int32, (BLOCK, GQA_GROUP * BLOCK), 1) % BLOCK
    from_prev = key > qry
    dist = jnp.where(from_prev, qry + BLOCK - key, qry - key).astype(F32)
    dist_first = jnp.where(jnp.logical_and(from_prev, first_of_seq), MASKED_DIST, dist)
    zeros_q = jnp.zeros((HEAD_DIM, GQA_GROUP * BLOCK), BF16)

    chains = [(b, g) for b in range(n_blocks) for g in range(N_KV_HEADS)]
    n_halves = seq_tile // V7X_MXU_DIM

    scores = {}

    def score_chain(b, g):
        cols = slice(b * BLOCK, (b + 1) * BLOCK)
        k_cat = kpad_ref[slot, b * BLOCK:(b + 2) * BLOCK, :]
        heads = range(g * GQA_GROUP, (g + 1) * GQA_GROUP)
        q4 = jnp.concatenate(
            [qnt_ref[slot, h * HEAD_DIM:(h + 1) * HEAD_DIM, cols] for h in heads], axis=1)
        w_q = jnp.concatenate([q4, zeros_q] if g == 0 else [zeros_q, q4], axis=0)
        scores[b, g] = jnp.dot(k_cat, w_q, preferred_element_type=F32)

    for b, g in chains[:SCORE_LOOKAHEAD]:
        score_chain(b, g)
    late = {}

    def conv_proj():
        upad = upad_ref[slot, 0:V7X_SUBLANES + seq_tile, :]
        u = upad[V7X_SUBLANES:]
        u1 = pltpu.roll(upad, 1, axis=0)[V7X_SUBLANES:]
        u2 = pltpu.roll(upad, 2, axis=0)[V7X_SUBLANES:]
        cw = convw_ref[...]
        conv = cw[0:1] * u2 + cw[1:2] * u1 + cw[2:3] * u + convb_ref[...]
        conv_n = _rms(cb_ref[slot] * conv, onc_ref[...])
        late["y_conv"] = jnp.dot(conv_n.astype(BF16), wout_ref[ATTN_WIDTH:ATTN_WIDTH + CONV_WIDTH, :],
                                 preferred_element_type=F32)

    chain_fill[CONV_PROJ_CHAIN].insert(0, conv_proj)
    mem_vt = vt_ref[0]
    mem_units = [(c, h) for c in range(n_halves) for h in range(N_MEM_HEADS)]
    mem_scores, mem_o = {}, {}

    def mem_score(c, h):
        mem_scores[c, h] = jnp.dot(kblk_ref[0, h * N_MEM:(h + 1) * N_MEM, :],
                                   qmnt_ref[slot, :, c * V7X_MXU_DIM:(c + 1) * V7X_MXU_DIM],
                                   preferred_element_type=F32)

    for unit in mem_units[:MEM_SCORE_LOOKAHEAD]:
        mem_score(*unit)

    def mem_softmax(c, h):
        s = mem_scores.pop((c, h))
        m = jnp.max(s, axis=0, keepdims=True)
        p = jnp.exp2(s - m)
        inv_l = 1.0 / jnp.sum(p, axis=0, keepdims=True)
        vt_h = mem_vt[h * HEAD_DIM:(h + 1) * HEAD_DIM]
        mem_o[c, h] = jnp.dot(vt_h, p.astype(BF16), preferred_element_type=F32) * inv_l

    def finish_rows(c):
        rows = slice(c * V7X_MXU_DIM, (c + 1) * V7X_MXU_DIM)
        blocks = range(c * V7X_MXU_DIM // BLOCK, (c + 1) * V7X_MXU_DIM // BLOCK)
        attn_rows = jnp.concatenate(
            [jnp.concatenate([row for g in range(N_KV_HEADS) for row in head_rows[b, g]], axis=0)
             for b in blocks], axis=1).T
        mem_rows = jnp.concatenate([mem_o[c, h] for h in range(N_MEM_HEADS)], axis=0).T
        y = late["y_conv"][rows] + jnp.dot(_rms(attn_rows, ona_ref[...]).astype(BF16), wout_ref[0:ATTN_WIDTH, :],
                                   preferred_element_type=F32)
        y = y + jnp.dot(_rms(mem_rows, onm_ref[...]).astype(BF16), wout_ref[ATTN_WIDTH + CONV_WIDTH:, :],
                        preferred_element_type=F32)
        out_ref[0, rows, :] = xres_ref[slot, rows, :] + y

    for piece in norms:
        piece()
    head_rows = {}
    chains_per_half = len(chains) // n_halves
    assert len(mem_units) == len(chains)
    for i, (b, g) in enumerate(chains):
        heads = range(g * GQA_GROUP, (g + 1) * GQA_GROUP)
        dist_b = dist_first if b == 0 else dist
        neg_slope = jnp.concatenate(
            [jnp.full((1, BLOCK), -(2.0 ** (-8.0 * (h + 1) / N_ATTN_HEADS)) * LOG2E, F32) for h in heads], axis=1)
        sink = jnp.concatenate([jnp.full((1, BLOCK), sinks_ref[h], F32) for h in heads], axis=1) * LOG2E
        if i + SCORE_LOOKAHEAD < len(chains):
            score_chain(*chains[i + SCORE_LOOKAHEAD])
        s2 = scores.pop((b, g))
        s = jnp.where(from_prev, s2[:BLOCK], s2[BLOCK:]) + dist_b * neg_slope
        m = jnp.maximum(jnp.max(s, axis=0, keepdims=True), sink)
        p = jnp.exp2(s - m)
        l = jnp.sum(p, axis=0, keepdims=True) + jnp.exp2(sink - m)
        p_t = jnp.concatenate(
            [jnp.where(from_prev, p, 0.0).astype(BF16), jnp.where(from_prev, 0.0, p).astype(BF16)],
            axis=0)
        for piece in chain_fill[i]:
            piece()
        vt_bg = vtpad_ref[slot, g * HEAD_DIM:(g + 1) * HEAD_DIM, b * BLOCK:(b + 2) * BLOCK]
        o_t = jnp.dot(vt_bg, p_t, preferred_element_type=F32) * (1.0 / l)
        head_rows[b, g] = [o_t[:, hh * BLOCK:(hh + 1) * BLOCK] for hh in range(GQA_GROUP)]
        if i + MEM_SCORE_LOOKAHEAD < len(mem_units):
            mem_score(*mem_units[i + MEM_SCORE_LOOKAHEAD])
        mem_softmax(*mem_units[i])
        if i == chains_per_half:
            finish_rows(0)
    for piece in tail_dots + tail_posts:
        piece()
    finish_rows(1)


def _ffn_kernel(x_ref, gain_ref, wg_ref, wu_ref, wd_ref, out_ref):
    x = x_ref[...]
    piece_rows = V7X_MXU_DIM
    first = slice(0, V7X_MXU_DIM)
    h_pieces, gate_pieces, up_pieces = [], [], []
    for r in range(0, x.shape[0], piece_rows):
        hp = _rms(x[r:r + piece_rows], gain_ref[...]).astype(BF16)
        h_pieces.append(hp)
        gate_pieces.append(jnp.dot(hp, wg_ref[:, first], preferred_element_type=F32))
        up_pieces.append(jnp.dot(hp, wu_ref[:, first], preferred_element_type=F32))
    h = jnp.concatenate(h_pieces, axis=0)
    y = x
    for c in range(wg_ref.shape[1] // V7X_MXU_DIM):
        cols = slice(c * V7X_MXU_DIM, (c + 1) * V7X_MXU_DIM)
        if c == 0:
            gate = jnp.concatenate(gate_pieces, axis=0)
            up = jnp.concatenate(up_pieces, axis=0)
        else:
            gate = jnp.dot(h, wg_ref[:, cols], preferred_element_type=F32)
            up = jnp.dot(h, wu_ref[:, cols], preferred_element_type=F32)
        act = (gate * jax.nn.sigmoid(gate) * up).astype(BF16)
        y = y + jnp.dot(act, wd_ref[cols, :], preferred_element_type=F32)
    out_ref[...] = y


def _const_spec(shape):
    return pl.BlockSpec(shape, lambda *_: (0,) * len(shape), pipeline_mode=pl.Buffered(1))


def _row(a):
    return a.reshape(1, -1)


def _layer(x, mem_blocks, lp, plan):
    seq_tile, ffn_tile, vmem_limit = plan
    batch, seq, _ = x.shape
    kblk, mem_vt = mem_blocks
    d_ff = lp["w_gate"].shape[1]

    tiles_per_seq = seq // seq_tile
    n_tiles = batch * tiles_per_seq
    slab_spec = pl.BlockSpec((D_MODEL // n_tiles, d_ff), lambda t: (t, 0))
    down_slabs = d_ff // BLOCK
    assert down_slabs <= n_tiles
    down_spec = pl.BlockSpec((BLOCK, D_MODEL), lambda t: (jnp.minimum(t, down_slabs - 1), 0))
    up_w16 = jax.ShapeDtypeStruct((D_MODEL, d_ff), BF16)
    down_w16 = jax.ShapeDtypeStruct((d_ff, D_MODEL), BF16)

    def tile_index(t):
        return (t // tiles_per_seq, t % tiles_per_seq, 0)

    mixer = pl.pallas_call(
        functools.partial(_mixer_kernel, seq_tile, tiles_per_seq),
        out_shape=(jax.ShapeDtypeStruct(x.shape, F32), up_w16, up_w16, down_w16),
        grid=(n_tiles,),
        in_specs=[
            pl.BlockSpec(memory_space=pltpu.SMEM),
            _const_spec((1, seq_tile, D_MODEL)),
            pl.BlockSpec((1, seq_tile, D_MODEL), lambda t: tile_index(jnp.minimum(t + 1, n_tiles - 1))),
            pl.BlockSpec((1, N_MEM_HEADS * N_MEM, MEM_WIDTH), lambda t: (t // tiles_per_seq, 0, 0)),
            pl.BlockSpec((1, MEM_WIDTH, N_MEM), lambda t: (t // tiles_per_seq, 0, 0)),
            _const_spec((D_MODEL, IN_PROJ_WIDTH)),
            _const_spec((D_MODEL, D_MODEL)),
            _const_spec((1, D_MODEL)),
            _const_spec((1, HEAD_DIM)),
            _const_spec((1, HEAD_DIM)),
            _const_spec((1, HEAD_DIM)),
            _const_spec((CONV_K, CONV_WIDTH)),
            _const_spec((1, CONV_WIDTH)),
            _const_spec((1, ATTN_WIDTH)),
            _const_spec((1, CONV_WIDTH)),
            _const_spec((1, MEM_WIDTH)),
            slab_spec, slab_spec, down_spec,
        ],
        out_specs=(pl.BlockSpec((1, seq_tile, D_MODEL), tile_index), slab_spec, slab_spec, down_spec),
        scratch_shapes=[
            pltpu.VMEM((2, ATTN_WIDTH, seq_tile), BF16),
            pltpu.VMEM((2, MEM_WIDTH, seq_tile), BF16),
            pltpu.VMEM((2, seq_tile + BLOCK, KV_WIDTH), BF16),
            pltpu.VMEM((2, KV_WIDTH, seq_tile + BLOCK), BF16),
            pltpu.VMEM((2, seq_tile + V7X_SUBLANES, CONV_WIDTH), F32),
            pltpu.VMEM((2, seq_tile, CONV_WIDTH), F32),
            pltpu.VMEM((2, seq_tile, D_MODEL), F32),
            pltpu.VMEM((D_MODEL, IN_PROJ_WIDTH), BF16),
            pltpu.VMEM((D_MODEL, D_MODEL), BF16),
        ],
        compiler_params=pltpu.CompilerParams(
            dimension_semantics=("arbitrary",), vmem_limit_bytes=vmem_limit),
        name="mixer",
    )
    x1, w_gate16, w_up16, w_down16 = mixer(
        lp["attn_sinks"], x, x, kblk, mem_vt, lp["w_in"], lp["w_out"],
        _row(lp["norm_mix"]), _row(lp["q_norm"]), _row(lp["k_norm"]), _row(lp["mem_q_norm"]),
        lp["conv_w"], _row(lp["conv_b"]),
        _row(lp["out_norm_attn"]), _row(lp["out_norm_conv"]), _row(lp["out_norm_mem"]),
        lp["w_gate"], lp["w_up"], lp["w_down"])

    tokens = batch * seq
    ffn = pl.pallas_call(
        _ffn_kernel,
        out_shape=jax.ShapeDtypeStruct((tokens, D_MODEL), F32),
        grid=(tokens // ffn_tile,),
        in_specs=[
            pl.BlockSpec((ffn_tile, D_MODEL), lambda i: (i, 0)),
            _const_spec((1, D_MODEL)),
            _const_spec((D_MODEL, d_ff)),
            _const_spec((D_MODEL, d_ff)),
            _const_spec((d_ff, D_MODEL)),
        ],
        out_specs=pl.BlockSpec((ffn_tile, D_MODEL), lambda i: (i, 0)),
        compiler_params=pltpu.CompilerParams(
            dimension_semantics=("arbitrary",), vmem_limit_bytes=vmem_limit),
        name="ffn",
    )
    y = ffn(x1.reshape(tokens, D_MODEL), _row(lp["norm_ffn"]), w_gate16, w_up16, w_down16)
    return y.reshape(x.shape)


def _mem_kv(mem, lp, vmem_limit):
    batch = mem.shape[0]
    seqs_per_step = 4
    steps = batch // seqs_per_step
    kblk_shape = (batch, N_MEM_HEADS * N_MEM, MEM_WIDTH)
    vt_shape = (batch, MEM_WIDTH, N_MEM)
    call = pl.pallas_call(
        _mem_kv_kernel,
        out_shape=(jax.ShapeDtypeStruct(kblk_shape, BF16), jax.ShapeDtypeStruct(vt_shape, BF16)),
        grid=(steps,),
        in_specs=[
            pl.BlockSpec((seqs_per_step, N_MEM, D_MODEL), lambda b: (b, 0, 0)),
            _const_spec((1, D_MODEL)),
            _const_spec((D_MODEL, 2 * MEM_WIDTH)),
            _const_spec((1, HEAD_DIM)),
        ],
        out_specs=(pl.BlockSpec((seqs_per_step,) + kblk_shape[1:], lambda b: (b, 0, 0)),
                   pl.BlockSpec((seqs_per_step,) + vt_shape[1:], lambda b: (b, 0, 0))),
        compiler_params=pltpu.CompilerParams(
            dimension_semantics=("arbitrary",), vmem_limit_bytes=vmem_limit),
        name="mem_kv",
    )
    return call(mem, _row(lp["norm_mem"]), lp["w_mem_kv"], _row(lp["mem_k_norm"]))


def kernel(x, mem, norm_mix, w_in, q_norm, k_norm, attn_sinks, conv_w, conv_b, norm_mem, w_mem_kv,
           mem_q_norm, mem_k_norm, out_norm_attn, out_norm_conv, out_norm_mem, w_out, norm_ffn,
           w_gate, w_up, w_down):
    plan = _plan()
    depth = w_in.shape[0]
    for l in range(depth):
        lp = dict(
            norm_mix=norm_mix[l], w_in=w_in[l], q_norm=q_norm[l], k_norm=k_norm[l],
            attn_sinks=attn_sinks[l], conv_w=conv_w[l], conv_b=conv_b[l], norm_mem=norm_mem[l],
            w_mem_kv=w_mem_kv[l], mem_q_norm=mem_q_norm[l], mem_k_norm=mem_k_norm[l],
            out_norm_attn=out_norm_attn[l], out_norm_conv=out_norm_conv[l],
            out_norm_mem=out_norm_mem[l], w_out=w_out[l], norm_ffn=norm_ffn[l],
            w_gate=w_gate[l], w_up=w_up[l], w_down=w_down[l])
        x = _layer(x, _mem_kv(mem, lp, plan[2]), lp, plan)
    return x
```

```python
import functools

import jax
import jax.numpy as jnp
from jax import lax
from jax.experimental import pallas as pl
from jax.experimental.pallas import tpu as pltpu

D_MODEL = 1024
HEAD_DIM = 64
N_ATTN_HEADS = 8
N_KV_HEADS = 2
GQA_GROUP = N_ATTN_HEADS // N_KV_HEADS
BLOCK = 128
N_MEM_HEADS = 4
N_MEM = 256
CONV_K = 3
ATTN_WIDTH = N_ATTN_HEADS * HEAD_DIM
KV_WIDTH = N_KV_HEADS * HEAD_DIM
CONV_WIDTH = 256
MEM_WIDTH = N_MEM_HEADS * HEAD_DIM
IN_PROJ_WIDTH = ATTN_WIDTH + 2 * KV_WIDTH + 3 * CONV_WIDTH + MEM_WIDTH
EPS = 1e-6
LOG2E = 1.4426950408889634
MASKED_DIST = 2.0 ** 110

V7X_VMEM_BYTES = 64 * 1024 * 1024
V7X_SUBLANES = 8
V7X_LANES = 128
V7X_MXU_DIM = 256

F32 = jnp.float32
BF16 = jnp.bfloat16


def _plan():
    seq_tile = 4 * BLOCK
    ffn_tile = 1024
    vmem_limit = V7X_VMEM_BYTES - 8 * 1024 * 1024
    return seq_tile, ffn_tile, vmem_limit


def _rms(a, gain):
    return a * lax.rsqrt(jnp.mean(a * a, axis=-1, keepdims=True) + EPS) * gain


def _head_rms_scale(t, gmat):
    sq = (t * t).astype(BF16)
    ss = jnp.dot(sq, gmat, preferred_element_type=F32)
    return lax.rsqrt(ss * (1.0 / HEAD_DIM) + EPS)


def _head_block_ones():
    row = lax.broadcasted_iota(jnp.int32, (V7X_MXU_DIM, V7X_MXU_DIM), 0) // HEAD_DIM
    col = lax.broadcasted_iota(jnp.int32, (V7X_MXU_DIM, V7X_MXU_DIM), 1) // HEAD_DIM
    return jnp.where(row == col, 1.0, 0.0).astype(BF16)


def _head_gain_column(gain_row, heads):
    two_heads = jnp.concatenate([gain_row, gain_row], axis=1)
    col = jnp.broadcast_to(two_heads, (BLOCK, BLOCK)).T
    return jnp.concatenate([col] * (heads * HEAD_DIM // BLOCK), axis=0)


def _head_rms_rows(t, gain):
    heads = t.shape[0] // HEAD_DIM
    out = []
    for h in range(heads):
        th = t[h * HEAD_DIM:(h + 1) * HEAD_DIM]
        ss = jnp.sum(th * th, axis=0, keepdims=True)
        out.append(th * lax.rsqrt(ss * (1.0 / HEAD_DIM) + EPS) * gain[h * HEAD_DIM:(h + 1) * HEAD_DIM])
    return out


def _mem_kv_kernel(mem_ref, gain_ref, w_ref, kgain_ref, kblk_ref, vt_ref):
    n_seq = mem_ref.shape[0]
    m = mem_ref[...].reshape(n_seq * N_MEM, D_MODEL)
    mn = _rms(m, gain_ref[...]).astype(BF16)
    kv = jnp.dot(mn, w_ref[...].astype(BF16), preferred_element_type=F32)
    k = kv[:, :MEM_WIDTH]
    kgain = jnp.concatenate([kgain_ref[...]] * N_MEM_HEADS, axis=1)
    kn = k * _head_rms_scale(k, _head_block_ones()) * kgain
    lane_head = lax.broadcasted_iota(jnp.int32, (N_MEM, MEM_WIDTH), 1) // HEAD_DIM
    for s in range(n_seq):
        rows = slice(s * N_MEM, (s + 1) * N_MEM)
        kn_s = kn[rows]
        vt_ref[s] = kv[rows, MEM_WIDTH:].T.astype(BF16)
        for h in range(N_MEM_HEADS):
            kblk_ref[s, h * N_MEM:(h + 1) * N_MEM, :] = jnp.where(lane_head == h, kn_s, 0.0).astype(BF16)


ROW_CHUNK = V7X_MXU_DIM
COL_GROUP = 2 * V7X_MXU_DIM
MEM_SCORE_LOOKAHEAD = 5
SCORE_LOOKAHEAD = 4
STAGE1_HELD_PIECES = 2
CONV_PROJ_CHAIN = 3


def _stage1_pieces(x_ref, wslot, carry, refs):
    (win_ref, nmix_ref, qgain_ref, kgain_ref, mqgain_ref,
     qnt_ref, qmnt_ref, kpad_ref, vtpad_ref, upad_ref, cb_ref, xres_ref) = refs
    seq_tile = cb_ref.shape[1]
    xn, val = {}, {}
    conv_base = ATTN_WIDTH + 2 * KV_WIDTH

    def carried(ref_slice_fn, shape, dtype):
        if carry is None:
            return jnp.zeros(shape, dtype)
        prev_slot, first_of_seq = carry
        return jnp.where(first_of_seq, jnp.zeros(shape, dtype), ref_slice_fn(prev_slot))

    def heads_t(t, gain_ref, heads, out_ref, r):
        t = t.T
        gain = _head_gain_column(gain_ref[...] * (HEAD_DIM ** -0.5 * LOG2E), heads)
        gain = jnp.concatenate([gain] * (ROW_CHUNK // BLOCK), axis=1)
        out_ref[wslot, :, r * ROW_CHUNK:(r + 1) * ROW_CHUNK] = jnp.concatenate(
            _head_rms_rows(t, gain), axis=0).astype(BF16)

    norms, dots, posts, narrow_dots, narrow_posts = [], [], [], [], []
    for r in range(seq_tile // ROW_CHUNK):
        rows = slice(r * ROW_CHUNK, (r + 1) * ROW_CHUNK)

        def norm(r=r, rows=rows):
            x = x_ref[0, rows, :]
            xres_ref[wslot, rows, :] = x
            xn[r] = _rms(x, nmix_ref[...]).astype(BF16)

        def dot_piece(name, start, width, r=r):
            def run():
                val[name, r] = jnp.dot(xn[r], win_ref[:, start:start + width], preferred_element_type=F32)
            return run

        def post_q(r=r):
            heads_t(val.pop(("q", r)), qgain_ref, N_ATTN_HEADS, qnt_ref, r)

        def post_kv(r=r):
            t = val.pop(("kvch", r))
            kv_a = t[:, :2 * KV_WIDTH]
            val["ch", r] = t[:, 2 * KV_WIDTH:]
            scale = _head_rms_scale(kv_a, _head_block_ones())
            kgain = jnp.concatenate([kgain_ref[...]] * N_KV_HEADS, axis=1)
            kn = kv_a[:, :KV_WIDTH] * scale[:, :KV_WIDTH] * kgain
            kpad_ref[wslot, BLOCK + r * ROW_CHUNK:BLOCK + (r + 1) * ROW_CHUNK, :] = kn.astype(BF16)
            vtpad_ref[wslot, :, BLOCK + r * ROW_CHUNK:BLOCK + (r + 1) * ROW_CHUNK] = (
                kv_a[:, KV_WIDTH:].T.astype(BF16))
            if r == 0:
                kpad_ref[wslot, 0:BLOCK, :] = carried(
                    lambda s: kpad_ref[s, seq_tile:seq_tile + BLOCK, :], (BLOCK, KV_WIDTH), BF16)
                vtpad_ref[wslot, :, 0:BLOCK] = carried(
                    lambda s: vtpad_ref[s, :, seq_tile:seq_tile + BLOCK], (KV_WIDTH, BLOCK), BF16)

        def post_conv(r=r):
            t = val.pop(("cbcc", r))
            cb_ref[wslot, r * ROW_CHUNK:(r + 1) * ROW_CHUNK, :] = t[:, :CONV_WIDTH]
            u = t[:, CONV_WIDTH:] * val.pop(("ch", r))
            upad_ref[wslot, V7X_SUBLANES + r * ROW_CHUNK:V7X_SUBLANES + (r + 1) * ROW_CHUNK, :] = u
            if r == 0:
                upad_ref[wslot, 0:V7X_SUBLANES, :] = carried(
                    lambda s: upad_ref[s, seq_tile:seq_tile + V7X_SUBLANES, :],
                    (V7X_SUBLANES, CONV_WIDTH), F32)

        def post_qm(r=r):
            heads_t(val.pop(("qm", r)), mqgain_ref, N_MEM_HEADS, qmnt_ref, r)

        norms.append(norm)
        dots += [dot_piece("q", 0, COL_GROUP), dot_piece("kvch", ATTN_WIDTH, COL_GROUP),
                 dot_piece("cbcc", conv_base + CONV_WIDTH, COL_GROUP)]
        posts += [post_q, post_kv, post_conv]
        narrow_dots.append(dot_piece("qm", IN_PROJ_WIDTH - MEM_WIDTH, MEM_WIDTH))
        narrow_posts.append(post_qm)
    return norms, dots + narrow_dots, posts + narrow_posts


def _mixer_kernel(seq_tile, tiles_per_seq,
                  sinks_ref, xfirst_ref, xnext_ref, kblk_ref, vt_ref, win32_ref, wout32_ref,
                  nmix_ref, qgain_ref, kgain_ref, mqgain_ref, convw_ref, convb_ref,
                  ona_ref, onc_ref, onm_ref, wg_ref, wu_ref, wd_ref,
                  out_ref, wg16_ref, wu16_ref, wd16_ref,
                  qnt_ref, qmnt_ref, kpad_ref, vtpad_ref, upad_ref, cb_ref, xres_ref, win_ref, wout_ref):
    wg16_ref[...] = wg_ref[...].astype(BF16)
    wu16_ref[...] = wu_ref[...].astype(BF16)
    wd16_ref[...] = wd_ref[...].astype(BF16)
    t = pl.program_id(0)
    slot = t % 2
    n_blocks = seq_tile // BLOCK
    first_of_seq = (t % tiles_per_seq) == 0
    stage1_refs = (win_ref, nmix_ref, qgain_ref, kgain_ref, mqgain_ref,
                   qnt_ref, qmnt_ref, kpad_ref, vtpad_ref, upad_ref, cb_ref, xres_ref)

    @pl.when(t == 0)
    def _():
        win_ref[...] = win32_ref[...].astype(BF16)
        wout_ref[...] = wout32_ref[...].astype(BF16)
        norms, dots, posts = _stage1_pieces(xfirst_ref, 0, None, stage1_refs)
        for piece in norms + [p for pair in zip(dots, posts) for p in pair]:
            piece()

    norms, dots, posts = _stage1_pieces(
        xnext_ref, 1 - slot, (slot, ((t + 1) % tiles_per_seq) == 0), stage1_refs)
    n_chains = n_blocks * N_KV_HEADS
    n_early = len(dots) - STAGE1_HELD_PIECES
    assert n_early < n_chains
    chain_fill = [[] for _ in range(n_chains)]
    chain_fill[0] = [dots[0], dots[1]]
    for i in range(1, n_early + 1):
        chain_fill[i] = [posts[i - 1]] + ([dots[i + 1]] if i + 1 < n_early else [])
    tail_dots, tail_posts = dots[n_early:], posts[n_early:]

    key = lax.broadcasted_iota(jnp.int32, (BLOCK, GQA_GROUP * BLOCK), 0)
    qry = lax.broadcasted_iota(jnp.int32, (BLOCK, GQA_GROUP * BLOCK), 1) % BLOCK
    from_prev = key > qry
    dist = jnp.where(from_prev, qry + BLOCK - key, qry - key).astype(F32)
    dist_first = jnp.where(jnp.logical_and(from_prev, first_of_seq), MASKED_DIST, dist)
    zeros_q = jnp.zeros((HEAD_DIM, GQA_GROUP * BLOCK), BF16)

    chains = [(b, g) for b in range(n_blocks) for g in range(N_KV_HEADS)]
    n_halves = seq_tile // V7X_MXU_DIM

    scores = {}

    def score_chain(b, g):
        cols = slice(b * BLOCK, (b + 1) * BLOCK)
        k_cat = kpad_ref[slot, b * BLOCK:(b + 2) * BLOCK, :]
        heads = range(g * GQA_GROUP, (g + 1) * GQA_GROUP)
        q4 = jnp.concatenate(
            [qnt_ref[slot, h * HEAD_DIM:(h + 1) * HEAD_DIM, cols] for h in heads], axis=1)
        w_q = jnp.concatenate([q4, zeros_q] if g == 0 else [zeros_q, q4], axis=0)
        scores[b, g] = jnp.dot(k_cat, w_q, preferred_element_type=F32)

    for b, g in chains[:SCORE_LOOKAHEAD]:
        score_chain(b, g)
    late = {}

    def conv_proj():
        upad = upad_ref[slot, 0:V7X_SUBLANES + seq_tile, :]
        u = upad[V7X_SUBLANES:]
        u1 = pltpu.roll(upad, 1, axis=0)[V7X_SUBLANES:]
        u2 = pltpu.roll(upad, 2, axis=0)[V7X_SUBLANES:]
        cw = convw_ref[...]
        conv = cw[0:1] * u2 + cw[1:2] * u1 + cw[2:3] * u + convb_ref[...]
        conv_n = _rms(cb_ref[slot] * conv, onc_ref[...])
        late["y_conv"] = jnp.dot(conv_n.astype(BF16), wout_ref[ATTN_WIDTH:ATTN_WIDTH + CONV_WIDTH, :],
                                 preferred_element_type=F32)

    chain_fill[CONV_PROJ_CHAIN].insert(0, conv_proj)
    mem_vt = vt_ref[0]
    mem_units = [(c, h) for c in range(n_halves) for h in range(N_MEM_HEADS)]
    mem_scores, mem_o = {}, {}

    def mem_score(c, h):
        mem_scores[c, h] = jnp.dot(kblk_ref[0, h * N_MEM:(h + 1) * N_MEM, :],
                                   qmnt_ref[slot, :, c * V7X_MXU_DIM:(c + 1) * V7X_MXU_DIM],
                                   preferred_element_type=F32)

    for unit in mem_units[:MEM_SCORE_LOOKAHEAD]:
        mem_score(*unit)

    def mem_softmax(c, h):
        s = mem_scores.pop((c, h))
        m = jnp.max(s, axis=0, keepdims=True)
        p = jnp.exp2(s - m)
        inv_l = 1.0 / jnp.sum(p, axis=0, keepdims=True)
        vt_h = mem_vt[h * HEAD_DIM:(h + 1) * HEAD_DIM]
        mem_o[c, h] = jnp.dot(vt_h, p.astype(BF16), preferred_element_type=F32) * inv_l

    def finish_rows(c):
        rows = slice(c * V7X_MXU_DIM, (c + 1) * V7X_MXU_DIM)
        blocks = range(c * V7X_MXU_DIM // BLOCK, (c + 1) * V7X_MXU_DIM // BLOCK)
        attn_rows = jnp.concatenate(
            [jnp.concatenate([row for g in range(N_KV_HEADS) for row in head_rows[b, g]], axis=0)
             for b in blocks], axis=1).T
        mem_rows = jnp.concatenate([mem_o[c, h] for h in range(N_MEM_HEADS)], axis=0).T
        y = late["y_conv"][rows] + jnp.dot(_rms(attn_rows, ona_ref[...]).astype(BF16), wout_ref[0:ATTN_WIDTH, :],
                                   preferred_element_type=F32)
        y = y + jnp.dot(_rms(mem_rows, onm_ref[...]).astype(BF16), wout_ref[ATTN_WIDTH + CONV_WIDTH:, :],
                        preferred_element_type=F32)
        out_ref[0, rows, :] = xres_ref[slot, rows, :] + y

    for piece in norms:
        piece()
    head_rows = {}
    chains_per_half = len(chains) // n_halves
    assert len(mem_units) == len(chains)
    for i, (b, g) in enumerate(chains):
        heads = range(g * GQA_GROUP, (g + 1) * GQA_GROUP)
        dist_b = dist_first if b == 0 else dist
        neg_slope = jnp.concatenate(
            [jnp.full((1, BLOCK), -(2.0 ** (-8.0 * (h + 1) / N_ATTN_HEADS)) * LOG2E, F32) for h in heads], axis=1)
        sink = jnp.concatenate([jnp.full((1, BLOCK), sinks_ref[h], F32) for h in heads], axis=1) * LOG2E
        if i + SCORE_LOOKAHEAD < len(chains):
            score_chain(*chains[i + SCORE_LOOKAHEAD])
        s2 = scores.pop((b, g))
        s = jnp.where(from_prev, s2[:BLOCK], s2[BLOCK:]) + dist_b * neg_slope
        m = jnp.maximum(jnp.max(s, axis=0, keepdims=True), sink)
        p = jnp.exp2(s - m)
        l = jnp.sum(p, axis=0, keepdims=True) + jnp.exp2(sink - m)
        p_t = jnp.concatenate(
            [jnp.where(from_prev, p, 0.0).astype(BF16), jnp.where(from_prev, 0.0, p).astype(BF16)],
            axis=0)
        for piece in chain_fill[i]:
            piece()
        vt_bg = vtpad_ref[slot, g * HEAD_DIM:(g + 1) * HEAD_DIM, b * BLOCK:(b + 2) * BLOCK]
        o_t = jnp.dot(vt_bg, p_t, preferred_element_type=F32) * (1.0 / l)
        head_rows[b, g] = [o_t[:, hh * BLOCK:(hh + 1) * BLOCK] for hh in range(GQA_GROUP)]
        if i + MEM_SCORE_LOOKAHEAD < len(mem_units):
            mem_score(*mem_units[i + MEM_SCORE_LOOKAHEAD])
        mem_softmax(*mem_units[i])
        if i == chains_per_half:
            finish_rows(0)
    for piece in tail_dots + tail_posts:
        piece()
    finish_rows(1)


def _ffn_kernel(x_ref, gain_ref, wg_ref, wu_ref, wd_ref, out_ref):
    x = x_ref[...]
    piece_rows = V7X_MXU_DIM
    first = slice(0, V7X_MXU_DIM)
    h_pieces, gate_pieces, up_pieces = [], [], []
    for r in range(0, x.shape[0], piece_rows):
        hp = _rms(x[r:r + piece_rows], gain_ref[...]).astype(BF16)
        h_pieces.append(hp)
        gate_pieces.append(jnp.dot(hp, wg_ref[:, first], preferred_element_type=F32))
        up_pieces.append(jnp.dot(hp, wu_ref[:, first], preferred_element_type=F32))
    h = jnp.concatenate(h_pieces, axis=0)
    y = x
    for c in range(wg_ref.shape[1] // V7X_MXU_DIM):
        cols = slice(c * V7X_MXU_DIM, (c + 1) * V7X_MXU_DIM)
        if c == 0:
            gate = jnp.concatenate(gate_pieces, axis=0)
            up = jnp.concatenate(up_pieces, axis=0)
        else:
            gate = jnp.dot(h, wg_ref[:, cols], preferred_element_type=F32)
            up = jnp.dot(h, wu_ref[:, cols], preferred_element_type=F32)
        act = (gate * jax.nn.sigmoid(gate) * up).astype(BF16)
        y = y + jnp.dot(act, wd_ref[cols, :], preferred_element_type=F32)
    out_ref[...] = y


def _const_spec(shape):
    return pl.BlockSpec(shape, lambda *_: (0,) * len(shape), pipeline_mode=pl.Buffered(1))


def _row(a):
    return a.reshape(1, -1)


def _layer(x, mem_blocks, lp, plan):
    seq_tile, ffn_tile, vmem_limit = plan
    batch, seq, _ = x.shape
    kblk, mem_vt = mem_blocks
    d_ff = lp["w_gate"].shape[1]

    tiles_per_seq = seq // seq_tile
    n_tiles = batch * tiles_per_seq
    slab_spec = pl.BlockSpec((D_MODEL // n_tiles, d_ff), lambda t: (t, 0))
    down_slabs = d_ff // BLOCK
    assert down_slabs <= n_tiles
    down_spec = pl.BlockSpec((BLOCK, D_MODEL), lambda t: (jnp.minimum(t, down_slabs - 1), 0))
    up_w16 = jax.ShapeDtypeStruct((D_MODEL, d_ff), BF16)
    down_w16 = jax.ShapeDtypeStruct((d_ff, D_MODEL), BF16)

    def tile_index(t):
        return (t // tiles_per_seq, t % tiles_per_seq, 0)

    mixer = pl.pallas_call(
        functools.partial(_mixer_kernel, seq_tile, tiles_per_seq),
        out_shape=(jax.ShapeDtypeStruct(x.shape, F32), up_w16, up_w16, down_w16),
        grid=(n_tiles,),
        in_specs=[
            pl.BlockSpec(memory_space=pltpu.SMEM),
            _const_spec((1, seq_tile, D_MODEL)),
            pl.BlockSpec((1, seq_tile, D_MODEL), lambda t: tile_index(jnp.minimum(t + 1, n_tiles - 1))),
            pl.BlockSpec((1, N_MEM_HEADS * N_MEM, MEM_WIDTH), lambda t: (t // tiles_per_seq, 0, 0)),
            pl.BlockSpec((1, MEM_WIDTH, N_MEM), lambda t: (t // tiles_per_seq, 0, 0)),
            _const_spec((D_MODEL, IN_PROJ_WIDTH)),
            _const_spec((D_MODEL, D_MODEL)),
            _const_spec((1, D_MODEL)),
            _const_spec((1, HEAD_DIM)),
            _const_spec((1, HEAD_DIM)),
            _const_spec((1, HEAD_DIM)),
            _const_spec((CONV_K, CONV_WIDTH)),
            _const_spec((1, CONV_WIDTH)),
            _const_spec((1, ATTN_WIDTH)),
            _const_spec((1, CONV_WIDTH)),
            _const_spec((1, MEM_WIDTH)),
            slab_spec, slab_spec, down_spec,
        ],
        out_specs=(pl.BlockSpec((1, seq_tile, D_MODEL), tile_index), slab_spec, slab_spec, down_spec),
        scratch_shapes=[
            pltpu.VMEM((2, ATTN_WIDTH, seq_tile), BF16),
            pltpu.VMEM((2, MEM_WIDTH, seq_tile), BF16),
            pltpu.VMEM((2, seq_tile + BLOCK, KV_WIDTH), BF16),
            pltpu.VMEM((2, KV_WIDTH, seq_tile + BLOCK), BF16),
            pltpu.VMEM((2, seq_tile + V7X_SUBLANES, CONV_WIDTH), F32),
            pltpu.VMEM((2, seq_tile, CONV_WIDTH), F32),
            pltpu.VMEM((2, seq_tile, D_MODEL), F32),
            pltpu.VMEM((D_MODEL, IN_PROJ_WIDTH), BF16),
            pltpu.VMEM((D_MODEL, D_MODEL), BF16),
        ],
        compiler_params=pltpu.CompilerParams(
            dimension_semantics=("arbitrary",), vmem_limit_bytes=vmem_limit),
        name="mixer",
    )
    x1, w_gate16, w_up16, w_down16 = mixer(
        lp["attn_sinks"], x, x, kblk, mem_vt, lp["w_in"], lp["w_out"],
        _row(lp["norm_mix"]), _row(lp["q_norm"]), _row(lp["k_norm"]), _row(lp["mem_q_norm"]),
        lp["conv_w"], _row(lp["conv_b"]),
        _row(lp["out_norm_attn"]), _row(lp["out_norm_conv"]), _row(lp["out_norm_mem"]),
        lp["w_gate"], lp["w_up"], lp["w_down"])

    tokens = batch * seq
    ffn = pl.pallas_call(
        _ffn_kernel,
        out_shape=jax.ShapeDtypeStruct((tokens, D_MODEL), F32),
        grid=(tokens // ffn_tile,),
        in_specs=[
            pl.BlockSpec((ffn_tile, D_MODEL), lambda i: (i, 0)),
            _const_spec((1, D_MODEL)),
            _const_spec((D_MODEL, d_ff)),
            _const_spec((D_MODEL, d_ff)),
            _const_spec((d_ff, D_MODEL)),
        ],
        out_specs=pl.BlockSpec((ffn_tile, D_MODEL), lambda i: (i, 0)),
        compiler_params=pltpu.CompilerParams(
            dimension_semantics=("arbitrary",), vmem_limit_bytes=vmem_limit),
        name="ffn",
    )
    y = ffn(x1.reshape(tokens, D_MODEL), _row(lp["norm_ffn"]), w_gate16, w_up16, w_down16)
    return y.reshape(x.shape)


def _mem_kv(mem, lp, vmem_limit):
    batch = mem.shape[0]
    seqs_per_step = 4
    steps = batch // seqs_per_step
    kblk_shape = (batch, N_MEM_HEADS * N_MEM, MEM_WIDTH)
    vt_shape = (batch, MEM_WIDTH, N_MEM)
    call = pl.pallas_call(
        _mem_kv_kernel,
        out_shape=(jax.ShapeDtypeStruct(kblk_shape, BF16), jax.ShapeDtypeStruct(vt_shape, BF16)),
        grid=(steps,),
        in_specs=[
            pl.BlockSpec((seqs_per_step, N_MEM, D_MODEL), lambda b: (b, 0, 0)),
            _const_spec((1, D_MODEL)),
            _const_spec((D_MODEL, 2 * MEM_WIDTH)),
            _const_spec((1, HEAD_DIM)),
        ],
        out_specs=(pl.BlockSpec((seqs_per_step,) + kblk_shape[1:], lambda b: (b, 0, 0)),
                   pl.BlockSpec((seqs_per_step,) + vt_shape[1:], lambda b: (b, 0, 0))),
        compiler_params=pltpu.CompilerParams(
            dimension_semantics=("arbitrary",), vmem_limit_bytes=vmem_limit),
        name="mem_kv",
    )
    return call(mem, _row(lp["norm_mem"]), lp["w_mem_kv"], _row(lp["mem_k_norm"]))


def kernel(x, mem, norm_mix, w_in, q_norm, k_norm, attn_sinks, conv_w, conv_b, norm_mem, w_mem_kv,
           mem_q_norm, mem_k_norm, out_norm_attn, out_norm_conv, out_norm_mem, w_out, norm_ffn,
           w_gate, w_up, w_down):
    plan = _plan()
    depth = w_in.shape[0]
    for l in range(depth):
        lp = dict(
            norm_mix=norm_mix[l], w_in=w_in[l], q_norm=q_norm[l], k_norm=k_norm[l],
            attn_sinks=attn_sinks[l], conv_w=conv_w[l], conv_b=conv_b[l], norm_mem=norm_mem[l],
            w_mem_kv=w_mem_kv[l], mem_q_norm=mem_q_norm[l], mem_k_norm=mem_k_norm[l],
            out_norm_attn=out_norm_attn[l], out_norm_conv=out_norm_conv[l],
            out_norm_mem=out_norm_mem[l], w_out=w_out[l], norm_ffn=norm_ffn[l],
            w_gate=w_gate[l], w_up=w_up[l], w_down=w_down[l])
        x = _layer(x, _mem_kv(mem, lp, plan[2]), lp, plan)
    return x
```
